```python
import math
import jax, jax.numpy as jnp
from jax import lax
import numpy as np

D_MODEL = 1024
BATCH = 16
SEQ = 4096
DEPTH = 2
DEC_BATCH = 2
DEC_SEQ = 8192
PAST_LEN = 128

A_HEADS = 8
A_DK = 128
A_DV = D_MODEL // A_HEADS
A_WK = A_HEADS * A_DK
A_WV = A_HEADS * A_DV
R_HEADS = 8
R_DK = 64
R_DV = 2 * R_DK
R_WK = R_HEADS * R_DK
R_WV = R_HEADS * R_DV
CHUNK = 64
ROPE_BASE = 10000.0
LN_EPS = 1e-5
LOG_FLOOR = 1e-30
NEG_BIG = -1e30
DEEPNORM_ALPHA = (2.0 * DEPTH) ** 0.25
DEEPNORM_BETA = (8.0 * DEPTH) ** -0.25
IN_SEGMENTS = [A_WK, A_WK, A_WK, A_WV, A_WV, R_WK, R_WK, R_WV, R_WV]
VALUE_SEGMENTS = (3, 7)
IN_WIDTH = sum(IN_SEGMENTS)
IN_OFFSETS = [int(v) for v in np.cumsum(IN_SEGMENTS)[:-1]]

kernel_name = 'hybrid_hgrn2_retention_encoder'

F32 = jnp.float32


def _flip(a):
    return jnp.flip(a, axis=1)


def _layer_norm(h):
    h = h.astype(F32)
    mu = jnp.mean(h, axis=-1, keepdims=True)
    var = jnp.mean(jnp.square(h - mu), axis=-1, keepdims=True)
    return (h - mu) * lax.rsqrt(var + LN_EPS)


def _rms_norm(o):
    return o * lax.rsqrt(jnp.mean(jnp.square(o), axis=-1, keepdims=True) + LN_EPS)


def _rope(T):
    pos = jnp.arange(T, dtype=F32)
    inv = ROPE_BASE ** (-jnp.arange(0, R_DK, 2, dtype=F32) / R_DK)
    ang = pos[:, None] * inv[None, :]
    return jnp.cos(ang)[:, None, :], jnp.sin(ang)[:, None, :]


def _apply_rope(x, cos, sin):
    x1, x2 = jnp.split(x, 2, axis=-1)
    return jnp.concatenate([x1 * cos - x2 * sin, x1 * sin + x2 * cos], axis=-1)


def _hgrn2_dir(q, k, logf, v):
    B_, T, H, dk = q.shape
    dv = v.shape[-1]
    N = T // CHUNK

    def chunks(a):
        return a.reshape(B_, N, CHUNK, H, a.shape[-1]).transpose(1, 0, 3, 2, 4)

    causal = jnp.tril(jnp.ones((CHUNK, CHUNK), dtype=bool))[:, :, None]

    def step(S, inp):
        qc, kc, fc, vc = inp
        b = jnp.cumsum(fc, axis=2)
        b_last = b[:, :, -1:, :]
        inter = jnp.einsum('bhtk,bhkv->bhtv', qc * jnp.exp(b), S)
        decay = jnp.exp(jnp.where(causal, b[:, :, :, None, :] - b[:, :, None, :, :], NEG_BIG))
        attn = jnp.einsum('bhtk,bhtsk,bhsk->bhts', qc, decay, kc)
        intra = jnp.einsum('bhts,bhsv->bhtv', attn, vc)
        S_new = jnp.exp(b_last[:, :, 0, :, None]) * S + jnp.einsum('bhsk,bhsv->bhkv', kc * jnp.exp(b_last - b), vc)
        return S_new, inter + intra

    S0 = jnp.zeros((B_, H, dk, dv), F32)
    _, o = lax.scan(step, S0, (chunks(q), chunks(k), chunks(logf), chunks(v)))
    return o.transpose(1, 0, 3, 2, 4).reshape(B_, T, H, dv)


def _retention_dir(q, k, v, lg):
    B_, T, H, dk = q.shape
    dv = v.shape[-1]
    N = T // CHUNK
    q = q.reshape(B_, N, CHUNK, H, dk)
    k = k.reshape(B_, N, CHUNK, H, dk)
    v = v.reshape(B_, N, CHUNK, H, dv)
    pos = jnp.arange(CHUNK, dtype=F32)
    dist = pos[:, None] - pos[None, :]
    decay = jnp.exp(jnp.where(dist[None] >= 0, dist[None] * lg[:, None, None], NEG_BIG))
    scores = jnp.einsum('bnthd,bnshd->bnhts', q, k) * decay
    intra = jnp.einsum('bnhts,bnshe->bnthe', scores, v)
    k_dec = k * jnp.exp((CHUNK - 1 - pos)[:, None] * lg[None, :])[:, :, None]
    kv = jnp.einsum('bnshd,bnshe->nbhde', k_dec, v)
    chunk_decay = jnp.exp(CHUNK * lg)[None, :, None, None]

    def step(S, kv_n):
        return S * chunk_decay + kv_n, S

    _, S_prev = lax.scan(step, jnp.zeros((B_, H, dk, dv), F32), kv)
    q_dec = q * jnp.exp((pos + 1)[:, None] * lg[None, :])[:, :, None]
    inter = jnp.einsum('bnthd,nbhde->bnthe', q_dec, S_prev)
    return (intra + inter).reshape(B_, T, H, dv)


def _layer(x, c, cos, sin, lb, w_ada, b_ada, w_in, a_norm_w, ret_decay, w_pa, w_pb, w_mg, b_mg, w_out, ln_g, ln_b):
    dt = x.dtype
    B_, T, _ = x.shape
    ada = jax.nn.silu(c) @ w_ada + b_ada
    shift, scale, gate = jnp.split(ada, 3, axis=-1)
    u = x * (1 + scale[:, None, :]) + shift[:, None, :]
    proj = u @ w_in
    aq, af_f, af_b, ai, ag, rq, rk, rv, rg = jnp.split(proj, IN_OFFSETS, axis=-1)

    q_a = (jax.nn.silu(aq.astype(F32)) * A_DK ** -0.5).reshape(B_, T, A_HEADS, A_DK)
    v_a = ai.astype(F32).reshape(B_, T, A_HEADS, A_DV)
    lbh = lb.reshape(A_HEADS, A_DK)
    log_lb = jnp.log(jnp.maximum(lbh, LOG_FLOOR))
    log_1m_lb = jnp.log1p(-lbh)

    def forget(fl):
        fl = fl.astype(F32).reshape(B_, T, A_HEADS, A_DK)
        logf = jnp.logaddexp(log_lb, log_1m_lb + jax.nn.log_sigmoid(fl))
        kk = (1.0 - lbh) * jax.nn.sigmoid(-fl)
        return logf, kk

    logf_f, k_f = forget(af_f)
    logf_b, k_b = forget(af_b)
    o_a = _hgrn2_dir(q_a, k_f, logf_f, v_a) + _flip(_hgrn2_dir(_flip(q_a), _flip(k_b), _flip(logf_b), _flip(v_a)))
    y_a = (_rms_norm(o_a) * a_norm_w).reshape(B_, T, A_WV) * jax.nn.silu(ag.astype(F32))
    y_a = y_a.astype(dt)

    q_r = _apply_rope(rq.astype(F32).reshape(B_, T, R_HEADS, R_DK), cos, sin) * R_DK ** -0.5
    k_r = _apply_rope(rk.astype(F32).reshape(B_, T, R_HEADS, R_DK), cos, sin)
    v_r = rv.astype(F32).reshape(B_, T, R_HEADS, R_DV)
    lg = jax.nn.log_sigmoid(ret_decay.astype(F32))
    o_r = _retention_dir(q_r, k_r, v_r, lg[0]) + _flip(_retention_dir(_flip(q_r), _flip(k_r), _flip(v_r), lg[1]))
    o_r = _layer_norm(o_r)
    y_r = (o_r.reshape(B_, T, R_WV) * jax.nn.silu(rg.astype(F32))).astype(dt)

    p_a = y_a @ w_pa
    p_r = y_r @ w_pb
    g_a, g_r = jnp.split(jax.nn.sigmoid(u @ w_mg + b_mg), 2, axis=-1)
    s = (g_a * p_a + g_r * p_r) @ w_out
    h = DEEPNORM_ALPHA * x + (1 + gate[:, None, :]) * s
    return (_layer_norm(h) * ln_g + ln_b).astype(dt)


def _trunk(x, c, lbs, w_ada, b_ada, w_in, a_norm_w, ret_decay, w_pa, w_pb, w_mg, b_mg, w_out, ln_g, ln_b):
    cos, sin = _rope(x.shape[1])
    for l in range(DEPTH):
        x = _layer(x, c, cos, sin, lbs[l], w_ada[l], b_ada[l], w_in[l], a_norm_w[l], ret_decay[l],
                   w_pa[l], w_pb[l], w_mg[l], b_mg[l], w_out[l], ln_g[l], ln_b[l])
    return x


def setup_inputs(seed: int = 0) -> dict:
    key = jax.random.key(seed)
    ks = jax.random.split(key, 24)

    def nrm(k, shape, s):
        return jax.random.normal(k, shape, F32) * s

    D = D_MODEL
    seg_keys = jax.random.split(ks[6], len(IN_SEGMENTS))
    w_in = jnp.concatenate([
        nrm(seg_keys[j], (DEPTH, D, w), D ** -0.5 * (DEEPNORM_BETA if j in VALUE_SEGMENTS else 1.0))
        for j, w in enumerate(IN_SEGMENTS)], axis=-1)
    eps = 2.0 ** -(5.0 + jnp.arange(R_HEADS, dtype=F32))
    decay_logit = jnp.log1p(-eps) - jnp.log(eps)
    return {
        'x_prompt': nrm(ks[0], (BATCH, SEQ, D), 1.0),
        'x_sample': nrm(ks[1], (DEC_BATCH, DEC_SEQ, D), 1.0),
        'c_prompt': nrm(ks[2], (BATCH, D), 1.0),
        'c_sample': nrm(ks[3], (DEC_BATCH, D), 1.0),
        'w_ada': nrm(ks[4], (DEPTH, D, 3 * D), 0.2 * D ** -0.5),
        'b_ada': nrm(ks[5], (DEPTH, 3 * D), 0.01),
        'w_in': w_in,
        'hgrn_lb': nrm(ks[7], (DEPTH, A_WK), 0.5),
        'a_norm_w': 1.0 + nrm(ks[8], (DEPTH, A_DV), 0.02),
        'ret_decay': decay_logit[None, None, :] + nrm(ks[9], (DEPTH, 2, R_HEADS), 0.01),
        'w_pa': nrm(ks[10], (DEPTH, A_WV, D), DEEPNORM_BETA * A_WV ** -0.5),
        'w_pb': nrm(ks[11], (DEPTH, R_WV, D), DEEPNORM_BETA * R_WV ** -0.5),
        'w_mg': nrm(ks[12], (DEPTH, D, 2 * D), D ** -0.5),
        'b_mg': nrm(ks[13], (DEPTH, 2 * D), 0.01),
        'w_out': nrm(ks[14], (DEPTH, D, D), DEEPNORM_BETA * D ** -0.5),
        'ln_g': 1.0 + nrm(ks[15], (DEPTH, D), 0.02),
        'ln_b': nrm(ks[16], (DEPTH, D), 0.01),
    }


def reference(x_prompt, x_sample, c_prompt, c_sample, w_ada, b_ada, w_in, hgrn_lb, a_norm_w, ret_decay,
              w_pa, w_pb, w_mg, b_mg, w_out, ln_g, ln_b):
    p = jax.nn.softmax(hgrn_lb.astype(F32), axis=0)
    lbs = jnp.cumsum(p, axis=0) - p[0:1]
    y_prompt = _trunk(x_prompt, c_prompt, lbs, w_ada, b_ada, w_in, a_norm_w, ret_decay,
                      w_pa, w_pb, w_mg, b_mg, w_out, ln_g, ln_b)
    y_sample = _trunk(x_sample, c_sample, lbs, w_ada, b_ada, w_in, a_norm_w, ret_decay,
                      w_pa, w_pb, w_mg, b_mg, w_out, ln_g, ln_b)
    return (y_prompt, y_sample)
```

```python
import functools

import numpy as np
import jax
import jax.numpy as jnp
from jax import lax
from jax.experimental import pallas as pl
from jax.experimental.pallas import tpu as pltpu

F32 = jnp.float32
BF16 = jnp.bfloat16

D_MODEL = 1024
DEPTH = 2
A_HEADS = 8
A_DK = 128
A_DV = 128
R_HEADS = 8
R_DK = 64
R_DV = 128
R_PAIRS = R_HEADS // 2
LN_EPS = 1e-5
LOG_FLOOR = 1e-30
NEG_BIG = -1e30
DEEPNORM_ALPHA = (2.0 * DEPTH) ** 0.25
ROPE_BASE = 10000.0

OFF_AQ, OFF_FF, OFF_FB, OFF_AI, OFF_AG, OFF_RQ, OFF_RK, OFF_RV, OFF_RG = (
    0, 1024, 2048, 3072, 4096, 5120, 5632, 6144, 7168)
IN_WIDTH = 8192

LANES = 128
HGRN_CHUNK = 64
HGRN_LEVELS = 6
RET_CHUNK = 128
IN_ROWS = 256
OUT_ROWS = 512
VMEM_LIMIT = 56 * 1024 * 1024


def _dot(a, b):
    return jnp.dot(a, b, preferred_element_type=F32)


def _dot_nt(a, b):
    return lax.dot_general(a, b, (((1,), (1,)), ((), ())), preferred_element_type=F32)


def _dot_tn(a, b):
    return lax.dot_general(a, b, (((0,), (0,)), ((), ())), preferred_element_type=F32)


def _sigmoid(z):
    return 1.0 / (1.0 + jnp.exp(-z))


def _const_spec(shape):
    nd = len(shape)
    return pl.BlockSpec(shape, lambda *_: (0,) * nd, pipeline_mode=pl.Buffered(1))


def _ada_kernel(c_ref, w_ref, b_ref, o_ref):
    c = c_ref[...]
    a = c * _sigmoid(c)
    o_ref[0] = jnp.dot(a, w_ref[0], preferred_element_type=F32,
                       precision=lax.Precision.HIGHEST) + b_ref[0]


def _ada_call(c, w_ada, b_ada):
    nb = c.shape[0]
    d = D_MODEL
    return pl.pallas_call(
        _ada_kernel,
        grid=(DEPTH, 3),
        in_specs=[
            pl.BlockSpec((nb, d), lambda l, j: (0, 0)),
            pl.BlockSpec((1, d, d), lambda l, j: (l, 0, j)),
            pl.BlockSpec((1, 1, d), lambda l, j: (l, 0, j)),
        ],
        out_specs=pl.BlockSpec((1, nb, d), lambda l, j: (l, 0, j)),
        out_shape=jax.ShapeDtypeStruct((DEPTH, nb, 3 * d), F32),
        compiler_params=pltpu.CompilerParams(
            dimension_semantics=("arbitrary", "arbitrary"), vmem_limit_bytes=VMEM_LIMIT),
        name="adaln",
    )(c, w_ada, b_ada.reshape(DEPTH, 1, 3 * d))


def _inproj_kernel(x_ref, sc_ref, sh_ref, win_ref, wmg_ref, bmg_ref, lbm_ref, oml_ref,
                   cos_ref, sin_ref,
                   qa_ref, lff_ref, lfb_ref, va_ref, ga_ref, qr_ref, kr_ref, vr_ref, gr_ref,
                   mg_ref):
    u = (x_ref[0] * sc_ref[0] + sh_ref[0]).astype(BF16)

    def proj(c0):
        return _dot(u, win_ref[:, c0:c0 + 2 * LANES])

    def halves(z):
        return z[:, :LANES], z[:, LANES:]

    for g in range(A_HEADS // 2):
        c = 2 * LANES * g
        z = proj(OFF_AQ + c)
        z = z * _sigmoid(z) * (A_DK ** -0.5)
        for i, zz in enumerate(halves(z)):
            qa_ref[0, 2 * g + i] = zz.astype(BF16)
        for off, ref in ((OFF_FF, lff_ref), (OFF_FB, lfb_ref)):
            z = proj(off + c)
            f = lbm_ref[:, c:c + 2 * LANES] + oml_ref[:, c:c + 2 * LANES] * _sigmoid(z)
            lf = jnp.log(f)
            for i, zz in enumerate(halves(lf)):
                ref[0, 2 * g + i] = zz
        z = proj(OFF_AI + c)
        for i, zz in enumerate(halves(z)):
            va_ref[0, 2 * g + i] = zz.astype(BF16)
        z = proj(OFF_AG + c)
        z = z * _sigmoid(z)
        for i, zz in enumerate(halves(z)):
            ga_ref[0, 2 * g + i] = zz.astype(BF16)
        z = proj(OFF_RV + c)
        for i, zz in enumerate(halves(z)):
            vr_ref[0, 2 * g + i] = zz.astype(BF16)
        z = proj(OFF_RG + c)
        z = z * _sigmoid(z)
        for i, zz in enumerate(halves(z)):
            gr_ref[0, 2 * g + i] = zz.astype(BF16)

    cosv = cos_ref[...]
    sinv = sin_ref[...]
    for off, ref, scale in ((OFF_RQ, qr_ref, R_DK ** -0.5), (OFF_RK, kr_ref, 1.0)):
        for g in range(R_PAIRS // 2):
            z = proj(off + 2 * LANES * g)
            for i, zz in enumerate(halves(z)):
                rot = pltpu.roll(zz, LANES // 2, 1)
                ref[0, 2 * g + i] = ((zz * cosv + rot * sinv) * scale).astype(BF16)

    for g in range(2 * D_MODEL // (2 * LANES)):
        c = 2 * LANES * g
        z = _dot(u, wmg_ref[:, c:c + 2 * LANES]) + bmg_ref[:, c:c + 2 * LANES]
        mg_ref[0, :, c:c + 2 * LANES] = _sigmoid(z).astype(BF16)


def _inproj_call(x, sc1p, shift, w_in, w_mg, b_mg, lbm, oml, cos_p, sin_p):
    B, T, D = x.shape
    rows = min(IN_ROWS, T)
    grid = (B, T // rows)
    hd = lambda n, dt: jax.ShapeDtypeStruct((B, n, T, LANES), dt)
    hspec = lambda n: pl.BlockSpec((1, n, rows, LANES), lambda b, t: (b, 0, t, 0))
    bvec = pl.BlockSpec((1, 1, D), lambda b, t: (b, 0, 0))
    return pl.pallas_call(
        _inproj_kernel,
        grid=grid,
        in_specs=[
            pl.BlockSpec((1, rows, D), lambda b, t: (b, t, 0)),
            bvec, bvec,
            _const_spec((D, IN_WIDTH)),
            _const_spec((D, 2 * D)),
            _const_spec((1, 2 * D)),
            _const_spec((1, D)),
            _const_spec((1, D)),
            pl.BlockSpec((rows, LANES), lambda b, t: (t, 0)),
            pl.BlockSpec((rows, LANES), lambda b, t: (t, 0)),
        ],
        out_specs=[hspec(A_HEADS), hspec(A_HEADS), hspec(A_HEADS), hspec(A_HEADS), hspec(A_HEADS),
                   hspec(R_PAIRS), hspec(R_PAIRS), hspec(R_HEADS), hspec(R_HEADS),
                   pl.BlockSpec((1, rows, 2 * D), lambda b, t: (b, t, 0))],
        out_shape=[hd(A_HEADS, BF16), hd(A_HEADS, F32), hd(A_HEADS, F32), hd(A_HEADS, BF16),
                   hd(A_HEADS, BF16), hd(R_PAIRS, BF16), hd(R_PAIRS, BF16), hd(R_HEADS, BF16),
                   hd(R_HEADS, BF16), jax.ShapeDtypeStruct((B, T, 2 * D), BF16)],
        compiler_params=pltpu.CompilerParams(
            dimension_semantics=("arbitrary", "arbitrary"), vmem_limit_bytes=VMEM_LIMIT),
        name="inproj",
    )(x, sc1p, shift, w_in, w_mg, b_mg, lbm, oml, cos_p, sin_p)


def _hgrn_tables(reverse):
    C = HGRN_CHUNK
    idx = np.arange(C)
    cum = (idx[None, :] >= idx[:, None]) if reverse else (idx[None, :] <= idx[:, None])
    cum = cum.astype(np.float64)
    end = 0 if reverse else C - 1
    blocks = []
    for j in range(HGRN_LEVELS):
        m = C >> (j + 1)
        base = idx - idx % (2 * m)
        ref = base + (m if reverse else m - 1)
        blocks.append(cum - cum[ref])
    blocks.append(cum)
    blocks.append(cum[end][None, :] - cum)
    tab = np.concatenate(blocks, axis=0)
    x = idx[:, None] ^ idx[None, :]
    bl = np.zeros_like(x)
    nz = x > 0
    bl[nz] = np.floor(np.log2(x[nz])).astype(np.int64) + 1
    lvl = HGRN_LEVELS - bl
    coupled = (idx[:, None] <= idx[None, :]) if reverse else (idx[:, None] >= idx[None, :])
    lvl = np.where(coupled, lvl, -1)
    return tab.astype(np.float32), lvl.astype(np.int32), end


def _hgrn_kernel(q_ref, lff_ref, lfb_ref, v_ref, g_ref, c_ref, nw_ref,
                 tabf_ref, tabb_ref, lvf_ref, lvb_ref, y_ref, o_scr, sf_scr, sb_scr, *, ends):
    C = HGRN_CHUNK
    T = q_ref.shape[2]
    n_chunks = T // C
    sf_scr[...] = jnp.zeros_like(sf_scr)
    sb_scr[...] = jnp.zeros_like(sb_scr)
    cvec = c_ref[0]
    nw = nw_ref[...]

    def chunk(ci, lf_ref, tab_ref, lv_ref, s_scr, end):
        r0 = pl.multiple_of(ci * C, C)
        rows = pl.ds(r0, C)
        lf = lf_ref[0, 0, rows, :]
        q = q_ref[0, 0, rows, :].astype(F32)
        v = v_ref[0, 0, rows, :]
        k = cvec - jnp.exp(lf)
        hi = lf.astype(BF16)
        lo = (lf - hi.astype(F32)).astype(BF16)
        r = _dot(tab_ref[...], jnp.concatenate([hi, lo], axis=1))
        dm = r[:, :LANES] + r[:, LANES:]
        lv = lv_ref[...]
        qb = q.astype(BF16)
        kb = k.astype(BF16)
        a = jnp.where(lv == HGRN_LEVELS, _dot_nt(qb, kb), 0.0)
        for j in range(HGRN_LEVELS):
            e = jnp.exp(-jnp.abs(dm[j * C:(j + 1) * C]))
            aj = _dot_nt((q * e).astype(BF16), (k * e).astype(BF16))
            a = jnp.where(lv == j, aj, a)
        bq = dm[HGRN_LEVELS * C:(HGRN_LEVELS + 1) * C]
        bk = dm[(HGRN_LEVELS + 1) * C:(HGRN_LEVELS + 2) * C]
        qi = (q * jnp.exp(bq)).astype(BF16)
        kd = (k * jnp.exp(bk)).astype(BF16)
        st = s_scr[...]
        o = _dot(a.astype(BF16), v) + _dot_nt(qi, st.astype(BF16))
        s_scr[...] = st * jnp.exp(bq[end:end + 1, :]) + _dot_tn(v, kd)
        return rows, o

    def finalize(rows, o):
        tot = o_scr[rows, :] + o
        ms = jnp.mean(tot * tot, axis=-1, keepdims=True)
        y = tot * lax.rsqrt(ms + LN_EPS) * nw * g_ref[0, 0, rows, :].astype(F32)
        y_ref[0, 0, rows, :] = y.astype(BF16)

    def first_half(i, carry):
        rows, o = chunk(i, lff_ref, tabf_ref, lvf_ref, sf_scr, ends[0])
        o_scr[rows, :] = o
        rows, o = chunk(n_chunks - 1 - i, lfb_ref, tabb_ref, lvb_ref, sb_scr, ends[1])
        o_scr[rows, :] = o
        return carry

    def second_half(i, carry):
        rows, o = chunk(i, lff_ref, tabf_ref, lvf_ref, sf_scr, ends[0])
        finalize(rows, o)
        rows, o = chunk(n_chunks - 1 - i, lfb_ref, tabb_ref, lvb_ref, sb_scr, ends[1])
        finalize(rows, o)
        return carry

    lax.fori_loop(0, n_chunks // 2, first_half, 0)
    lax.fori_loop(n_chunks // 2, n_chunks, second_half, 0)


def _hgrn_call(qa, lff, lfb, va, ga, cvec, nw):
    B, H, T, _ = qa.shape
    tabf, lvf, endf = _hgrn_tables(False)
    tabb, lvb, endb = _hgrn_tables(True)
    seq = pl.BlockSpec((1, 1, T, LANES), lambda b, h: (b, h, 0, 0))
    return pl.pallas_call(
        functools.partial(_hgrn_kernel, ends=(endf, endb)),
        grid=(B, H),
        in_specs=[seq, seq, seq, seq, seq,
                  pl.BlockSpec((1, 1, LANES), lambda b, h: (h, 0, 0)),
                  _const_spec((1, LANES)),
                  _const_spec(tabf.shape), _const_spec(tabb.shape),
                  _const_spec(lvf.shape), _const_spec(lvb.shape)],
        out_specs=seq,
        out_shape=jax.ShapeDtypeStruct((B, H, T, LANES), BF16),
        scratch_shapes=[pltpu.VMEM((T, LANES), F32),
                        pltpu.VMEM((A_DV, A_DK), F32),
                        pltpu.VMEM((A_DV, A_DK), F32)],
        compiler_params=pltpu.CompilerParams(
            dimension_semantics=("arbitrary", "arbitrary"), vmem_limit_bytes=VMEM_LIMIT),
        name="hgrn",
    )(qa, lff, lfb, va, ga, cvec, nw,
      jnp.asarray(tabf, BF16), jnp.asarray(tabb, BF16), jnp.asarray(lvf), jnp.asarray(lvb))


def _ret_kernel(q_ref, k_ref, v_ref, g_ref, lg_ref, y_ref, o_scr, s_scr):
    C = RET_CHUNK
    T = q_ref.shape[2]
    n_chunks = T // C
    s_scr[...] = jnp.zeros_like(s_scr)
    lane = lax.broadcasted_iota(jnp.int32, (1, LANES), 1)
    head_mask = [(lane % (LANES // 2)) < (LANES // 4), (lane % (LANES // 2)) >= (LANES // 4)]
    tpos = lax.broadcasted_iota(jnp.int32, (C, C), 0)
    spos = lax.broadcasted_iota(jnp.int32, (C, C), 1)
    pos = lax.broadcasted_iota(jnp.int32, (C, 1), 0).astype(F32)

    tables = []
    for d in range(2):
        dist = ((tpos - spos) if d == 0 else (spos - tpos)).astype(F32)
        after = pos if d == 0 else (C - 1.0 - pos)
        per_head = []
        for i in range(2):
            lg = lg_ref[0, 2 * d + i:2 * d + i + 1, :]
            decay = jnp.exp(jnp.where(dist >= 0, dist * lg, NEG_BIG))
            q_dec = jnp.exp((after + 1.0) * lg)
            k_dec = jnp.exp((C - 1.0 - after) * lg)
            s_dec = jnp.exp(C * lg)
            per_head.append((decay, q_dec, k_dec, s_dec))
        tables.append(per_head)

    def chunk(ci, d):
        r0 = pl.multiple_of(ci * C, C)
        rows = pl.ds(r0, C)
        q = q_ref[0, 0, rows, :]
        k = k_ref[0, 0, rows, :]
        qf = q.astype(F32)
        kf = k.astype(F32)
        outs = []
        for i in range(2):
            decay, q_dec, k_dec, s_dec = tables[d][i]
            v = v_ref[0, i, rows, :]
            qm = jnp.where(head_mask[i], qf, 0.0)
            scores = _dot_nt(qm.astype(BF16), k) * decay
            st = s_scr[2 * d + i]
            o = _dot(scores.astype(BF16), v) + _dot_nt((qm * q_dec).astype(BF16), st.astype(BF16))
            km = jnp.where(head_mask[i], kf * k_dec, 0.0).astype(BF16)
            s_scr[2 * d + i] = st * s_dec + _dot_tn(v, km)
            outs.append(o)
        return rows, outs

    def finalize(rows, outs):
        for i in range(2):
            tot = o_scr[i, rows, :] + outs[i]
            mu = jnp.mean(tot, axis=-1, keepdims=True)
            cen = tot - mu
            var = jnp.mean(cen * cen, axis=-1, keepdims=True)
            y = cen * lax.rsqrt(var + LN_EPS) * g_ref[0, i, rows, :].astype(F32)
            y_ref[0, i, rows, :] = y.astype(BF16)

    def first_half(j, carry):
        for d, ci in ((0, j), (1, n_chunks - 1 - j)):
            rows, outs = chunk(ci, d)
            for i in range(2):
                o_scr[i, rows, :] = outs[i]
        return carry

    def second_half(j, carry):
        for d, ci in ((0, j), (1, n_chunks - 1 - j)):
            rows, outs = chunk(ci, d)
            finalize(rows, outs)
        return carry

    lax.fori_loop(0, n_chunks // 2, first_half, 0)
    lax.fori_loop(n_chunks // 2, n_chunks, second_half, 0)


def _ret_call(qr, kr, vr, gr, lg_rows):
    B, P, T, _ = qr.shape
    seq1 = pl.BlockSpec((1, 1, T, LANES), lambda b, p: (b, p, 0, 0))
    seq2 = pl.BlockSpec((1, 2, T, LANES), lambda b, p: (b, p, 0, 0))
    return pl.pallas_call(
        _ret_kernel,
        grid=(B, P),
        in_specs=[seq1, seq1, seq2, seq2,
                  pl.BlockSpec((1, 4, LANES), lambda b, p: (p, 0, 0))],
        out_specs=seq2,
        out_shape=jax.ShapeDtypeStruct((B, R_HEADS, T, LANES), BF16),
        scratch_shapes=[pltpu.VMEM((2, T, LANES), F32),
                        pltpu.VMEM((4, R_DV, LANES), F32)],
        compiler_params=pltpu.CompilerParams(
            dimension_semantics=("arbitrary", "arbitrary"), vmem_limit_bytes=VMEM_LIMIT),
        name="retention",
    )(qr, kr, vr, gr, lg_rows)


def _outproj_kernel(ya_ref, yr_ref, mg_ref, x_ref, gate_ref, wpa_ref, wpb_ref, wout_ref,
                    lng_ref, lnb_ref, o_ref):
    ya = jnp.concatenate([ya_ref[0, h] for h in range(A_HEADS)], axis=1)
    yr = jnp.concatenate([yr_ref[0, h] for h in range(R_HEADS)], axis=1)
    pa = _dot(ya, wpa_ref[...])
    pr = _dot(yr, wpb_ref[...])
    mg = mg_ref[0]
    m = mg[:, :D_MODEL].astype(F32) * pa + mg[:, D_MODEL:].astype(F32) * pr
    s = _dot(m.astype(BF16), wout_ref[...])
    h = DEEPNORM_ALPHA * x_ref[0] + gate_ref[0] * s
    mu = jnp.mean(h, axis=-1, keepdims=True)
    cen = h - mu
    var = jnp.mean(cen * cen, axis=-1, keepdims=True)
    o_ref[0] = cen * lax.rsqrt(var + LN_EPS) * lng_ref[...] + lnb_ref[...]


def _outproj_call(ya, yr, mg, x, gate1p, w_pa, w_pb, w_out, ln_g, ln_b):
    B, T, D = x.shape
    rows = min(OUT_ROWS, T)
    hspec = pl.BlockSpec((1, A_HEADS, rows, LANES), lambda b, t: (b, 0, t, 0))
    return pl.pallas_call(
        _outproj_kernel,
        grid=(B, T // rows),
        in_specs=[hspec, hspec,
                  pl.BlockSpec((1, rows, 2 * D), lambda b, t: (b, t, 0)),
                  pl.BlockSpec((1, rows, D), lambda b, t: (b, t, 0)),
                  pl.BlockSpec((1, 1, D), lambda b, t: (b, 0, 0)),
                  _const_spec((D, D)), _const_spec((D, D)), _const_spec((D, D)),
                  _const_spec((1, D)), _const_spec((1, D))],
        out_specs=pl.BlockSpec((1, rows, D), lambda b, t: (b, t, 0)),
        out_shape=jax.ShapeDtypeStruct((B, T, D), F32),
        compiler_params=pltpu.CompilerParams(
            dimension_semantics=("arbitrary", "arbitrary"), vmem_limit_bytes=VMEM_LIMIT),
        name="outproj",
    )(ya, yr, mg, x, gate1p, w_pa, w_pb, w_out, ln_g, ln_b)


def _rope_perm():
    half = R_DK // 2
    perm = []
    for p in range(R_PAIRS):
        for part in range(2):
            for i in range(2):
                h = 2 * p + i
                perm.extend(h * R_DK + part * half + np.arange(half))
    return np.asarray(perm, np.int32)


def _rope_tables(T):
    half = R_DK // 2
    pos = jnp.arange(T, dtype=F32)
    inv = ROPE_BASE ** (-jnp.arange(0, R_DK, 2, dtype=F32) / R_DK)
    ang = pos[:, None] * inv[None, :]
    cos, sin = jnp.cos(ang), jnp.sin(ang)
    return jnp.tile(cos, (1, LANES // half)), jnp.concatenate([-sin, -sin, sin, sin], axis=1)


def _trunk(x, ada, params):
    B, T, D = x.shape
    cos_p, sin_p = _rope_tables(T)
    for l in range(DEPTH):
        p = params[l]
        shift = ada[l, :, :D].reshape(B, 1, D)
        sc1p = (1.0 + ada[l, :, D:2 * D]).reshape(B, 1, D)
        gate1p = (1.0 + ada[l, :, 2 * D:]).reshape(B, 1, D)
        qa, lff, lfb, va, ga, qr, kr, vr, gr, mg = _inproj_call(
            x, sc1p, shift, p["w_in"], p["w_mg"], p["b_mg"], p["lbm"], p["oml"], cos_p, sin_p)
        ya = _hgrn_call(qa, lff, lfb, va, ga, p["cvec"], p["nw"])
        yr = _ret_call(qr, kr, vr, gr, p["lg_rows"])
        x = _outproj_call(ya, yr, mg, x, gate1p, p["w_pa"], p["w_pb"], p["w_out"],
                          p["ln_g"], p["ln_b"])
    return x


def kernel(x_prompt, x_sample, c_prompt, c_sample, w_ada, b_ada, w_in, hgrn_lb, a_norm_w, ret_decay,
           w_pa, w_pb, w_mg, b_mg, w_out, ln_g, ln_b):
    D = D_MODEL
    pr = jax.nn.softmax(hgrn_lb.astype(F32), axis=0)
    lbs = jnp.cumsum(pr, axis=0) - pr[0:1]
    perm = _rope_perm()
    params = []
    for l in range(DEPTH):
        lb = lbs[l]
        lbm = jnp.maximum(lb, LOG_FLOOR)
        wl = w_in[l]
        wl = jnp.concatenate([wl[:, :OFF_RQ], wl[:, OFF_RQ + perm], wl[:, OFF_RK + perm],
                              wl[:, OFF_RV:]], axis=1)
        lg = jax.nn.log_sigmoid(ret_decay[l].astype(F32))
        lg_rows = jnp.broadcast_to(
            lg.reshape(2, R_PAIRS, 2).transpose(1, 0, 2).reshape(R_PAIRS, 4, 1), (R_PAIRS, 4, LANES))
        params.append(dict(
            w_in=wl.astype(BF16), w_mg=w_mg[l].astype(BF16), b_mg=b_mg[l].reshape(1, 2 * D),
            lbm=lbm.reshape(1, D), oml=(1.0 - lb).reshape(1, D),
            cvec=((1.0 - lb) + lbm).reshape(A_HEADS, 1, A_DK),
            nw=a_norm_w[l].reshape(1, A_DV), lg_rows=lg_rows,
            w_pa=w_pa[l].astype(BF16), w_pb=w_pb[l].astype(BF16), w_out=w_out[l].astype(BF16),
            ln_g=ln_g[l].reshape(1, D), ln_b=ln_b[l].reshape(1, D)))
    nbp = x_prompt.shape[0]
    ada = _ada_call(jnp.concatenate([c_prompt, c_sample], axis=0), w_ada, b_ada)
    y_prompt = _trunk(x_prompt, ada[:, :nbp], params)
    y_sample = _trunk(x_sample, ada[:, nbp:], params)
    return (y_prompt, y_sample)
```

```python
import functools

import numpy as np
import jax
import jax.numpy as jnp
from jax import lax
from jax.experimental import pallas as pl
from jax.experimental.pallas import tpu as pltpu

F32 = jnp.float32
BF16 = jnp.bfloat16

D_MODEL = 1024
DEPTH = 2
A_HEADS = 8
A_DK = 128
A_DV = 128
R_HEADS = 8
R_DK = 64
R_DV = 128
R_PAIRS = R_HEADS // 2
LN_EPS = 1e-5
LOG_FLOOR = 1e-30
NEG_BIG = -1e30
DEEPNORM_ALPHA = (2.0 * DEPTH) ** 0.25
ROPE_BASE = 10000.0

OFF_AQ, OFF_FF, OFF_FB, OFF_AI, OFF_AG, OFF_RQ, OFF_RK, OFF_RV, OFF_RG = (
    0, 1024, 2048, 3072, 4096, 5120, 5632, 6144, 7168)
IN_WIDTH = 8192

LANES = 128
HGRN_CHUNK = 64
HGRN_LEVELS = 6
RET_CHUNK = 128
HGRN_SAFE_DECAY = 60.0
HGRN_SAFE_MAG = 1e10
HGRN_FAST_UNROLL = 4
IN_ROWS = 256
OUT_ROWS = 512
VMEM_LIMIT = 56 * 1024 * 1024


def _dot(a, b):
    return jnp.dot(a, b, preferred_element_type=F32)


def _dot_nt(a, b):
    return lax.dot_general(a, b, (((1,), (1,)), ((), ())), preferred_element_type=F32)


def _dot_tn(a, b):
    return lax.dot_general(a, b, (((0,), (0,)), ((), ())), preferred_element_type=F32)


def _sigmoid(z):
    return 1.0 / (1.0 + jnp.exp(-z))


def _const_spec(shape):
    nd = len(shape)
    return pl.BlockSpec(shape, lambda *_: (0,) * nd, pipeline_mode=pl.Buffered(1))


def _ada_kernel(c_ref, w_ref, b_ref, o_ref):
    c = c_ref[...]
    a = c * _sigmoid(c)
    o_ref[0] = jnp.dot(a, w_ref[0], preferred_element_type=F32,
                       precision=lax.Precision.HIGHEST) + b_ref[0]


def _ada_call(c, w_ada, b_ada):
    nb = c.shape[0]
    d = D_MODEL
    return pl.pallas_call(
        _ada_kernel,
        grid=(DEPTH, 3),
        in_specs=[
            pl.BlockSpec((nb, d), lambda l, j: (0, 0)),
            pl.BlockSpec((1, d, d), lambda l, j: (l, 0, j)),
            pl.BlockSpec((1, 1, d), lambda l, j: (l, 0, j)),
        ],
        out_specs=pl.BlockSpec((1, nb, d), lambda l, j: (l, 0, j)),
        out_shape=jax.ShapeDtypeStruct((DEPTH, nb, 3 * d), F32),
        compiler_params=pltpu.CompilerParams(
            dimension_semantics=("arbitrary", "arbitrary"), vmem_limit_bytes=VMEM_LIMIT),
        name="adaln",
    )(c, w_ada, b_ada.reshape(DEPTH, 1, 3 * d))


def _inproj_kernel(x_ref, sc_ref, sh_ref, win_ref, wmg_ref, bmg_ref, lbm_ref, oml_ref,
                   cos_ref, sin_ref,
                   qa_ref, lff_ref, lfb_ref, va_ref, ga_ref, qr_ref, kr_ref, vr_ref, gr_ref,
                   mg_ref):
    u = (x_ref[0] * sc_ref[0] + sh_ref[0]).astype(BF16)

    def proj(c0):
        return _dot(u, win_ref[:, c0:c0 + 2 * LANES])

    def halves(z):
        return z[:, :LANES], z[:, LANES:]

    for g in range(A_HEADS // 2):
        c = 2 * LANES * g
        z = proj(OFF_AQ + c)
        z = z * _sigmoid(z) * (A_DK ** -0.5)
        for i, zz in enumerate(halves(z)):
            qa_ref[0, 2 * g + i] = zz.astype(BF16)
        for off, ref in ((OFF_FF, lff_ref), (OFF_FB, lfb_ref)):
            z = proj(off + c)
            f = lbm_ref[:, c:c + 2 * LANES] + oml_ref[:, c:c + 2 * LANES] * _sigmoid(z)
            lf = jnp.log(f)
            for i, zz in enumerate(halves(lf)):
                ref[0, 2 * g + i] = zz
        z = proj(OFF_AI + c)
        for i, zz in enumerate(halves(z)):
            va_ref[0, 2 * g + i] = zz.astype(BF16)
        z = proj(OFF_AG + c)
        z = z * _sigmoid(z)
        for i, zz in enumerate(halves(z)):
            ga_ref[0, 2 * g + i] = zz.astype(BF16)
        z = proj(OFF_RV + c)
        for i, zz in enumerate(halves(z)):
            vr_ref[0, 2 * g + i] = zz.astype(BF16)
        z = proj(OFF_RG + c)
        z = z * _sigmoid(z)
        for i, zz in enumerate(halves(z)):
            gr_ref[0, 2 * g + i] = zz.astype(BF16)

    cosv = cos_ref[...]
    sinv = sin_ref[...]
    for off, ref, scale in ((OFF_RQ, qr_ref, R_DK ** -0.5), (OFF_RK, kr_ref, 1.0)):
        for g in range(R_PAIRS // 2):
            z = proj(off + 2 * LANES * g)
            for i, zz in enumerate(halves(z)):
                rot = pltpu.roll(zz, LANES // 2, 1)
                ref[0, 2 * g + i] = ((zz * cosv + rot * sinv) * scale).astype(BF16)

    for g in range(2 * D_MODEL // (2 * LANES)):
        c = 2 * LANES * g
        z = _dot(u, wmg_ref[:, c:c + 2 * LANES]) + bmg_ref[:, c:c + 2 * LANES]
        mg_ref[0, :, c:c + 2 * LANES] = _sigmoid(z).astype(BF16)


def _inproj_call(x, sc1p, shift, w_in, w_mg, b_mg, lbm, oml, cos_p, sin_p):
    B, T, D = x.shape
    rows = min(IN_ROWS, T)
    grid = (B, T // rows)
    hd = lambda n, dt: jax.ShapeDtypeStruct((B, n, T, LANES), dt)
    hspec = lambda n: pl.BlockSpec((1, n, rows, LANES), lambda b, t: (b, 0, t, 0))
    bvec = pl.BlockSpec((1, 1, D), lambda b, t: (b, 0, 0))
    return pl.pallas_call(
        _inproj_kernel,
        grid=grid,
        in_specs=[
            pl.BlockSpec((1, rows, D), lambda b, t: (b, t, 0)),
            bvec, bvec,
            _const_spec((D, IN_WIDTH)),
            _const_spec((D, 2 * D)),
            _const_spec((1, 2 * D)),
            _const_spec((1, D)),
            _const_spec((1, D)),
            pl.BlockSpec((rows, LANES), lambda b, t: (t, 0)),
            pl.BlockSpec((rows, LANES), lambda b, t: (t, 0)),
        ],
        out_specs=[hspec(A_HEADS), hspec(A_HEADS), hspec(A_HEADS), hspec(A_HEADS), hspec(A_HEADS),
                   hspec(R_PAIRS), hspec(R_PAIRS), hspec(R_HEADS), hspec(R_HEADS),
                   pl.BlockSpec((1, rows, 2 * D), lambda b, t: (b, t, 0))],
        out_shape=[hd(A_HEADS, BF16), hd(A_HEADS, F32), hd(A_HEADS, F32), hd(A_HEADS, BF16),
                   hd(A_HEADS, BF16), hd(R_PAIRS, BF16), hd(R_PAIRS, BF16), hd(R_HEADS, BF16),
                   hd(R_HEADS, BF16), jax.ShapeDtypeStruct((B, T, 2 * D), BF16)],
        compiler_params=pltpu.CompilerParams(
            dimension_semantics=("arbitrary", "arbitrary"), vmem_limit_bytes=VMEM_LIMIT),
        name="inproj",
    )(x, sc1p, shift, w_in, w_mg, b_mg, lbm, oml, cos_p, sin_p)


def _hgrn_tables(reverse):
    C = HGRN_CHUNK
    idx = np.arange(C)
    cum = (idx[None, :] >= idx[:, None]) if reverse else (idx[None, :] <= idx[:, None])
    cum = cum.astype(np.float64)
    end = 0 if reverse else C - 1
    blocks = []
    for j in range(HGRN_LEVELS):
        m = C >> (j + 1)
        base = idx - idx % (2 * m)
        ref = base + (m if reverse else m - 1)
        blocks.append(cum - cum[ref])
    blocks.append(cum)
    blocks.append(cum[end][None, :] - cum)
    safe_tab = np.concatenate(blocks, axis=0)
    mid = C // 2 if reverse else C // 2 - 1
    fast_tab = np.concatenate([cum - cum[mid][None, :], np.tile(cum[mid][None, :], (8, 1))], axis=0)
    x = idx[:, None] ^ idx[None, :]
    bl = np.zeros_like(x)
    nz = x > 0
    bl[nz] = np.floor(np.log2(x[nz])).astype(np.int64) + 1
    lvl = HGRN_LEVELS - bl
    coupled = (idx[:, None] <= idx[None, :]) if reverse else (idx[:, None] >= idx[None, :])
    lvl = np.where(coupled, lvl, -1)
    return safe_tab.astype(np.float32), fast_tab.astype(np.float32), lvl.astype(np.int32), end


def _hgrn_kernel(q_ref, lff_ref, lfb_ref, v_ref, g_ref, c_ref, nw_ref,
                 tabf_ref, tabb_ref, ftabf_ref, ftabb_ref, lvf_ref, lvb_ref,
                 y_ref, o_scr, sf_scr, sb_scr, ops_scr, dec_scr, *, ends):
    C = HGRN_CHUNK
    T = q_ref.shape[2]
    n_chunks = T // C
    sf_scr[...] = jnp.zeros_like(sf_scr)
    sb_scr[...] = jnp.zeros_like(sb_scr)
    cvec = c_ref[0]
    nw = nw_ref[...]

    def load(ci, lf_ref):
        rows = pl.ds(pl.multiple_of(ci * C, C), C)
        lf = lf_ref[0, 0, rows, :]
        q = q_ref[0, 0, rows, :].astype(F32)
        v = v_ref[0, 0, rows, :]
        k = cvec - jnp.exp(lf)
        hi = lf.astype(BF16)
        lo = (lf - hi.astype(F32)).astype(BF16)
        return rows, q, k, v, jnp.concatenate([hi, lo], axis=1)

    def advance(s_scr, a, v, qi, kd, s_decay):
        st = s_scr[...]
        o = _dot(a.astype(BF16), v) + _dot_nt(qi, st.astype(BF16))
        s_scr[...] = st * s_decay + _dot_tn(v, kd)
        return o


    def fast_prepare(chains, slot):
        loaded = [load(ci, p[0]) for ci, p in chains]
        tabs = [_dot(p[1][...], ld[4]) for (_, p), ld in zip(chains, loaded)]
        for c, ((_, p), (_, q, k, _, _), r) in enumerate(zip(chains, loaded, tabs)):
            dm = r[:, :LANES] + r[:, LANES:]
            d = dm[:C]
            b_mid = dm[C:C + 1]
            d_end = d[p[4]:p[4] + 1]
            qh = q * jnp.exp(d)
            kh = k * jnp.exp(-d)
            ops_scr[slot, c, 0] = qh.astype(BF16)
            ops_scr[slot, c, 1] = kh.astype(BF16)
            ops_scr[slot, c, 2] = (qh * jnp.exp(b_mid)).astype(BF16)
            ops_scr[slot, c, 3] = (kh * jnp.exp(d_end)).astype(BF16)
            dec_scr[slot, c] = jnp.broadcast_to(jnp.exp(b_mid + d_end), (8, LANES))

    def fast_scores(n, slot):
        return [_dot_nt(ops_scr[slot, c, 0], ops_scr[slot, c, 1]) for c in range(n)]

    def fast_finish(chains, slot, scores):
        outs = []
        for c, ((ci, p), sc) in enumerate(zip(chains, scores)):
            rows = pl.ds(pl.multiple_of(ci * C, C), C)
            a = jnp.where(p[2][...] >= 0, sc, 0.0)
            outs.append((rows, advance(p[3], a, v_ref[0, 0, rows, :], ops_scr[slot, c, 2],
                                       ops_scr[slot, c, 3], dec_scr[slot, c, 0:1, :])))
        return outs

    def safe_chunk(ci, lf_ref, tab_ref, lv_ref, s_scr, end):
        rows, q, k, v, hl = load(ci, lf_ref)
        r = _dot(tab_ref[...], hl)
        dm = r[:, :LANES] + r[:, LANES:]
        lv = lv_ref[...]
        a = jnp.where(lv == HGRN_LEVELS, _dot_nt(q.astype(BF16), k.astype(BF16)), 0.0)
        for j in range(HGRN_LEVELS):
            e = jnp.exp(-jnp.abs(dm[j * C:(j + 1) * C]))
            aj = _dot_nt((q * e).astype(BF16), (k * e).astype(BF16))
            a = jnp.where(lv == j, aj, a)
        bq = dm[HGRN_LEVELS * C:(HGRN_LEVELS + 1) * C]
        bk = dm[(HGRN_LEVELS + 1) * C:(HGRN_LEVELS + 2) * C]
        qi = (q * jnp.exp(bq)).astype(BF16)
        kd = (k * jnp.exp(bk)).astype(BF16)
        return rows, advance(s_scr, a, v, qi, kd, jnp.exp(bq[end:end + 1, :]))

    def finalize(rows, o):
        tot = o_scr[rows, :] + o
        ms = jnp.mean(tot * tot, axis=-1, keepdims=True)
        y = tot * lax.rsqrt(ms + LN_EPS) * nw * g_ref[0, 0, rows, :].astype(F32)
        y_ref[0, 0, rows, :] = y.astype(BF16)

    def first_visit(rows, o):
        o_scr[rows, :] = o

    def chain_groups(tf_ref, tb_ref, per_step):
        fwd = (lff_ref, tf_ref, lvf_ref, sf_scr, ends[0])
        bwd = (lfb_ref, tb_ref, lvb_ref, sb_scr, ends[1])

        def chains(g):
            out = []
            for u in range(per_step):
                i = g * per_step + u
                out += [(i, fwd), (n_chunks - 1 - i, bwd)]
            return out
        return chains

    def loop(lo, hi, body, finish):
        def wrapped(g, carry):
            body(g, finish)
            return carry
        lax.fori_loop(lo, hi, wrapped, 0)

    def run_safe():
        chains = chain_groups(tabf_ref, tabb_ref, 1)

        def step(g, finish):
            for ci, p in chains(g):
                finish(*safe_chunk(ci, *p))

        loop(0, n_chunks // 2, step, first_visit)
        loop(n_chunks // 2, n_chunks, step, finalize)

    def run_fast():
        per_step = ops_scr.shape[1] // 2
        chains = chain_groups(ftabf_ref, ftabb_ref, per_step)
        n_groups = n_chunks // per_step
        n_ch = 2 * per_step

        def pair(gg, finish):
            for slot in range(2):
                g = 2 * gg + slot
                scores = fast_scores(n_ch, slot)
                fast_prepare(chains(jnp.minimum(g + 1, n_groups - 1)), 1 - slot)
                for rows, o in fast_finish(chains(g), slot, scores):
                    finish(rows, o)

        fast_prepare(chains(0), 0)
        loop(0, n_groups // 4, pair, first_visit)
        loop(n_groups // 4, n_groups // 2, pair, finalize)

    group = 8 if n_chunks % 8 == 0 else 1
    grows = group * C

    def scan(i, carry):
        dec, mag = carry
        rows = pl.ds(pl.multiple_of(i * grows, grows), grows)
        for lf_ref in (lff_ref, lfb_ref):
            s = jnp.sum(lf_ref[0, 0, rows, :].reshape(group, C, LANES), axis=1)
            dec = jnp.minimum(dec, jnp.min(s, axis=0, keepdims=True))
        for ref in (q_ref, v_ref):
            x = jnp.abs(ref[0, 0, rows, :].astype(F32))
            mag = jnp.maximum(mag, jnp.max(x, axis=0, keepdims=True))
        return dec, mag

    dec, mag = lax.fori_loop(0, n_chunks // group, scan,
                             (jnp.zeros((1, LANES), F32), jnp.zeros((1, LANES), F32)))
    is_safe = jnp.logical_and(jnp.min(dec) > -HGRN_SAFE_DECAY, jnp.max(mag) < HGRN_SAFE_MAG)

    pl.when(is_safe)(run_fast)
    pl.when(jnp.logical_not(is_safe))(run_safe)


def _hgrn_call(qa, lff, lfb, va, ga, cvec, nw):
    B, H, T, _ = qa.shape
    tabf, ftabf, lvf, endf = _hgrn_tables(False)
    tabb, ftabb, lvb, endb = _hgrn_tables(True)
    seq = pl.BlockSpec((1, 1, T, LANES), lambda b, h: (b, h, 0, 0))
    n_chunks = T // HGRN_CHUNK
    assert n_chunks % 4 == 0, T
    per_step = max(u for u in (HGRN_FAST_UNROLL, 2, 1) if n_chunks % (4 * u) == 0)
    return pl.pallas_call(
        functools.partial(_hgrn_kernel, ends=(endf, endb)),
        grid=(B, H),
        in_specs=[seq, seq, seq, seq, seq,
                  pl.BlockSpec((1, 1, LANES), lambda b, h: (h, 0, 0)),
                  _const_spec((1, LANES)),
                  _const_spec(tabf.shape), _const_spec(tabb.shape),
                  _const_spec(ftabf.shape), _const_spec(ftabb.shape),
                  _const_spec(lvf.shape), _const_spec(lvb.shape)],
        out_specs=seq,
        out_shape=jax.ShapeDtypeStruct((B, H, T, LANES), BF16),
        scratch_shapes=[pltpu.VMEM((T, LANES), F32),
                        pltpu.VMEM((A_DV, A_DK), F32),
                        pltpu.VMEM((A_DV, A_DK), F32),
                        pltpu.VMEM((2, 2 * per_step, 4, HGRN_CHUNK, LANES), BF16),
                        pltpu.VMEM((2, 2 * per_step, 8, LANES), F32)],
        compiler_params=pltpu.CompilerParams(
            dimension_semantics=("arbitrary", "arbitrary"), vmem_limit_bytes=VMEM_LIMIT),
        name="hgrn",
    )(qa, lff, lfb, va, ga, cvec, nw,
      jnp.asarray(tabf, BF16), jnp.asarray(tabb, BF16),
      jnp.asarray(ftabf, BF16), jnp.asarray(ftabb, BF16), jnp.asarray(lvf), jnp.asarray(lvb))


def _ret_kernel(q_ref, k_ref, v_ref, g_ref, lg_ref, y_ref, o_scr, s_scr):
    C = RET_CHUNK
    T = q_ref.shape[2]
    n_chunks = T // C
    s_scr[...] = jnp.zeros_like(s_scr)
    lane = lax.broadcasted_iota(jnp.int32, (1, LANES), 1)
    head_mask = [(lane % (LANES // 2)) < (LANES // 4), (lane % (LANES // 2)) >= (LANES // 4)]
    tpos = lax.broadcasted_iota(jnp.int32, (C, C), 0)
    spos = lax.broadcasted_iota(jnp.int32, (C, C), 1)
    pos = lax.broadcasted_iota(jnp.int32, (C, 1), 0).astype(F32)

    tables = []
    for d in range(2):
        dist = ((tpos - spos) if d == 0 else (spos - tpos)).astype(F32)
        after = pos if d == 0 else (C - 1.0 - pos)
        per_head = []
        for i in range(2):
            lg = lg_ref[0, 2 * d + i:2 * d + i + 1, :]
            decay = jnp.exp(jnp.where(dist >= 0, dist * lg, NEG_BIG))
            q_dec = jnp.exp((after + 1.0) * lg)
            k_dec = jnp.exp((C - 1.0 - after) * lg)
            s_dec = jnp.exp(C * lg)
            per_head.append((decay, q_dec, k_dec, s_dec))
        tables.append(per_head)

    def chunk(ci, d):
        r0 = pl.multiple_of(ci * C, C)
        rows = pl.ds(r0, C)
        q = q_ref[0, 0, rows, :]
        k = k_ref[0, 0, rows, :]
        qf = q.astype(F32)
        kf = k.astype(F32)
        outs = []
        for i in range(2):
            decay, q_dec, k_dec, s_dec = tables[d][i]
            v = v_ref[0, i, rows, :]
            qm = jnp.where(head_mask[i], qf, 0.0)
            scores = _dot_nt(qm.astype(BF16), k) * decay
            st = s_scr[2 * d + i]
            o = _dot(scores.astype(BF16), v) + _dot_nt((qm * q_dec).astype(BF16), st.astype(BF16))
            km = jnp.where(head_mask[i], kf * k_dec, 0.0).astype(BF16)
            s_scr[2 * d + i] = st * s_dec + _dot_tn(v, km)
            outs.append(o)
        return rows, outs

    def finalize(rows, outs):
        for i in range(2):
            tot = o_scr[i, rows, :] + outs[i]
            mu = jnp.mean(tot, axis=-1, keepdims=True)
            cen = tot - mu
            var = jnp.mean(cen * cen, axis=-1, keepdims=True)
            y = cen * lax.rsqrt(var + LN_EPS) * g_ref[0, i, rows, :].astype(F32)
            y_ref[0, i, rows, :] = y.astype(BF16)

    def first_half(j, carry):
        for d, ci in ((0, j), (1, n_chunks - 1 - j)):
            rows, outs = chunk(ci, d)
            for i in range(2):
                o_scr[i, rows, :] = outs[i]
        return carry

    def second_half(j, carry):
        for d, ci in ((0, j), (1, n_chunks - 1 - j)):
            rows, outs = chunk(ci, d)
            finalize(rows, outs)
        return carry

    lax.fori_loop(0, n_chunks // 2, first_half, 0)
    lax.fori_loop(n_chunks // 2, n_chunks, second_half, 0)


def _ret_call(qr, kr, vr, gr, lg_rows):
    B, P, T, _ = qr.shape
    seq1 = pl.BlockSpec((1, 1, T, LANES), lambda b, p: (b, p, 0, 0))
    seq2 = pl.BlockSpec((1, 2, T, LANES), lambda b, p: (b, p, 0, 0))
    return pl.pallas_call(
        _ret_kernel,
        grid=(B, P),
        in_specs=[seq1, seq1, seq2, seq2,
                  pl.BlockSpec((1, 4, LANES), lambda b, p: (p, 0, 0))],
        out_specs=seq2,
        out_shape=jax.ShapeDtypeStruct((B, R_HEADS, T, LANES), BF16),
        scratch_shapes=[pltpu.VMEM((2, T, LANES), F32),
                        pltpu.VMEM((4, R_DV, LANES), F32)],
        compiler_params=pltpu.CompilerParams(
            dimension_semantics=("arbitrary", "arbitrary"), vmem_limit_bytes=VMEM_LIMIT),
        name="retention",
    )(qr, kr, vr, gr, lg_rows)


def _outproj_kernel(ya_ref, yr_ref, mg_ref, x_ref, gate_ref, wpa_ref, wpb_ref, wout_ref,
                    lng_ref, lnb_ref, o_ref):
    ya = jnp.concatenate([ya_ref[0, h] for h in range(A_HEADS)], axis=1)
    yr = jnp.concatenate([yr_ref[0, h] for h in range(R_HEADS)], axis=1)
    pa = _dot(ya, wpa_ref[...])
    pr = _dot(yr, wpb_ref[...])
    mg = mg_ref[0]
    m = mg[:, :D_MODEL].astype(F32) * pa + mg[:, D_MODEL:].astype(F32) * pr
    s = _dot(m.astype(BF16), wout_ref[...])
    h = DEEPNORM_ALPHA * x_ref[0] + gate_ref[0] * s
    mu = jnp.mean(h, axis=-1, keepdims=True)
    cen = h - mu
    var = jnp.mean(cen * cen, axis=-1, keepdims=True)
    o_ref[0] = cen * lax.rsqrt(var + LN_EPS) * lng_ref[...] + lnb_ref[...]


def _outproj_call(ya, yr, mg, x, gate1p, w_pa, w_pb, w_out, ln_g, ln_b):
    B, T, D = x.shape
    rows = min(OUT_ROWS, T)
    hspec = pl.BlockSpec((1, A_HEADS, rows, LANES), lambda b, t: (b, 0, t, 0))
    return pl.pallas_call(
        _outproj_kernel,
        grid=(B, T // rows),
        in_specs=[hspec, hspec,
                  pl.BlockSpec((1, rows, 2 * D), lambda b, t: (b, t, 0)),
                  pl.BlockSpec((1, rows, D), lambda b, t: (b, t, 0)),
                  pl.BlockSpec((1, 1, D), lambda b, t: (b, 0, 0)),
                  _const_spec((D, D)), _const_spec((D, D)), _const_spec((D, D)),
                  _const_spec((1, D)), _const_spec((1, D))],
        out_specs=pl.BlockSpec((1, rows, D), lambda b, t: (b, t, 0)),
        out_shape=jax.ShapeDtypeStruct((B, T, D), F32),
        compiler_params=pltpu.CompilerParams(
            dimension_semantics=("arbitrary", "arbitrary"), vmem_limit_bytes=VMEM_LIMIT),
        name="outproj",
    )(ya, yr, mg, x, gate1p, w_pa, w_pb, w_out, ln_g, ln_b)


def _rope_perm():
    half = R_DK // 2
    perm = []
    for p in range(R_PAIRS):
        for part in range(2):
            for i in range(2):
                h = 2 * p + i
                perm.extend(h * R_DK + part * half + np.arange(half))
    return np.asarray(perm, np.int32)


def _rope_tables(T):
    half = R_DK // 2
    pos = jnp.arange(T, dtype=F32)
    inv = ROPE_BASE ** (-jnp.arange(0, R_DK, 2, dtype=F32) / R_DK)
    ang = pos[:, None] * inv[None, :]
    cos, sin = jnp.cos(ang), jnp.sin(ang)
    return jnp.tile(cos, (1, LANES // half)), jnp.concatenate([-sin, -sin, sin, sin], axis=1)


def _trunk(x, ada, params):
    B, T, D = x.shape
    cos_p, sin_p = _rope_tables(T)
    for l in range(DEPTH):
        p = params[l]
        shift = ada[l, :, :D].reshape(B, 1, D)
        sc1p = (1.0 + ada[l, :, D:2 * D]).reshape(B, 1, D)
        gate1p = (1.0 + ada[l, :, 2 * D:]).reshape(B, 1, D)
        qa, lff, lfb, va, ga, qr, kr, vr, gr, mg = _inproj_call(
            x, sc1p, shift, p["w_in"], p["w_mg"], p["b_mg"], p["lbm"], p["oml"], cos_p, sin_p)
        ya = _hgrn_call(qa, lff, lfb, va, ga, p["cvec"], p["nw"])
        yr = _ret_call(qr, kr, vr, gr, p["lg_rows"])
        x = _outproj_call(ya, yr, mg, x, gate1p, p["w_pa"], p["w_pb"], p["w_out"],
                          p["ln_g"], p["ln_b"])
    return x


def kernel(x_prompt, x_sample, c_prompt, c_sample, w_ada, b_ada, w_in, hgrn_lb, a_norm_w, ret_decay,
           w_pa, w_pb, w_mg, b_mg, w_out, ln_g, ln_b):
    D = D_MODEL
    pr = jax.nn.softmax(hgrn_lb.astype(F32), axis=0)
    lbs = jnp.cumsum(pr, axis=0) - pr[0:1]
    perm = _rope_perm()
    params = []
    for l in range(DEPTH):
        lb = lbs[l]
        lbm = jnp.maximum(lb, LOG_FLOOR)
        wl = w_in[l]
        wl = jnp.concatenate([wl[:, :OFF_RQ], wl[:, OFF_RQ + perm], wl[:, OFF_RK + perm],
                              wl[:, OFF_RV:]], axis=1)
        lg = jax.nn.log_sigmoid(ret_decay[l].astype(F32))
        lg_rows = jnp.broadcast_to(
            lg.reshape(2, R_PAIRS, 2).transpose(1, 0, 2).reshape(R_PAIRS, 4, 1), (R_PAIRS, 4, LANES))
        params.append(dict(
            w_in=wl.astype(BF16), w_mg=w_mg[l].astype(BF16), b_mg=b_mg[l].reshape(1, 2 * D),
            lbm=lbm.reshape(1, D), oml=(1.0 - lb).reshape(1, D),
            cvec=((1.0 - lb) + lbm).reshape(A_HEADS, 1, A_DK),
            nw=a_norm_w[l].reshape(1, A_DV), lg_rows=lg_rows,
            w_pa=w_pa[l].astype(BF16), w_pb=w_pb[l].astype(BF16), w_out=w_out[l].astype(BF16),
            ln_g=ln_g[l].reshape(1, D), ln_b=ln_b[l].reshape(1, D)))
    nbp = x_prompt.shape[0]
    ada = _ada_call(jnp.concatenate([c_prompt, c_sample], axis=0), w_ada, b_ada)
    y_prompt = _trunk(x_prompt, ada[:, :nbp], params)
    y_sample = _trunk(x_sample, ada[:, nbp:], params)
    return (y_prompt, y_sample)
```

```python
import functools

import numpy as np
import jax
import jax.numpy as jnp
from jax import lax
from jax.experimental import pallas as pl
from jax.experimental.pallas import tpu as pltpu

F32 = jnp.float32
BF16 = jnp.bfloat16

D_MODEL = 1024
DEPTH = 2
A_HEADS = 8
A_DK = 128
A_DV = 128
R_HEADS = 8
R_DK = 64
R_DV = 128
R_PAIRS = R_HEADS // 2
LN_EPS = 1e-5
LOG_FLOOR = 1e-30
NEG_BIG = -1e30
DEEPNORM_ALPHA = (2.0 * DEPTH) ** 0.25
ROPE_BASE = 10000.0

OFF_AQ, OFF_FF, OFF_FB, OFF_AI, OFF_AG, OFF_RQ, OFF_RK, OFF_RV, OFF_RG = (
    0, 1024, 2048, 3072, 4096, 5120, 5632, 6144, 7168)
IN_WIDTH = 8192

LANES = 128
HGRN_CHUNK = 64
HGRN_LEVELS = 6
RET_CHUNK = 128
HGRN_SAFE_DECAY = 75.0
HGRN_SAFE_MAG = 1e5
HGRN_FAST_UNROLL = 4
IN_ROWS = 256
OUT_ROWS = 512
VMEM_LIMIT = 56 * 1024 * 1024


def _dot(a, b):
    return jnp.dot(a, b, preferred_element_type=F32)


def _dot_nt(a, b):
    return lax.dot_general(a, b, (((1,), (1,)), ((), ())), preferred_element_type=F32)


def _dot_tn(a, b):
    return lax.dot_general(a, b, (((0,), (0,)), ((), ())), preferred_element_type=F32)


def _sigmoid(z):
    return 1.0 / (1.0 + jnp.exp(-z))


def _const_spec(shape):
    nd = len(shape)
    return pl.BlockSpec(shape, lambda *_: (0,) * nd, pipeline_mode=pl.Buffered(1))


def _ada_kernel(c_ref, w_ref, b_ref, o_ref):
    c = c_ref[...]
    a = c * _sigmoid(c)
    o_ref[0] = jnp.dot(a, w_ref[0], preferred_element_type=F32,
                       precision=lax.Precision.HIGHEST) + b_ref[0]


def _ada_call(c, w_ada, b_ada):
    nb = c.shape[0]
    d = D_MODEL
    return pl.pallas_call(
        _ada_kernel,
        grid=(DEPTH, 3),
        in_specs=[
            pl.BlockSpec((nb, d), lambda l, j: (0, 0)),
            pl.BlockSpec((1, d, d), lambda l, j: (l, 0, j)),
            pl.BlockSpec((1, 1, d), lambda l, j: (l, 0, j)),
        ],
        out_specs=pl.BlockSpec((1, nb, d), lambda l, j: (l, 0, j)),
        out_shape=jax.ShapeDtypeStruct((DEPTH, nb, 3 * d), F32),
        compiler_params=pltpu.CompilerParams(
            dimension_semantics=("arbitrary", "arbitrary"), vmem_limit_bytes=VMEM_LIMIT),
        name="adaln",
    )(c, w_ada, b_ada.reshape(DEPTH, 1, 3 * d))


def _inproj_kernel(x_ref, sc_ref, sh_ref, win_ref, wmg_ref, bmg_ref, lbm_ref, oml_ref,
                   cos_ref, sin_ref,
                   qa_ref, lff_ref, lfb_ref, va_ref, ga_ref, qr_ref, kr_ref, vr_ref, gr_ref,
                   mg_ref):
    u = (x_ref[0] * sc_ref[0] + sh_ref[0]).astype(BF16)

    def proj(c0):
        return _dot(u, win_ref[:, c0:c0 + 2 * LANES])

    def halves(z):
        return z[:, :LANES], z[:, LANES:]

    for g in range(A_HEADS // 2):
        c = 2 * LANES * g
        z = proj(OFF_AQ + c)
        z = z * _sigmoid(z) * (A_DK ** -0.5)
        for i, zz in enumerate(halves(z)):
            qa_ref[0, 2 * g + i] = zz.astype(BF16)
        for off, ref in ((OFF_FF, lff_ref), (OFF_FB, lfb_ref)):
            z = proj(off + c)
            f = lbm_ref[:, c:c + 2 * LANES] + oml_ref[:, c:c + 2 * LANES] * _sigmoid(z)
            lf = jnp.log(f)
            for i, zz in enumerate(halves(lf)):
                ref[0, 2 * g + i] = zz
        z = proj(OFF_AI + c)
        for i, zz in enumerate(halves(z)):
            va_ref[0, 2 * g + i] = zz.astype(BF16)
        z = proj(OFF_AG + c)
        z = z * _sigmoid(z)
        for i, zz in enumerate(halves(z)):
            ga_ref[0, 2 * g + i] = zz.astype(BF16)
        z = proj(OFF_RV + c)
        for i, zz in enumerate(halves(z)):
            vr_ref[0, 2 * g + i] = zz.astype(BF16)
        z = proj(OFF_RG + c)
        z = z * _sigmoid(z)
        for i, zz in enumerate(halves(z)):
            gr_ref[0, 2 * g + i] = zz.astype(BF16)

    cosv = cos_ref[...]
    sinv = sin_ref[...]
    for off, ref, scale in ((OFF_RQ, qr_ref, R_DK ** -0.5), (OFF_RK, kr_ref, 1.0)):
        for g in range(R_PAIRS // 2):
            z = proj(off + 2 * LANES * g)
            for i, zz in enumerate(halves(z)):
                rot = pltpu.roll(zz, LANES // 2, 1)
                ref[0, 2 * g + i] = ((zz * cosv + rot * sinv) * scale).astype(BF16)

    for g in range(2 * D_MODEL // (2 * LANES)):
        c = 2 * LANES * g
        z = _dot(u, wmg_ref[:, c:c + 2 * LANES]) + bmg_ref[:, c:c + 2 * LANES]
        mg_ref[0, :, c:c + 2 * LANES] = _sigmoid(z).astype(BF16)


def _inproj_call(x, sc1p, shift, w_in, w_mg, b_mg, lbm, oml, cos_p, sin_p):
    B, T, D = x.shape
    rows = min(IN_ROWS, T)
    grid = (B, T // rows)
    hd = lambda n, dt: jax.ShapeDtypeStruct((B, n, T, LANES), dt)
    hspec = lambda n: pl.BlockSpec((1, n, rows, LANES), lambda b, t: (b, 0, t, 0))
    bvec = pl.BlockSpec((1, 1, D), lambda b, t: (b, 0, 0))
    return pl.pallas_call(
        _inproj_kernel,
        grid=grid,
        in_specs=[
            pl.BlockSpec((1, rows, D), lambda b, t: (b, t, 0)),
            bvec, bvec,
            _const_spec((D, IN_WIDTH)),
            _const_spec((D, 2 * D)),
            _const_spec((1, 2 * D)),
            _const_spec((1, D)),
            _const_spec((1, D)),
            pl.BlockSpec((rows, LANES), lambda b, t: (t, 0)),
            pl.BlockSpec((rows, LANES), lambda b, t: (t, 0)),
        ],
        out_specs=[hspec(A_HEADS), hspec(A_HEADS), hspec(A_HEADS), hspec(A_HEADS), hspec(A_HEADS),
                   hspec(R_PAIRS), hspec(R_PAIRS), hspec(R_HEADS), hspec(R_HEADS),
                   pl.BlockSpec((1, rows, 2 * D), lambda b, t: (b, t, 0))],
        out_shape=[hd(A_HEADS, BF16), hd(A_HEADS, F32), hd(A_HEADS, F32), hd(A_HEADS, BF16),
                   hd(A_HEADS, BF16), hd(R_PAIRS, BF16), hd(R_PAIRS, BF16), hd(R_HEADS, BF16),
                   hd(R_HEADS, BF16), jax.ShapeDtypeStruct((B, T, 2 * D), BF16)],
        compiler_params=pltpu.CompilerParams(
            dimension_semantics=("arbitrary", "arbitrary"), vmem_limit_bytes=VMEM_LIMIT),
        name="inproj",
    )(x, sc1p, shift, w_in, w_mg, b_mg, lbm, oml, cos_p, sin_p)


def _hgrn_tables(reverse):
    C = HGRN_CHUNK
    idx = np.arange(C)
    cum = (idx[None, :] >= idx[:, None]) if reverse else (idx[None, :] <= idx[:, None])
    cum = cum.astype(np.float64)
    end = 0 if reverse else C - 1
    blocks = []
    for j in range(HGRN_LEVELS):
        m = C >> (j + 1)
        base = idx - idx % (2 * m)
        ref = base + (m if reverse else m - 1)
        blocks.append(cum - cum[ref])
    blocks.append(cum)
    blocks.append(cum[end][None, :] - cum)
    safe_tab = np.concatenate(blocks, axis=0)
    mid = C // 2 if reverse else C // 2 - 1
    fast_tab = np.concatenate([cum - cum[mid][None, :], np.tile(cum[mid][None, :], (8, 1))], axis=0)
    x = idx[:, None] ^ idx[None, :]
    bl = np.zeros_like(x)
    nz = x > 0
    bl[nz] = np.floor(np.log2(x[nz])).astype(np.int64) + 1
    lvl = HGRN_LEVELS - bl
    coupled = (idx[:, None] <= idx[None, :]) if reverse else (idx[:, None] >= idx[None, :])
    lvl = np.where(coupled, lvl, -1)
    return safe_tab.astype(np.float32), fast_tab.astype(np.float32), lvl.astype(np.int32), end


def _hgrn_kernel(q_ref, lff_ref, lfb_ref, v_ref, g_ref, c_ref, nw_ref,
                 tabf_ref, tabb_ref, ftabf_ref, ftabb_ref, lvf_ref, lvb_ref,
                 y_ref, o_scr, sf_scr, sb_scr, ops_scr, dec_scr, a_scr, u_scr, *, ends):
    C = HGRN_CHUNK
    T = q_ref.shape[2]
    n_chunks = T // C
    sf_scr[...] = jnp.zeros_like(sf_scr)
    sb_scr[...] = jnp.zeros_like(sb_scr)
    cvec = c_ref[0]
    nw = nw_ref[...]

    def load(ci, lf_ref):
        rows = pl.ds(pl.multiple_of(ci * C, C), C)
        lf = lf_ref[0, 0, rows, :]
        q = q_ref[0, 0, rows, :].astype(F32)
        v = v_ref[0, 0, rows, :]
        k = cvec - jnp.exp(lf)
        hi = lf.astype(BF16)
        lo = (lf - hi.astype(F32)).astype(BF16)
        return rows, q, k, v, jnp.concatenate([hi, lo], axis=1)

    def advance(s_scr, a, v, qi, kd, s_decay):
        st = s_scr[...]
        o = _dot(a.astype(BF16), v) + _dot_nt(qi, st.astype(BF16))
        s_scr[...] = st * s_decay + _dot_tn(v, kd)
        return o


    def chunk_rows(ci):
        return pl.ds(pl.multiple_of(ci * C, C), C)

    def prepare_issue(chains):
        loaded = [load(ci, p[0]) for ci, p in chains]
        return loaded, [_dot(p[1][...], ld[4]) for (_, p), ld in zip(chains, loaded)]

    def prepare_store(chains, issued, slot):
        for c, ((_, p), (_, q, k, _, _), r) in enumerate(zip(chains, *issued)):
            dm = r[:, :LANES] + r[:, LANES:]
            d = dm[:C]
            b_mid = dm[C:C + 1]
            d_end = d[p[4]:p[4] + 1]
            qh = q * jnp.exp(d)
            kh = k * jnp.exp(-d)
            ops_scr[slot, c, 0] = qh.astype(BF16)
            ops_scr[slot, c, 1] = kh.astype(BF16)
            ops_scr[slot, c, 2] = (qh * jnp.exp(b_mid)).astype(BF16)
            ops_scr[slot, c, 3] = (kh * jnp.exp(d_end)).astype(BF16)
            dec_scr[slot, c] = jnp.broadcast_to(jnp.exp(b_mid + d_end), (8, LANES))

    def scores_issue(chains, slot):
        scores = [_dot_nt(ops_scr[slot, c, 0], ops_scr[slot, c, 1]) for c in range(len(chains))]
        updates = [_dot_tn(v_ref[0, 0, chunk_rows(ci), :], ops_scr[slot, c, 3])
                   for c, (ci, _) in enumerate(chains)]
        return scores, updates

    def scores_store(chains, issued, slot):
        for c, ((_, p), sc, u) in enumerate(zip(chains, *issued)):
            a_scr[slot, c] = jnp.where(p[2][...] >= 0, sc, 0.0).astype(BF16)
            u_scr[slot, c] = u

    def finish_issue(chains, ops_slot, slot):
        outs = []
        for c, (ci, p) in enumerate(chains):
            rows = chunk_rows(ci)
            st = p[3][...]
            o = (_dot(a_scr[slot, c], v_ref[0, 0, rows, :])
                 + _dot_nt(ops_scr[ops_slot, c, 2], st.astype(BF16)))
            p[3][...] = st * dec_scr[ops_slot, c, 0:1, :] + u_scr[slot, c]
            outs.append((rows, o))
        return outs

    def safe_chunk(ci, lf_ref, tab_ref, lv_ref, s_scr, end):
        rows, q, k, v, hl = load(ci, lf_ref)
        r = _dot(tab_ref[...], hl)
        dm = r[:, :LANES] + r[:, LANES:]
        lv = lv_ref[...]
        a = jnp.where(lv == HGRN_LEVELS, _dot_nt(q.astype(BF16), k.astype(BF16)), 0.0)
        for j in range(HGRN_LEVELS):
            e = jnp.exp(-jnp.abs(dm[j * C:(j + 1) * C]))
            aj = _dot_nt((q * e).astype(BF16), (k * e).astype(BF16))
            a = jnp.where(lv == j, aj, a)
        bq = dm[HGRN_LEVELS * C:(HGRN_LEVELS + 1) * C]
        bk = dm[(HGRN_LEVELS + 1) * C:(HGRN_LEVELS + 2) * C]
        qi = (q * jnp.exp(bq)).astype(BF16)
        kd = (k * jnp.exp(bk)).astype(BF16)
        return rows, advance(s_scr, a, v, qi, kd, jnp.exp(bq[end:end + 1, :]))

    def finalize(rows, o):
        tot = o_scr[rows, :] + o
        ms = jnp.mean(tot * tot, axis=-1, keepdims=True)
        y = tot * lax.rsqrt(ms + LN_EPS) * nw * g_ref[0, 0, rows, :].astype(F32)
        y_ref[0, 0, rows, :] = y.astype(BF16)

    def first_visit(rows, o):
        o_scr[rows, :] = o

    def chain_groups(tf_ref, tb_ref, per_step):
        fwd = (lff_ref, tf_ref, lvf_ref, sf_scr, ends[0])
        bwd = (lfb_ref, tb_ref, lvb_ref, sb_scr, ends[1])

        def chains(g):
            out = []
            for u in range(per_step):
                i = g * per_step + u
                out += [(i, fwd), (n_chunks - 1 - i, bwd)]
            return out
        return chains

    def loop(lo, hi, body, finish):
        def wrapped(g, carry):
            body(g, finish)
            return carry
        lax.fori_loop(lo, hi, wrapped, 0)

    def run_safe():
        chains = chain_groups(tabf_ref, tabb_ref, 1)

        def step(g, finish):
            for ci, p in chains(g):
                finish(*safe_chunk(ci, *p))

        loop(0, n_chunks // 2, step, first_visit)
        loop(n_chunks // 2, n_chunks, step, finalize)

    def run_fast():
        per_step = ops_scr.shape[1] // 2
        chains = chain_groups(ftabf_ref, ftabb_ref, per_step)
        n_groups = n_chunks // per_step
        last = n_groups - 1

        def step(g, finish):
            g1 = jnp.minimum(g + 1, last)
            g2 = jnp.minimum(g + 2, last)
            sc = scores_issue(chains(g1), g1 % 3)
            pr = prepare_issue(chains(g2))
            outs = finish_issue(chains(g), g % 3, g % 2)
            scores_store(chains(g1), sc, (g + 1) % 2)
            prepare_store(chains(g2), pr, (g + 2) % 3)
            for rows, o in outs:
                finish(rows, o)

        for g in range(2):
            prepare_store(chains(g), prepare_issue(chains(g)), g)
        scores_store(chains(0), scores_issue(chains(0), 0), 0)
        loop(0, n_groups // 2, step, first_visit)
        loop(n_groups // 2, n_groups, step, finalize)

    group = 8 if n_chunks % 8 == 0 else 1
    grows = group * C

    def scan(i, carry):
        dec, mag = carry
        rows = pl.ds(pl.multiple_of(i * grows, grows), grows)
        for lf_ref in (lff_ref, lfb_ref):
            s = jnp.sum(lf_ref[0, 0, rows, :].reshape(2 * group, C // 2, LANES), axis=1)
            dec = jnp.minimum(dec, jnp.min(s, axis=0, keepdims=True))
        for ref in (q_ref, v_ref):
            x = jnp.abs(ref[0, 0, rows, :].astype(F32))
            mag = jnp.maximum(mag, jnp.max(x, axis=0, keepdims=True))
        return dec, mag

    dec, mag = lax.fori_loop(0, n_chunks // group, scan,
                             (jnp.zeros((1, LANES), F32), jnp.zeros((1, LANES), F32)))
    is_safe = jnp.logical_and(jnp.min(dec) > -HGRN_SAFE_DECAY, jnp.max(mag) < HGRN_SAFE_MAG)

    pl.when(is_safe)(run_fast)
    pl.when(jnp.logical_not(is_safe))(run_safe)


def _hgrn_call(qa, lff, lfb, va, ga, cvec, nw):
    B, H, T, _ = qa.shape
    tabf, ftabf, lvf, endf = _hgrn_tables(False)
    tabb, ftabb, lvb, endb = _hgrn_tables(True)
    seq = pl.BlockSpec((1, 1, T, LANES), lambda b, h: (b, h, 0, 0))
    n_chunks = T // HGRN_CHUNK
    assert n_chunks % 4 == 0, T
    per_step = max(u for u in (HGRN_FAST_UNROLL, 2, 1) if n_chunks % (4 * u) == 0)
    return pl.pallas_call(
        functools.partial(_hgrn_kernel, ends=(endf, endb)),
        grid=(B, H),
        in_specs=[seq, seq, seq, seq, seq,
                  pl.BlockSpec((1, 1, LANES), lambda b, h: (h, 0, 0)),
                  _const_spec((1, LANES)),
                  _const_spec(tabf.shape), _const_spec(tabb.shape),
                  _const_spec(ftabf.shape), _const_spec(ftabb.shape),
                  _const_spec(lvf.shape), _const_spec(lvb.shape)],
        out_specs=seq,
        out_shape=jax.ShapeDtypeStruct((B, H, T, LANES), BF16),
        scratch_shapes=[pltpu.VMEM((T, LANES), F32),
                        pltpu.VMEM((A_DV, A_DK), F32),
                        pltpu.VMEM((A_DV, A_DK), F32),
                        pltpu.VMEM((3, 2 * per_step, 4, HGRN_CHUNK, LANES), BF16),
                        pltpu.VMEM((3, 2 * per_step, 8, LANES), F32),
                        pltpu.VMEM((2, 2 * per_step, HGRN_CHUNK, HGRN_CHUNK), BF16),
                        pltpu.VMEM((2, 2 * per_step, A_DV, A_DK), F32)],
        compiler_params=pltpu.CompilerParams(
            dimension_semantics=("arbitrary", "arbitrary"), vmem_limit_bytes=VMEM_LIMIT),
        name="hgrn",
    )(qa, lff, lfb, va, ga, cvec, nw,
      jnp.asarray(tabf, BF16), jnp.asarray(tabb, BF16),
      jnp.asarray(ftabf, BF16), jnp.asarray(ftabb, BF16), jnp.asarray(lvf), jnp.asarray(lvb))


def _ret_kernel(q_ref, k_ref, v_ref, g_ref, lg_ref, y_ref, o_scr, s_scr):
    C = RET_CHUNK
    T = q_ref.shape[2]
    n_chunks = T // C
    s_scr[...] = jnp.zeros_like(s_scr)
    lane = lax.broadcasted_iota(jnp.int32, (1, LANES), 1)
    head_mask = [(lane % (LANES // 2)) < (LANES // 4), (lane % (LANES // 2)) >= (LANES // 4)]
    tpos = lax.broadcasted_iota(jnp.int32, (C, C), 0)
    spos = lax.broadcasted_iota(jnp.int32, (C, C), 1)
    pos = lax.broadcasted_iota(jnp.int32, (C, 1), 0).astype(F32)

    tables = []
    for d in range(2):
        dist = ((tpos - spos) if d == 0 else (spos - tpos)).astype(F32)
        after = pos if d == 0 else (C - 1.0 - pos)
        per_head = []
        for i in range(2):
            lg = lg_ref[0, 2 * d + i:2 * d + i + 1, :]
            decay = jnp.exp(jnp.where(dist >= 0, dist * lg, NEG_BIG))
            q_dec = jnp.exp((after + 1.0) * lg)
            k_dec = jnp.exp((C - 1.0 - after) * lg)
            s_dec = jnp.exp(C * lg)
            per_head.append((decay, q_dec, k_dec, s_dec))
        tables.append(per_head)

    def chunk(ci, d):
        r0 = pl.multiple_of(ci * C, C)
        rows = pl.ds(r0, C)
        q = q_ref[0, 0, rows, :]
        k = k_ref[0, 0, rows, :]
        qf = q.astype(F32)
        kf = k.astype(F32)
        outs = []
        for i in range(2):
            decay, q_dec, k_dec, s_dec = tables[d][i]
            v = v_ref[0, i, rows, :]
            qm = jnp.where(head_mask[i], qf, 0.0)
            scores = _dot_nt(qm.astype(BF16), k) * decay
            st = s_scr[2 * d + i]
            o = _dot(scores.astype(BF16), v) + _dot_nt((qm * q_dec).astype(BF16), st.astype(BF16))
            km = jnp.where(head_mask[i], kf * k_dec, 0.0).astype(BF16)
            s_scr[2 * d + i] = st * s_dec + _dot_tn(v, km)
            outs.append(o)
        return rows, outs

    def finalize(rows, outs):
        for i in range(2):
            tot = o_scr[i, rows, :] + outs[i]
            mu = jnp.mean(tot, axis=-1, keepdims=True)
            cen = tot - mu
            var = jnp.mean(cen * cen, axis=-1, keepdims=True)
            y = cen * lax.rsqrt(var + LN_EPS) * g_ref[0, i, rows, :].astype(F32)
            y_ref[0, i, rows, :] = y.astype(BF16)

    def first_half(j, carry):
        for d, ci in ((0, j), (1, n_chunks - 1 - j)):
            rows, outs = chunk(ci, d)
            for i in range(2):
                o_scr[i, rows, :] = outs[i]
        return carry

    def second_half(j, carry):
        for d, ci in ((0, j), (1, n_chunks - 1 - j)):
            rows, outs = chunk(ci, d)
            finalize(rows, outs)
        return carry

    lax.fori_loop(0, n_chunks // 2, first_half, 0)
    lax.fori_loop(n_chunks // 2, n_chunks, second_half, 0)


def _ret_call(qr, kr, vr, gr, lg_rows):
    B, P, T, _ = qr.shape
    seq1 = pl.BlockSpec((1, 1, T, LANES), lambda b, p: (b, p, 0, 0))
    seq2 = pl.BlockSpec((1, 2, T, LANES), lambda b, p: (b, p, 0, 0))
    return pl.pallas_call(
        _ret_kernel,
        grid=(B, P),
        in_specs=[seq1, seq1, seq2, seq2,
                  pl.BlockSpec((1, 4, LANES), lambda b, p: (p, 0, 0))],
        out_specs=seq2,
        out_shape=jax.ShapeDtypeStruct((B, R_HEADS, T, LANES), BF16),
        scratch_shapes=[pltpu.VMEM((2, T, LANES), F32),
                        pltpu.VMEM((4, R_DV, LANES), F32)],
        compiler_params=pltpu.CompilerParams(
            dimension_semantics=("arbitrary", "arbitrary"), vmem_limit_bytes=VMEM_LIMIT),
        name="retention",
    )(qr, kr, vr, gr, lg_rows)


def _outproj_kernel(ya_ref, yr_ref, mg_ref, x_ref, gate_ref, wpa_ref, wpb_ref, wout_ref,
                    lng_ref, lnb_ref, o_ref):
    ya = jnp.concatenate([ya_ref[0, h] for h in range(A_HEADS)], axis=1)
    yr = jnp.concatenate([yr_ref[0, h] for h in range(R_HEADS)], axis=1)
    pa = _dot(ya, wpa_ref[...])
    pr = _dot(yr, wpb_ref[...])
    mg = mg_ref[0]
    m = mg[:, :D_MODEL].astype(F32) * pa + mg[:, D_MODEL:].astype(F32) * pr
    s = _dot(m.astype(BF16), wout_ref[...])
    h = DEEPNORM_ALPHA * x_ref[0] + gate_ref[0] * s
    mu = jnp.mean(h, axis=-1, keepdims=True)
    cen = h - mu
    var = jnp.mean(cen * cen, axis=-1, keepdims=True)
    o_ref[0] = cen * lax.rsqrt(var + LN_EPS) * lng_ref[...] + lnb_ref[...]


def _outproj_call(ya, yr, mg, x, gate1p, w_pa, w_pb, w_out, ln_g, ln_b):
    B, T, D = x.shape
    rows = min(OUT_ROWS, T)
    hspec = pl.BlockSpec((1, A_HEADS, rows, LANES), lambda b, t: (b, 0, t, 0))
    return pl.pallas_call(
        _outproj_kernel,
        grid=(B, T // rows),
        in_specs=[hspec, hspec,
                  pl.BlockSpec((1, rows, 2 * D), lambda b, t: (b, t, 0)),
                  pl.BlockSpec((1, rows, D), lambda b, t: (b, t, 0)),
                  pl.BlockSpec((1, 1, D), lambda b, t: (b, 0, 0)),
                  _const_spec((D, D)), _const_spec((D, D)), _const_spec((D, D)),
                  _const_spec((1, D)), _const_spec((1, D))],
        out_specs=pl.BlockSpec((1, rows, D), lambda b, t: (b, t, 0)),
        out_shape=jax.ShapeDtypeStruct((B, T, D), F32),
        compiler_params=pltpu.CompilerParams(
            dimension_semantics=("arbitrary", "arbitrary"), vmem_limit_bytes=VMEM_LIMIT),
        name="outproj",
    )(ya, yr, mg, x, gate1p, w_pa, w_pb, w_out, ln_g, ln_b)


def _rope_perm():
    half = R_DK // 2
    perm = []
    for p in range(R_PAIRS):
        for part in range(2):
            for i in range(2):
                h = 2 * p + i
                perm.extend(h * R_DK + part * half + np.arange(half))
    return np.asarray(perm, np.int32)


def _rope_tables(T):
    half = R_DK // 2
    pos = jnp.arange(T, dtype=F32)
    inv = ROPE_BASE ** (-jnp.arange(0, R_DK, 2, dtype=F32) / R_DK)
    ang = pos[:, None] * inv[None, :]
    cos, sin = jnp.cos(ang), jnp.sin(ang)
    return jnp.tile(cos, (1, LANES // half)), jnp.concatenate([-sin, -sin, sin, sin], axis=1)


def _trunk(x, ada, params):
    B, T, D = x.shape
    cos_p, sin_p = _rope_tables(T)
    for l in range(DEPTH):
        p = params[l]
        shift = ada[l, :, :D].reshape(B, 1, D)
        sc1p = (1.0 + ada[l, :, D:2 * D]).reshape(B, 1, D)
        gate1p = (1.0 + ada[l, :, 2 * D:]).reshape(B, 1, D)
        qa, lff, lfb, va, ga, qr, kr, vr, gr, mg = _inproj_call(
            x, sc1p, shift, p["w_in"], p["w_mg"], p["b_mg"], p["lbm"], p["oml"], cos_p, sin_p)
        ya = _hgrn_call(qa, lff, lfb, va, ga, p["cvec"], p["nw"])
        yr = _ret_call(qr, kr, vr, gr, p["lg_rows"])
        x = _outproj_call(ya, yr, mg, x, gate1p, p["w_pa"], p["w_pb"], p["w_out"],
                          p["ln_g"], p["ln_b"])
    return x


def kernel(x_prompt, x_sample, c_prompt, c_sample, w_ada, b_ada, w_in, hgrn_lb, a_norm_w, ret_decay,
           w_pa, w_pb, w_mg, b_mg, w_out, ln_g, ln_b):
    D = D_MODEL
    pr = jax.nn.softmax(hgrn_lb.astype(F32), axis=0)
    lbs = jnp.cumsum(pr, axis=0) - pr[0:1]
    perm = _rope_perm()
    params = []
    for l in range(DEPTH):
        lb = lbs[l]
        lbm = jnp.maximum(lb, LOG_FLOOR)
        wl = w_in[l]
        wl = jnp.concatenate([wl[:, :OFF_RQ], wl[:, OFF_RQ + perm], wl[:, OFF_RK + perm],
                              wl[:, OFF_RV:]], axis=1)
        lg = jax.nn.log_sigmoid(ret_decay[l].astype(F32))
        lg_rows = jnp.broadcast_to(
            lg.reshape(2, R_PAIRS, 2).transpose(1, 0, 2).reshape(R_PAIRS, 4, 1), (R_PAIRS, 4, LANES))
        params.append(dict(
            w_in=wl.astype(BF16), w_mg=w_mg[l].astype(BF16), b_mg=b_mg[l].reshape(1, 2 * D),
            lbm=lbm.reshape(1, D), oml=(1.0 - lb).reshape(1, D),
            cvec=((1.0 - lb) + lbm).reshape(A_HEADS, 1, A_DK),
            nw=a_norm_w[l].reshape(1, A_DV), lg_rows=lg_rows,
            w_pa=w_pa[l].astype(BF16), w_pb=w_pb[l].astype(BF16), w_out=w_out[l].astype(BF16),
            ln_g=ln_g[l].reshape(1, D), ln_b=ln_b[l].reshape(1, D)))
    nbp = x_prompt.shape[0]
    ada = _ada_call(jnp.concatenate([c_prompt, c_sample], axis=0), w_ada, b_ada)
    y_prompt = _trunk(x_prompt, ada[:, :nbp], params)
    y_sample = _trunk(x_sample, ada[:, nbp:], params)
    return (y_prompt, y_sample)
```

```python
import functools

import numpy as np
import jax
import jax.numpy as jnp
from jax import lax
from jax.experimental import pallas as pl
from jax.experimental.pallas import tpu as pltpu

F32 = jnp.float32
BF16 = jnp.bfloat16

D_MODEL = 1024
DEPTH = 2
A_HEADS = 8
A_DK = 128
A_DV = 128
R_HEADS = 8
R_DK = 64
R_DV = 128
R_PAIRS = R_HEADS // 2
LN_EPS = 1e-5
LOG_FLOOR = 1e-30
NEG_BIG = -1e30
DEEPNORM_ALPHA = (2.0 * DEPTH) ** 0.25
ROPE_BASE = 10000.0

OFF_AQ, OFF_FF, OFF_FB, OFF_AI, OFF_AG, OFF_RQ, OFF_RK, OFF_RV, OFF_RG = (
    0, 1024, 2048, 3072, 4096, 5120, 5632, 6144, 7168)
IN_WIDTH = 8192

LANES = 128
HGRN_CHUNK = 64
HGRN_LEVELS = 6
RET_CHUNK = 128
HGRN_SAFE_DECAY = 75.0
HGRN_SAFE_MAG = 1e5
HGRN_FAST_UNROLL = 4
RET_STEP_CHUNKS = 2
IN_ROWS = 256
OUT_ROWS = 512
VMEM_LIMIT = 56 * 1024 * 1024


def _dot(a, b):
    return jnp.dot(a, b, preferred_element_type=F32)


def _dot_nt(a, b):
    return lax.dot_general(a, b, (((1,), (1,)), ((), ())), preferred_element_type=F32)


def _dot_tn(a, b):
    return lax.dot_general(a, b, (((0,), (0,)), ((), ())), preferred_element_type=F32)


def _sigmoid(z):
    return 1.0 / (1.0 + jnp.exp(-z))


def _const_spec(shape):
    nd = len(shape)
    return pl.BlockSpec(shape, lambda *_: (0,) * nd, pipeline_mode=pl.Buffered(1))


def _ada_kernel(c_ref, w_ref, b_ref, o_ref):
    c = c_ref[...]
    a = c * _sigmoid(c)
    o_ref[0] = jnp.dot(a, w_ref[0], preferred_element_type=F32,
                       precision=lax.Precision.HIGHEST) + b_ref[0]


def _ada_call(c, w_ada, b_ada):
    nb = c.shape[0]
    d = D_MODEL
    return pl.pallas_call(
        _ada_kernel,
        grid=(DEPTH, 3),
        in_specs=[
            pl.BlockSpec((nb, d), lambda l, j: (0, 0)),
            pl.BlockSpec((1, d, d), lambda l, j: (l, 0, j)),
            pl.BlockSpec((1, 1, d), lambda l, j: (l, 0, j)),
        ],
        out_specs=pl.BlockSpec((1, nb, d), lambda l, j: (l, 0, j)),
        out_shape=jax.ShapeDtypeStruct((DEPTH, nb, 3 * d), F32),
        compiler_params=pltpu.CompilerParams(
            dimension_semantics=("arbitrary", "arbitrary"), vmem_limit_bytes=VMEM_LIMIT),
        name="adaln",
    )(c, w_ada, b_ada.reshape(DEPTH, 1, 3 * d))


def _inproj_kernel(x_ref, sc_ref, sh_ref, win_ref, wmg_ref, bmg_ref, lbm_ref, oml_ref,
                   cos_ref, sin_ref,
                   qa_ref, lff_ref, lfb_ref, va_ref, ga_ref, qr_ref, kr_ref, vr_ref, gr_ref,
                   mg_ref):
    u = (x_ref[0] * sc_ref[0] + sh_ref[0]).astype(BF16)

    def proj(c0):
        return _dot(u, win_ref[:, c0:c0 + 2 * LANES])

    def halves(z):
        return z[:, :LANES], z[:, LANES:]

    for g in range(A_HEADS // 2):
        c = 2 * LANES * g
        z = proj(OFF_AQ + c)
        z = z * _sigmoid(z) * (A_DK ** -0.5)
        for i, zz in enumerate(halves(z)):
            qa_ref[0, 2 * g + i] = zz.astype(BF16)
        for off, ref in ((OFF_FF, lff_ref), (OFF_FB, lfb_ref)):
            z = proj(off + c)
            f = lbm_ref[:, c:c + 2 * LANES] + oml_ref[:, c:c + 2 * LANES] * _sigmoid(z)
            lf = jnp.log(f)
            for i, zz in enumerate(halves(lf)):
                ref[0, 2 * g + i] = zz
        z = proj(OFF_AI + c)
        for i, zz in enumerate(halves(z)):
            va_ref[0, 2 * g + i] = zz.astype(BF16)
        z = proj(OFF_AG + c)
        z = z * _sigmoid(z)
        for i, zz in enumerate(halves(z)):
            ga_ref[0, 2 * g + i] = zz.astype(BF16)
        z = proj(OFF_RV + c)
        for i, zz in enumerate(halves(z)):
            vr_ref[0, 2 * g + i] = zz.astype(BF16)
        z = proj(OFF_RG + c)
        z = z * _sigmoid(z)
        for i, zz in enumerate(halves(z)):
            gr_ref[0, 2 * g + i] = zz.astype(BF16)

    cosv = cos_ref[...]
    sinv = sin_ref[...]
    for off, ref, scale in ((OFF_RQ, qr_ref, R_DK ** -0.5), (OFF_RK, kr_ref, 1.0)):
        for g in range(R_PAIRS // 2):
            z = proj(off + 2 * LANES * g)
            for i, zz in enumerate(halves(z)):
                rot = pltpu.roll(zz, LANES // 2, 1)
                ref[0, 2 * g + i] = ((zz * cosv + rot * sinv) * scale).astype(BF16)

    for g in range(2 * D_MODEL // (2 * LANES)):
        c = 2 * LANES * g
        z = _dot(u, wmg_ref[:, c:c + 2 * LANES]) + bmg_ref[:, c:c + 2 * LANES]
        mg_ref[0, :, c:c + 2 * LANES] = _sigmoid(z).astype(BF16)


def _inproj_call(x, sc1p, shift, w_in, w_mg, b_mg, lbm, oml, cos_p, sin_p):
    B, T, D = x.shape
    rows = min(IN_ROWS, T)
    grid = (B, T // rows)
    hd = lambda n, dt: jax.ShapeDtypeStruct((B, n, T, LANES), dt)
    hspec = lambda n: pl.BlockSpec((1, n, rows, LANES), lambda b, t: (b, 0, t, 0))
    bvec = pl.BlockSpec((1, 1, D), lambda b, t: (b, 0, 0))
    return pl.pallas_call(
        _inproj_kernel,
        grid=grid,
        in_specs=[
            pl.BlockSpec((1, rows, D), lambda b, t: (b, t, 0)),
            bvec, bvec,
            _const_spec((D, IN_WIDTH)),
            _const_spec((D, 2 * D)),
            _const_spec((1, 2 * D)),
            _const_spec((1, D)),
            _const_spec((1, D)),
            pl.BlockSpec((rows, LANES), lambda b, t: (t, 0)),
            pl.BlockSpec((rows, LANES), lambda b, t: (t, 0)),
        ],
        out_specs=[hspec(A_HEADS), hspec(A_HEADS), hspec(A_HEADS), hspec(A_HEADS), hspec(A_HEADS),
                   hspec(R_PAIRS), hspec(R_PAIRS), hspec(R_HEADS), hspec(R_HEADS),
                   pl.BlockSpec((1, rows, 2 * D), lambda b, t: (b, t, 0))],
        out_shape=[hd(A_HEADS, BF16), hd(A_HEADS, F32), hd(A_HEADS, F32), hd(A_HEADS, BF16),
                   hd(A_HEADS, BF16), hd(R_PAIRS, BF16), hd(R_PAIRS, BF16), hd(R_HEADS, BF16),
                   hd(R_HEADS, BF16), jax.ShapeDtypeStruct((B, T, 2 * D), BF16)],
        compiler_params=pltpu.CompilerParams(
            dimension_semantics=("arbitrary", "arbitrary"), vmem_limit_bytes=VMEM_LIMIT),
        name="inproj",
    )(x, sc1p, shift, w_in, w_mg, b_mg, lbm, oml, cos_p, sin_p)


def _hgrn_tables(reverse):
    C = HGRN_CHUNK
    idx = np.arange(C)
    cum = (idx[None, :] >= idx[:, None]) if reverse else (idx[None, :] <= idx[:, None])
    cum = cum.astype(np.float64)
    end = 0 if reverse else C - 1
    blocks = []
    for j in range(HGRN_LEVELS):
        m = C >> (j + 1)
        base = idx - idx % (2 * m)
        ref = base + (m if reverse else m - 1)
        blocks.append(cum - cum[ref])
    blocks.append(cum)
    blocks.append(cum[end][None, :] - cum)
    safe_tab = np.concatenate(blocks, axis=0)
    mid = C // 2 if reverse else C // 2 - 1
    fast_tab = np.concatenate([cum - cum[mid][None, :], np.tile(cum[mid][None, :], (8, 1))], axis=0)
    x = idx[:, None] ^ idx[None, :]
    bl = np.zeros_like(x)
    nz = x > 0
    bl[nz] = np.floor(np.log2(x[nz])).astype(np.int64) + 1
    lvl = HGRN_LEVELS - bl
    coupled = (idx[:, None] <= idx[None, :]) if reverse else (idx[:, None] >= idx[None, :])
    lvl = np.where(coupled, lvl, -1)
    return safe_tab.astype(np.float32), fast_tab.astype(np.float32), lvl.astype(np.int32), end


def _hgrn_kernel(q_ref, lff_ref, lfb_ref, v_ref, g_ref, c_ref, nw_ref,
                 tabf_ref, tabb_ref, ftabf_ref, ftabb_ref, lvf_ref, lvb_ref,
                 y_ref, o_scr, sf_scr, sb_scr, ops_scr, dec_scr, a_scr, u_scr, *, ends):
    C = HGRN_CHUNK
    T = q_ref.shape[2]
    n_chunks = T // C
    sf_scr[...] = jnp.zeros_like(sf_scr)
    sb_scr[...] = jnp.zeros_like(sb_scr)
    cvec = c_ref[0]
    nw = nw_ref[...]

    def load(ci, lf_ref):
        rows = pl.ds(pl.multiple_of(ci * C, C), C)
        lf = lf_ref[0, 0, rows, :]
        q = q_ref[0, 0, rows, :].astype(F32)
        v = v_ref[0, 0, rows, :]
        k = cvec - jnp.exp(lf)
        hi = lf.astype(BF16)
        lo = (lf - hi.astype(F32)).astype(BF16)
        return rows, q, k, v, jnp.concatenate([hi, lo], axis=1)

    def advance(s_scr, a, v, qi, kd, s_decay):
        st = s_scr[...]
        o = _dot(a.astype(BF16), v) + _dot_nt(qi, st.astype(BF16))
        s_scr[...] = st * s_decay + _dot_tn(v, kd)
        return o


    def chunk_rows(ci):
        return pl.ds(pl.multiple_of(ci * C, C), C)

    def prepare_issue(chains):
        loaded = [load(ci, p[0]) for ci, p in chains]
        return loaded, [_dot(p[1][...], ld[4]) for (_, p), ld in zip(chains, loaded)]

    def prepare_store(chains, issued, slot):
        for c, ((_, p), (_, q, k, _, _), r) in enumerate(zip(chains, *issued)):
            dm = r[:, :LANES] + r[:, LANES:]
            d = dm[:C]
            b_mid = dm[C:C + 1]
            d_end = d[p[4]:p[4] + 1]
            qh = q * jnp.exp(d)
            kh = k * jnp.exp(-d)
            ops_scr[slot, c, 0] = qh.astype(BF16)
            ops_scr[slot, c, 1] = kh.astype(BF16)
            ops_scr[slot, c, 2] = (qh * jnp.exp(b_mid)).astype(BF16)
            ops_scr[slot, c, 3] = (kh * jnp.exp(d_end)).astype(BF16)
            dec_scr[slot, c] = jnp.broadcast_to(jnp.exp(b_mid + d_end), (8, LANES))

    def scores_issue(chains, slot):
        scores = [_dot_nt(ops_scr[slot, c, 0], ops_scr[slot, c, 1]) for c in range(len(chains))]
        updates = [_dot_tn(v_ref[0, 0, chunk_rows(ci), :], ops_scr[slot, c, 3])
                   for c, (ci, _) in enumerate(chains)]
        return scores, updates

    def scores_store(chains, issued, slot):
        for c, ((_, p), sc, u) in enumerate(zip(chains, *issued)):
            a_scr[slot, c] = jnp.where(p[2][...] >= 0, sc, 0.0).astype(BF16)
            u_scr[slot, c] = u

    def finish_issue(chains, ops_slot, slot):
        outs = []
        for c, (ci, p) in enumerate(chains):
            rows = chunk_rows(ci)
            st = p[3][...]
            o = (_dot(a_scr[slot, c], v_ref[0, 0, rows, :])
                 + _dot_nt(ops_scr[ops_slot, c, 2], st.astype(BF16)))
            p[3][...] = st * dec_scr[ops_slot, c, 0:1, :] + u_scr[slot, c]
            outs.append((rows, o))
        return outs

    def safe_chunk(ci, lf_ref, tab_ref, lv_ref, s_scr, end):
        rows, q, k, v, hl = load(ci, lf_ref)
        r = _dot(tab_ref[...], hl)
        dm = r[:, :LANES] + r[:, LANES:]
        lv = lv_ref[...]
        a = jnp.where(lv == HGRN_LEVELS, _dot_nt(q.astype(BF16), k.astype(BF16)), 0.0)
        for j in range(HGRN_LEVELS):
            e = jnp.exp(-jnp.abs(dm[j * C:(j + 1) * C]))
            aj = _dot_nt((q * e).astype(BF16), (k * e).astype(BF16))
            a = jnp.where(lv == j, aj, a)
        bq = dm[HGRN_LEVELS * C:(HGRN_LEVELS + 1) * C]
        bk = dm[(HGRN_LEVELS + 1) * C:(HGRN_LEVELS + 2) * C]
        qi = (q * jnp.exp(bq)).astype(BF16)
        kd = (k * jnp.exp(bk)).astype(BF16)
        return rows, advance(s_scr, a, v, qi, kd, jnp.exp(bq[end:end + 1, :]))

    def finalize(rows, o):
        tot = o_scr[rows, :] + o
        ms = jnp.mean(tot * tot, axis=-1, keepdims=True)
        y = tot * lax.rsqrt(ms + LN_EPS) * nw * g_ref[0, 0, rows, :].astype(F32)
        y_ref[0, 0, rows, :] = y.astype(BF16)

    def first_visit(rows, o):
        o_scr[rows, :] = o

    def chain_groups(tf_ref, tb_ref, per_step):
        fwd = (lff_ref, tf_ref, lvf_ref, sf_scr, ends[0])
        bwd = (lfb_ref, tb_ref, lvb_ref, sb_scr, ends[1])

        def chains(g):
            out = []
            for u in range(per_step):
                i = g * per_step + u
                out += [(i, fwd), (n_chunks - 1 - i, bwd)]
            return out
        return chains

    def loop(lo, hi, body, finish):
        def wrapped(g, carry):
            body(g, finish)
            return carry
        lax.fori_loop(lo, hi, wrapped, 0)

    def run_safe():
        chains = chain_groups(tabf_ref, tabb_ref, 1)

        def step(g, finish):
            for ci, p in chains(g):
                finish(*safe_chunk(ci, *p))

        loop(0, n_chunks // 2, step, first_visit)
        loop(n_chunks // 2, n_chunks, step, finalize)

    def run_fast():
        per_step = ops_scr.shape[1] // 2
        chains = chain_groups(ftabf_ref, ftabb_ref, per_step)
        n_groups = n_chunks // per_step
        last = n_groups - 1

        def step(g, finish):
            g1 = jnp.minimum(g + 1, last)
            g2 = jnp.minimum(g + 2, last)
            sc = scores_issue(chains(g1), g1 % 3)
            pr = prepare_issue(chains(g2))
            outs = finish_issue(chains(g), g % 3, g % 2)
            scores_store(chains(g1), sc, (g + 1) % 2)
            prepare_store(chains(g2), pr, (g + 2) % 3)
            for rows, o in outs:
                finish(rows, o)

        for g in range(2):
            prepare_store(chains(g), prepare_issue(chains(g)), g)
        scores_store(chains(0), scores_issue(chains(0), 0), 0)
        loop(0, n_groups // 2, step, first_visit)
        loop(n_groups // 2, n_groups, step, finalize)

    group = 8 if n_chunks % 8 == 0 else 1
    grows = group * C

    def scan(i, carry):
        dec, mag = carry
        rows = pl.ds(pl.multiple_of(i * grows, grows), grows)
        for lf_ref in (lff_ref, lfb_ref):
            s = jnp.sum(lf_ref[0, 0, rows, :].reshape(2 * group, C // 2, LANES), axis=1)
            dec = jnp.minimum(dec, jnp.min(s, axis=0, keepdims=True))
        for ref in (q_ref, v_ref):
            x = jnp.abs(ref[0, 0, rows, :].astype(F32))
            mag = jnp.maximum(mag, jnp.max(x, axis=0, keepdims=True))
        return dec, mag

    dec, mag = lax.fori_loop(0, n_chunks // group, scan,
                             (jnp.zeros((1, LANES), F32), jnp.zeros((1, LANES), F32)))
    is_safe = jnp.logical_and(jnp.min(dec) > -HGRN_SAFE_DECAY, jnp.max(mag) < HGRN_SAFE_MAG)

    pl.when(is_safe)(run_fast)
    pl.when(jnp.logical_not(is_safe))(run_safe)


def _hgrn_call(qa, lff, lfb, va, ga, cvec, nw):
    B, H, T, _ = qa.shape
    tabf, ftabf, lvf, endf = _hgrn_tables(False)
    tabb, ftabb, lvb, endb = _hgrn_tables(True)
    seq = pl.BlockSpec((1, 1, T, LANES), lambda b, h: (b, h, 0, 0))
    n_chunks = T // HGRN_CHUNK
    assert n_chunks % 4 == 0, T
    per_step = max(u for u in (HGRN_FAST_UNROLL, 2, 1) if n_chunks % (4 * u) == 0)
    return pl.pallas_call(
        functools.partial(_hgrn_kernel, ends=(endf, endb)),
        grid=(B, H),
        in_specs=[seq, seq, seq, seq, seq,
                  pl.BlockSpec((1, 1, LANES), lambda b, h: (h, 0, 0)),
                  _const_spec((1, LANES)),
                  _const_spec(tabf.shape), _const_spec(tabb.shape),
                  _const_spec(ftabf.shape), _const_spec(ftabb.shape),
                  _const_spec(lvf.shape), _const_spec(lvb.shape)],
        out_specs=seq,
        out_shape=jax.ShapeDtypeStruct((B, H, T, LANES), BF16),
        scratch_shapes=[pltpu.VMEM((T, LANES), F32),
                        pltpu.VMEM((A_DV, A_DK), F32),
                        pltpu.VMEM((A_DV, A_DK), F32),
                        pltpu.VMEM((3, 2 * per_step, 4, HGRN_CHUNK, LANES), BF16),
                        pltpu.VMEM((3, 2 * per_step, 8, LANES), F32),
                        pltpu.VMEM((2, 2 * per_step, HGRN_CHUNK, HGRN_CHUNK), BF16),
                        pltpu.VMEM((2, 2 * per_step, A_DV, A_DK), F32)],
        compiler_params=pltpu.CompilerParams(
            dimension_semantics=("arbitrary", "arbitrary"), vmem_limit_bytes=VMEM_LIMIT),
        name="hgrn",
    )(qa, lff, lfb, va, ga, cvec, nw,
      jnp.asarray(tabf, BF16), jnp.asarray(tabb, BF16),
      jnp.asarray(ftabf, BF16), jnp.asarray(ftabb, BF16), jnp.asarray(lvf), jnp.asarray(lvb))


def _ret_kernel(q_ref, k_ref, v_ref, g_ref, lg_ref, y_ref,
                o_scr, s_scr, dec_scr, qd_scr, kd_scr, a_scr, u_scr):
    C = RET_CHUNK
    T = q_ref.shape[2]
    n_chunks = T // C
    per_step = a_scr.shape[1] // 4
    n_groups = n_chunks // per_step
    s_scr[...] = jnp.zeros_like(s_scr)

    lane = lax.broadcasted_iota(jnp.int32, (1, LANES), 1)
    head_mask = [(lane % (LANES // 2)) < (LANES // 4), (lane % (LANES // 2)) >= (LANES // 4)]
    tpos = lax.broadcasted_iota(jnp.int32, (C, C), 0)
    spos = lax.broadcasted_iota(jnp.int32, (C, C), 1)
    pos = lax.broadcasted_iota(jnp.int32, (C, 1), 0).astype(F32)
    s_dec = []
    for d in range(2):
        dist = ((tpos - spos) if d == 0 else (spos - tpos)).astype(F32)
        after = pos if d == 0 else (C - 1.0 - pos)
        for i in range(2):
            lg = lg_ref[0, 2 * d + i:2 * d + i + 1, :]
            dec_scr[2 * d + i] = jnp.exp(jnp.where(dist >= 0, dist * lg, NEG_BIG))
            qd_scr[2 * d + i] = jnp.where(head_mask[i], jnp.exp((after + 1.0) * lg), 0.0).astype(BF16)
            kd_scr[2 * d + i] = jnp.where(head_mask[i], jnp.exp((C - 1.0 - after) * lg), 0.0).astype(BF16)
            s_dec.append(jnp.exp(C * lg))

    def chunk_rows(ci):
        return pl.ds(pl.multiple_of(ci * C, C), C)

    def units(g):
        out = []
        for u in range(per_step):
            j = g * per_step + u
            out += [(j, 0), (n_chunks - 1 - j, 1)]
        return out

    def scores_issue(us):
        out = []
        for ci, d in us:
            rows = chunk_rows(ci)
            q = q_ref[0, 0, rows, :]
            k = k_ref[0, 0, rows, :]
            for i in range(2):
                qm = q * (qd_scr[2 * d + i] != 0).astype(BF16)
                km = k * kd_scr[2 * d + i]
                out.append((_dot_nt(qm, k), _dot_tn(km, v_ref[0, i, rows, :])))
        return out

    def scores_store(us, issued, slot):
        for n, ((ci, d), i) in enumerate((u, i) for u in us for i in range(2)):
            sc, upd = issued[n]
            a_scr[slot, n] = (sc * dec_scr[2 * d + i]).astype(BF16)
            u_scr[slot, n] = upd

    def finish_issue(us, slot):
        outs = []
        for n, ((ci, d), i) in enumerate((u, i) for u in us for i in range(2)):
            rows = chunk_rows(ci)
            st = s_scr[2 * d + i]
            qd = q_ref[0, 0, rows, :] * qd_scr[2 * d + i]
            lhs = jnp.concatenate([a_scr[slot, n], qd], axis=1)
            rhs = jnp.concatenate([v_ref[0, i, rows, :], st.astype(BF16)], axis=0)
            outs.append((i, rows, _dot(lhs, rhs)))
            s_scr[2 * d + i] = st * s_dec[2 * d + i] + u_scr[slot, n]
        return outs

    def first_visit(i, rows, o):
        o_scr[i, rows, :] = o

    def finalize(i, rows, o):
        tot = o_scr[i, rows, :] + o
        mu = jnp.mean(tot, axis=-1, keepdims=True)
        cen = tot - mu
        var = jnp.mean(cen * cen, axis=-1, keepdims=True)
        y = cen * lax.rsqrt(var + LN_EPS) * g_ref[0, i, rows, :].astype(F32)
        y_ref[0, i, rows, :] = y.astype(BF16)

    def step(g, finish):
        nxt = units(jnp.minimum(g + 1, n_groups - 1))
        sc = scores_issue(nxt)
        outs = finish_issue(units(g), g % 2)
        scores_store(nxt, sc, (g + 1) % 2)
        for i, rows, o in outs:
            finish(i, rows, o)

    def loop(lo, hi, finish):
        def wrapped(g, carry):
            step(g, finish)
            return carry
        lax.fori_loop(lo, hi, wrapped, 0)

    scores_store(units(0), scores_issue(units(0)), 0)
    loop(0, n_groups // 2, first_visit)
    loop(n_groups // 2, n_groups, finalize)


def _ret_call(qr, kr, vr, gr, lg_rows):
    B, P, T, _ = qr.shape
    n_chunks = T // RET_CHUNK
    assert n_chunks % 2 == 0, T
    per_step = max(u for u in (RET_STEP_CHUNKS, 1) if n_chunks % (2 * u) == 0)
    n_units = 4 * per_step
    seq1 = pl.BlockSpec((1, 1, T, LANES), lambda b, p: (b, p, 0, 0))
    seq2 = pl.BlockSpec((1, 2, T, LANES), lambda b, p: (b, p, 0, 0))
    return pl.pallas_call(
        _ret_kernel,
        grid=(B, P),
        in_specs=[seq1, seq1, seq2, seq2,
                  pl.BlockSpec((1, 4, LANES), lambda b, p: (p, 0, 0))],
        out_specs=seq2,
        out_shape=jax.ShapeDtypeStruct((B, R_HEADS, T, LANES), BF16),
        scratch_shapes=[pltpu.VMEM((2, T, LANES), F32),
                        pltpu.VMEM((4, LANES, R_DV), F32),
                        pltpu.VMEM((4, RET_CHUNK, RET_CHUNK), F32),
                        pltpu.VMEM((4, RET_CHUNK, LANES), BF16),
                        pltpu.VMEM((4, RET_CHUNK, LANES), BF16),
                        pltpu.VMEM((2, n_units, RET_CHUNK, RET_CHUNK), BF16),
                        pltpu.VMEM((2, n_units, LANES, R_DV), F32)],
        compiler_params=pltpu.CompilerParams(
            dimension_semantics=("arbitrary", "arbitrary"), vmem_limit_bytes=VMEM_LIMIT),
        name="retention",
    )(qr, kr, vr, gr, lg_rows)


def _outproj_kernel(ya_ref, yr_ref, mg_ref, x_ref, gate_ref, wpa_ref, wpb_ref, wout_ref,
                    lng_ref, lnb_ref, o_ref):
    ya = jnp.concatenate([ya_ref[0, h] for h in range(A_HEADS)], axis=1)
    yr = jnp.concatenate([yr_ref[0, h] for h in range(R_HEADS)], axis=1)
    pa = _dot(ya, wpa_ref[...])
    pr = _dot(yr, wpb_ref[...])
    mg = mg_ref[0]
    m = mg[:, :D_MODEL].astype(F32) * pa + mg[:, D_MODEL:].astype(F32) * pr
    s = _dot(m.astype(BF16), wout_ref[...])
    h = DEEPNORM_ALPHA * x_ref[0] + gate_ref[0] * s
    mu = jnp.mean(h, axis=-1, keepdims=True)
    cen = h - mu
    var = jnp.mean(cen * cen, axis=-1, keepdims=True)
    o_ref[0] = cen * lax.rsqrt(var + LN_EPS) * lng_ref[...] + lnb_ref[...]


def _outproj_call(ya, yr, mg, x, gate1p, w_pa, w_pb, w_out, ln_g, ln_b):
    B, T, D = x.shape
    rows = min(OUT_ROWS, T)
    hspec = pl.BlockSpec((1, A_HEADS, rows, LANES), lambda b, t: (b, 0, t, 0))
    return pl.pallas_call(
        _outproj_kernel,
        grid=(B, T // rows),
        in_specs=[hspec, hspec,
                  pl.BlockSpec((1, rows, 2 * D), lambda b, t: (b, t, 0)),
                  pl.BlockSpec((1, rows, D), lambda b, t: (b, t, 0)),
                  pl.BlockSpec((1, 1, D), lambda b, t: (b, 0, 0)),
                  _const_spec((D, D)), _const_spec((D, D)), _const_spec((D, D)),
                  _const_spec((1, D)), _const_spec((1, D))],
        out_specs=pl.BlockSpec((1, rows, D), lambda b, t: (b, t, 0)),
        out_shape=jax.ShapeDtypeStruct((B, T, D), F32),
        compiler_params=pltpu.CompilerParams(
            dimension_semantics=("arbitrary", "arbitrary"), vmem_limit_bytes=VMEM_LIMIT),
        name="outproj",
    )(ya, yr, mg, x, gate1p, w_pa, w_pb, w_out, ln_g, ln_b)


def _rope_perm():
    half = R_DK // 2
    perm = []
    for p in range(R_PAIRS):
        for part in range(2):
            for i in range(2):
                h = 2 * p + i
                perm.extend(h * R_DK + part * half + np.arange(half))
    return np.asarray(perm, np.int32)


def _rope_tables(T):
    half = R_DK // 2
    pos = jnp.arange(T, dtype=F32)
    inv = ROPE_BASE ** (-jnp.arange(0, R_DK, 2, dtype=F32) / R_DK)
    ang = pos[:, None] * inv[None, :]
    cos, sin = jnp.cos(ang), jnp.sin(ang)
    return jnp.tile(cos, (1, LANES // half)), jnp.concatenate([-sin, -sin, sin, sin], axis=1)


def _trunk(x, ada, params):
    B, T, D = x.shape
    cos_p, sin_p = _rope_tables(T)
    for l in range(DEPTH):
        p = params[l]
        shift = ada[l, :, :D].reshape(B, 1, D)
        sc1p = (1.0 + ada[l, :, D:2 * D]).reshape(B, 1, D)
        gate1p = (1.0 + ada[l, :, 2 * D:]).reshape(B, 1, D)
        qa, lff, lfb, va, ga, qr, kr, vr, gr, mg = _inproj_call(
            x, sc1p, shift, p["w_in"], p["w_mg"], p["b_mg"], p["lbm"], p["oml"], cos_p, sin_p)
        ya = _hgrn_call(qa, lff, lfb, va, ga, p["cvec"], p["nw"])
        yr = _ret_call(qr, kr, vr, gr, p["lg_rows"])
        x = _outproj_call(ya, yr, mg, x, gate1p, p["w_pa"], p["w_pb"], p["w_out"],
                          p["ln_g"], p["ln_b"])
    return x


def kernel(x_prompt, x_sample, c_prompt, c_sample, w_ada, b_ada, w_in, hgrn_lb, a_norm_w, ret_decay,
           w_pa, w_pb, w_mg, b_mg, w_out, ln_g, ln_b):
    D = D_MODEL
    pr = jax.nn.softmax(hgrn_lb.astype(F32), axis=0)
    lbs = jnp.cumsum(pr, axis=0) - pr[0:1]
    perm = _rope_perm()
    params = []
    for l in range(DEPTH):
        lb = lbs[l]
        lbm = jnp.maximum(lb, LOG_FLOOR)
        wl = w_in[l]
        wl = jnp.concatenate([wl[:, :OFF_RQ], wl[:, OFF_RQ + perm], wl[:, OFF_RK + perm],
                              wl[:, OFF_RV:]], axis=1)
        lg = jax.nn.log_sigmoid(ret_decay[l].astype(F32))
        lg_rows = jnp.broadcast_to(
            lg.reshape(2, R_PAIRS, 2).transpose(1, 0, 2).reshape(R_PAIRS, 4, 1), (R_PAIRS, 4, LANES))
        params.append(dict(
            w_in=wl.astype(BF16), w_mg=w_mg[l].astype(BF16), b_mg=b_mg[l].reshape(1, 2 * D),
            lbm=lbm.reshape(1, D), oml=(1.0 - lb).reshape(1, D),
            cvec=((1.0 - lb) + lbm).reshape(A_HEADS, 1, A_DK),
            nw=a_norm_w[l].reshape(1, A_DV), lg_rows=lg_rows,
            w_pa=w_pa[l].astype(BF16), w_pb=w_pb[l].astype(BF16), w_out=w_out[l].astype(BF16),
            ln_g=ln_g[l].reshape(1, D), ln_b=ln_b[l].reshape(1, D)))
    nbp = x_prompt.shape[0]
    ada = _ada_call(jnp.concatenate([c_prompt, c_sample], axis=0), w_ada, b_ada)
    y_prompt = _trunk(x_prompt, ada[:, :nbp], params)
    y_sample = _trunk(x_sample, ada[:, nbp:], params)
    return (y_prompt, y_sample)
```

```python
import functools

import numpy as np
import jax
import jax.numpy as jnp
from jax import lax
from jax.experimental import pallas as pl
from jax.experimental.pallas import tpu as pltpu

F32 = jnp.float32
BF16 = jnp.bfloat16

D_MODEL = 1024
DEPTH = 2
A_HEADS = 8
A_DK = 128
A_DV = 128
R_HEADS = 8
R_DK = 64
R_DV = 128
R_PAIRS = R_HEADS // 2
LN_EPS = 1e-5
LOG_FLOOR = 1e-30
NEG_BIG = -1e30
DEEPNORM_ALPHA = (2.0 * DEPTH) ** 0.25
ROPE_BASE = 10000.0

OFF_AQ, OFF_FF, OFF_FB, OFF_AI, OFF_AG, OFF_RQ, OFF_RK, OFF_RV, OFF_RG = (
    0, 1024, 2048, 3072, 4096, 5120, 5632, 6144, 7168)
IN_WIDTH = 8192

LANES = 128
HGRN_CHUNK = 64
HGRN_LEVELS = 6
RET_CHUNK = 128
HGRN_SAFE_DECAY = 75.0
HGRN_SAFE_MAG = 1e5
HGRN_FAST_UNROLL = 8
HGRN_ROLL_LAG = 12
RET_STEP_CHUNKS = 2
IN_ROWS = 256
OUT_ROWS = 512
VMEM_LIMIT = 56 * 1024 * 1024


def _dot(a, b):
    return jnp.dot(a, b, preferred_element_type=F32)


def _dot_nt(a, b):
    return lax.dot_general(a, b, (((1,), (1,)), ((), ())), preferred_element_type=F32)


def _dot_tn(a, b):
    return lax.dot_general(a, b, (((0,), (0,)), ((), ())), preferred_element_type=F32)


def _sigmoid(z):
    return 1.0 / (1.0 + jnp.exp(-z))


def _const_spec(shape):
    nd = len(shape)
    return pl.BlockSpec(shape, lambda *_: (0,) * nd, pipeline_mode=pl.Buffered(1))


def _ada_kernel(c_ref, w_ref, b_ref, o_ref):
    c = c_ref[...]
    a = c * _sigmoid(c)
    o_ref[0] = jnp.dot(a, w_ref[0], preferred_element_type=F32,
                       precision=lax.Precision.HIGHEST) + b_ref[0]


def _ada_call(c, w_ada, b_ada):
    nb = c.shape[0]
    d = D_MODEL
    return pl.pallas_call(
        _ada_kernel,
        grid=(DEPTH, 3),
        in_specs=[
            pl.BlockSpec((nb, d), lambda l, j: (0, 0)),
            pl.BlockSpec((1, d, d), lambda l, j: (l, 0, j)),
            pl.BlockSpec((1, 1, d), lambda l, j: (l, 0, j)),
        ],
        out_specs=pl.BlockSpec((1, nb, d), lambda l, j: (l, 0, j)),
        out_shape=jax.ShapeDtypeStruct((DEPTH, nb, 3 * d), F32),
        compiler_params=pltpu.CompilerParams(
            dimension_semantics=("arbitrary", "arbitrary"), vmem_limit_bytes=VMEM_LIMIT),
        name="adaln",
    )(c, w_ada, b_ada.reshape(DEPTH, 1, 3 * d))


def _inproj_kernel(x_ref, sc_ref, sh_ref, win_ref, wmg_ref, bmg_ref, lbm_ref, oml_ref,
                   cos_ref, sin_ref,
                   qa_ref, hif_ref, lof_ref, kf_ref, hib_ref, lob_ref, kb_ref, va_ref, ga_ref,
                   qr_ref, kr_ref, vr_ref, gr_ref, mg_ref, dstat_ref, mstat_ref):
    rows = x_ref.shape[1]
    u = (x_ref[0] * sc_ref[0] + sh_ref[0]).astype(BF16)

    def col_max(z):
        return jnp.max(jnp.abs(z), axis=0, keepdims=True)

    def proj(c0):
        return _dot(u, win_ref[:, c0:c0 + 2 * LANES])

    def halves(z):
        return z[:, :LANES], z[:, LANES:]

    for g in range(A_HEADS // 2):
        c = 2 * LANES * g
        z = proj(OFF_AQ + c)
        z = z * _sigmoid(z) * (A_DK ** -0.5)
        mag = col_max(z)
        for i, zz in enumerate(halves(z)):
            qa_ref[0, 2 * g + i] = zz.astype(BF16)
        lbm = lbm_ref[:, c:c + 2 * LANES]
        oml = oml_ref[:, c:c + 2 * LANES]
        dmin = None
        for off, hi_ref, lo_ref, k_ref in ((OFF_FF, hif_ref, lof_ref, kf_ref),
                                           (OFF_FB, hib_ref, lob_ref, kb_ref)):
            f = lbm + oml * _sigmoid(proj(off + c))
            lf = jnp.log(f)
            hi = lf.astype(BF16)
            lo = (lf - hi.astype(F32)).astype(BF16)
            kk = ((oml + lbm) - f).astype(BF16)
            halfsum = jnp.sum(lf.reshape(2 * rows // HGRN_CHUNK, HGRN_CHUNK // 2, 2 * LANES), axis=1)
            halfmin = jnp.min(halfsum, axis=0, keepdims=True)
            dmin = halfmin if dmin is None else jnp.minimum(dmin, halfmin)
            for i in range(2):
                lanes = slice(i * LANES, (i + 1) * LANES)
                hi_ref[0, 2 * g + i] = hi[:, lanes]
                lo_ref[0, 2 * g + i] = lo[:, lanes]
                k_ref[0, 2 * g + i] = kk[:, lanes]
        z = proj(OFF_AI + c)
        mag = jnp.maximum(mag, col_max(z))
        for i, zz in enumerate(halves(z)):
            va_ref[0, 2 * g + i] = zz.astype(BF16)
        for i in range(2):
            lanes = slice(i * LANES, (i + 1) * LANES)
            dstat_ref[0, 0, 2 * g + i:2 * g + i + 1, :] = dmin[:, lanes]
            mstat_ref[0, 0, 2 * g + i:2 * g + i + 1, :] = mag[:, lanes]
        z = proj(OFF_AG + c)
        z = z * _sigmoid(z)
        for i, zz in enumerate(halves(z)):
            ga_ref[0, 2 * g + i] = zz.astype(BF16)
        z = proj(OFF_RV + c)
        for i, zz in enumerate(halves(z)):
            vr_ref[0, 2 * g + i] = zz.astype(BF16)
        z = proj(OFF_RG + c)
        z = z * _sigmoid(z)
        for i, zz in enumerate(halves(z)):
            gr_ref[0, 2 * g + i] = zz.astype(BF16)

    cosv = cos_ref[...]
    sinv = sin_ref[...]
    for off, ref, scale in ((OFF_RQ, qr_ref, R_DK ** -0.5), (OFF_RK, kr_ref, 1.0)):
        for g in range(R_PAIRS // 2):
            z = proj(off + 2 * LANES * g)
            for i, zz in enumerate(halves(z)):
                rot = pltpu.roll(zz, LANES // 2, 1)
                ref[0, 2 * g + i] = ((zz * cosv + rot * sinv) * scale).astype(BF16)

    for g in range(2 * D_MODEL // (2 * LANES)):
        c = 2 * LANES * g
        z = _dot(u, wmg_ref[:, c:c + 2 * LANES]) + bmg_ref[:, c:c + 2 * LANES]
        mg_ref[0, :, c:c + 2 * LANES] = _sigmoid(z).astype(BF16)


def _inproj_call(x, sc1p, shift, w_in, w_mg, b_mg, lbm, oml, cos_p, sin_p):
    B, T, D = x.shape
    rows = min(IN_ROWS, T)
    grid = (B, T // rows)
    hd = lambda n, dt: jax.ShapeDtypeStruct((B, n, T, LANES), dt)
    hspec = lambda n: pl.BlockSpec((1, n, rows, LANES), lambda b, t: (b, 0, t, 0))
    bvec = pl.BlockSpec((1, 1, D), lambda b, t: (b, 0, 0))
    stat_spec = pl.BlockSpec((1, 1, A_HEADS, LANES), lambda b, t: (b, t, 0, 0))
    stat_shape = jax.ShapeDtypeStruct((B, T // rows, A_HEADS, LANES), F32)
    return pl.pallas_call(
        _inproj_kernel,
        grid=grid,
        in_specs=[
            pl.BlockSpec((1, rows, D), lambda b, t: (b, t, 0)),
            bvec, bvec,
            _const_spec((D, IN_WIDTH)),
            _const_spec((D, 2 * D)),
            _const_spec((1, 2 * D)),
            _const_spec((1, D)),
            _const_spec((1, D)),
            pl.BlockSpec((rows, LANES), lambda b, t: (t, 0)),
            pl.BlockSpec((rows, LANES), lambda b, t: (t, 0)),
        ],
        out_specs=[hspec(A_HEADS)] * 9 + [
            hspec(R_PAIRS), hspec(R_PAIRS), hspec(R_HEADS), hspec(R_HEADS),
            pl.BlockSpec((1, rows, 2 * D), lambda b, t: (b, t, 0)), stat_spec, stat_spec],
        out_shape=[hd(A_HEADS, BF16)] * 9 + [
            hd(R_PAIRS, BF16), hd(R_PAIRS, BF16), hd(R_HEADS, BF16), hd(R_HEADS, BF16),
            jax.ShapeDtypeStruct((B, T, 2 * D), BF16), stat_shape, stat_shape],
        compiler_params=pltpu.CompilerParams(
            dimension_semantics=("arbitrary", "arbitrary"), vmem_limit_bytes=VMEM_LIMIT),
        name="inproj",
    )(x, sc1p, shift, w_in, w_mg, b_mg, lbm, oml, cos_p, sin_p)


def _hgrn_tables(reverse):
    C = HGRN_CHUNK
    idx = np.arange(C)
    cum = (idx[None, :] >= idx[:, None]) if reverse else (idx[None, :] <= idx[:, None])
    cum = cum.astype(np.float64)
    end = 0 if reverse else C - 1
    blocks = []
    for j in range(HGRN_LEVELS):
        m = C >> (j + 1)
        base = idx - idx % (2 * m)
        ref = base + (m if reverse else m - 1)
        blocks.append(cum - cum[ref])
    blocks.append(cum)
    blocks.append(cum[end][None, :] - cum)
    safe_tab = np.concatenate(blocks, axis=0)
    mid = C // 2 if reverse else C // 2 - 1
    fast_tab = np.tile(cum - cum[mid][None, :], (1, 2))
    x = idx[:, None] ^ idx[None, :]
    bl = np.zeros_like(x)
    nz = x > 0
    bl[nz] = np.floor(np.log2(x[nz])).astype(np.int64) + 1
    lvl = HGRN_LEVELS - bl
    coupled = (idx[:, None] <= idx[None, :]) if reverse else (idx[:, None] >= idx[None, :])
    lvl = np.where(coupled, lvl, -1)
    return safe_tab.astype(np.float32), fast_tab.astype(np.float32), lvl.astype(np.int32), end


def _hgrn_kernel(q_ref, hif_ref, lof_ref, kf_ref, hib_ref, lob_ref, kb_ref, v_ref, g_ref,
                 dstat_ref, mstat_ref, nw_ref,
                 tabf_ref, tabb_ref, ftabf_ref, ftabb_ref, lvf_ref, lvb_ref,
                 y_ref, o_scr, sf_scr, sb_scr, ops_scr, dec_scr, a_scr, u_scr, *, ends):
    C = HGRN_CHUNK
    T = q_ref.shape[2]
    n_chunks = T // C
    sf_scr[...] = jnp.zeros_like(sf_scr)
    sb_scr[...] = jnp.zeros_like(sb_scr)
    nw = nw_ref[...]

    def load(ci, gate_refs):
        rows = pl.ds(pl.multiple_of(ci * C, C), C)
        hi_ref, lo_ref, k_ref = gate_refs
        return (rows, q_ref[0, 0, rows, :], k_ref[0, 0, rows, :], v_ref[0, 0, rows, :],
                hi_ref[0, 0, rows, :], lo_ref[0, 0, rows, :])

    def advance(s_scr, a, v, qi, kd, s_decay):
        st = s_scr[...]
        o = _dot(a.astype(BF16), v) + _dot_nt(qi, st.astype(BF16))
        s_scr[...] = st * s_decay + _dot_tn(v, kd)
        return o


    def chunk_rows(ci):
        return pl.ds(pl.multiple_of(ci * C, C), C)

    def prepare_task(c, ci, p, slot):
        start = C - 1 - p[4]

        def issue():
            _, q, k, _, hi, lo = load(ci, p[0])
            lf_start = (hi[start:start + 1].astype(F32) + lo[start:start + 1].astype(F32))
            return q, k, lf_start, _dot(p[1][...], jnp.concatenate([hi, lo], axis=0))

        def consume(issued):
            q, k, lf_start, d = issued
            b_mid = lf_start - d[start:start + 1]
            d_end = d[p[4]:p[4] + 1]
            qh = q * jnp.exp(d).astype(BF16)
            kh = k * jnp.exp(-d).astype(BF16)
            ops_scr[slot, c, 0] = qh
            ops_scr[slot, c, 1] = kh
            ops_scr[slot, c, 2] = qh * jnp.exp(b_mid).astype(BF16)
            ops_scr[slot, c, 3] = kh * jnp.exp(d_end).astype(BF16)
            dec_scr[slot, c] = jnp.broadcast_to(jnp.exp(b_mid + d_end), (8, LANES))
        return issue, consume

    def scores_task(c, ci, p, ops_slot, slot):
        def issue():
            return (_dot_nt(ops_scr[ops_slot, c, 0], ops_scr[ops_slot, c, 1]),
                    _dot_tn(v_ref[0, 0, chunk_rows(ci), :], ops_scr[ops_slot, c, 3]))

        def consume(issued):
            a_scr[slot, c] = jnp.where(p[2][...] >= 0, issued[0], 0.0).astype(BF16)
            u_scr[slot, c] = issued[1]
        return issue, consume

    def finish_task(c, ci, p, ops_slot, slot, finish):
        rows = chunk_rows(ci)

        def issue():
            st = p[3][...]
            o = (_dot(a_scr[slot, c], v_ref[0, 0, rows, :])
                 + _dot_nt(ops_scr[ops_slot, c, 2], st.astype(BF16)))
            p[3][...] = st * dec_scr[ops_slot, c, 0:1, :] + u_scr[slot, c]
            return o

        def consume(o):
            finish(rows, o)
        return issue, consume

    def rolling(tasks, lag):
        pending = []
        for issue, consume in tasks:
            pending.append((consume, issue()))
            if len(pending) > lag:
                cons, res = pending.pop(0)
                cons(res)
        for cons, res in pending:
            cons(res)

    def safe_chunk(ci, lf_ref, tab_ref, lv_ref, s_scr, end):
        rows, qb, kb, v, hi, lo = load(ci, lf_ref)
        q = qb.astype(F32)
        k = kb.astype(F32)
        r = _dot(tab_ref[...], jnp.concatenate([hi, lo], axis=1))
        dm = r[:, :LANES] + r[:, LANES:]
        lv = lv_ref[...]
        a = jnp.where(lv == HGRN_LEVELS, _dot_nt(qb, kb), 0.0)
        for j in range(HGRN_LEVELS):
            e = jnp.exp(-jnp.abs(dm[j * C:(j + 1) * C]))
            aj = _dot_nt((q * e).astype(BF16), (k * e).astype(BF16))
            a = jnp.where(lv == j, aj, a)
        bq = dm[HGRN_LEVELS * C:(HGRN_LEVELS + 1) * C]
        bk = dm[(HGRN_LEVELS + 1) * C:(HGRN_LEVELS + 2) * C]
        qi = (q * jnp.exp(bq)).astype(BF16)
        kd = (k * jnp.exp(bk)).astype(BF16)
        return rows, advance(s_scr, a, v, qi, kd, jnp.exp(bq[end:end + 1, :]))

    def finalize(rows, o):
        tot = o_scr[rows, :] + o
        ms = jnp.mean(tot * tot, axis=-1, keepdims=True)
        y = tot * lax.rsqrt(ms + LN_EPS) * nw * g_ref[0, 0, rows, :].astype(F32)
        y_ref[0, 0, rows, :] = y.astype(BF16)

    def first_visit(rows, o):
        o_scr[rows, :] = o

    def chain_groups(tf_ref, tb_ref, per_step):
        fwd = ((hif_ref, lof_ref, kf_ref), tf_ref, lvf_ref, sf_scr, ends[0])
        bwd = ((hib_ref, lob_ref, kb_ref), tb_ref, lvb_ref, sb_scr, ends[1])

        def chains(g):
            out = []
            for u in range(per_step):
                i = g * per_step + u
                out += [(i, fwd), (n_chunks - 1 - i, bwd)]
            return out
        return chains

    def loop(lo, hi, body, finish):
        def wrapped(g, carry):
            body(g, finish)
            return carry
        lax.fori_loop(lo, hi, wrapped, 0)

    def run_safe():
        chains = chain_groups(tabf_ref, tabb_ref, 1)

        def step(g, finish):
            for ci, p in chains(g):
                finish(*safe_chunk(ci, *p))

        loop(0, n_chunks // 2, step, first_visit)
        loop(n_chunks // 2, n_chunks, step, finalize)

    def run_fast():
        per_step = ops_scr.shape[1] // 2
        chains = chain_groups(ftabf_ref, ftabb_ref, per_step)
        n_groups = n_chunks // per_step
        last = n_groups - 1

        def step(g, finish):
            g1 = jnp.minimum(g + 1, last)
            g2 = jnp.minimum(g + 2, last)
            tasks = []
            for c, ((c1, p1), (c2, p2), (c0, p0)) in enumerate(zip(chains(g1), chains(g2), chains(g))):
                tasks.append(scores_task(c, c1, p1, g1 % 3, (g + 1) % 2))
                tasks.append(prepare_task(c, c2, p2, (g + 2) % 3))
                tasks.append(finish_task(c, c0, p0, g % 3, g % 2, finish))
            rolling(tasks, HGRN_ROLL_LAG)

        for g in range(2):
            rolling([prepare_task(c, ci, p, g) for c, (ci, p) in enumerate(chains(g))], HGRN_ROLL_LAG)
        rolling([scores_task(c, ci, p, 0, 0) for c, (ci, p) in enumerate(chains(0))], HGRN_ROLL_LAG)
        loop(0, n_groups // 2, step, first_visit)
        loop(n_groups // 2, n_groups, step, finalize)

    is_head = lax.broadcasted_iota(jnp.int32, (A_HEADS, LANES), 0) == pl.program_id(1)
    dec = jnp.min(jnp.where(is_head, jnp.min(dstat_ref[0], axis=0), 0.0))
    mag = jnp.max(jnp.where(is_head, jnp.max(mstat_ref[0], axis=0), 0.0))
    is_safe = jnp.logical_and(dec > -HGRN_SAFE_DECAY, mag < HGRN_SAFE_MAG)

    pl.when(is_safe)(run_fast)
    pl.when(jnp.logical_not(is_safe))(run_safe)


def _hgrn_call(qa, hif, lof, kf, hib, lob, kb, va, ga, dstat, mstat, nw):
    B, H, T, _ = qa.shape
    stat = pl.BlockSpec((1,) + dstat.shape[1:], lambda b, h: (b, 0, 0, 0))
    tabf, ftabf, lvf, endf = _hgrn_tables(False)
    tabb, ftabb, lvb, endb = _hgrn_tables(True)
    seq = pl.BlockSpec((1, 1, T, LANES), lambda b, h: (b, h, 0, 0))
    n_chunks = T // HGRN_CHUNK
    assert n_chunks % 4 == 0, T
    per_step = max(u for u in (HGRN_FAST_UNROLL, 2, 1) if n_chunks % (4 * u) == 0)
    return pl.pallas_call(
        functools.partial(_hgrn_kernel, ends=(endf, endb)),
        grid=(B, H),
        in_specs=[seq] * 9 + [stat, stat,
                  _const_spec((1, LANES)),
                  _const_spec(tabf.shape), _const_spec(tabb.shape),
                  _const_spec(ftabf.shape), _const_spec(ftabb.shape),
                  _const_spec(lvf.shape), _const_spec(lvb.shape)],
        out_specs=seq,
        out_shape=jax.ShapeDtypeStruct((B, H, T, LANES), BF16),
        scratch_shapes=[pltpu.VMEM((T, LANES), F32),
                        pltpu.VMEM((A_DV, A_DK), F32),
                        pltpu.VMEM((A_DV, A_DK), F32),
                        pltpu.VMEM((3, 2 * per_step, 4, HGRN_CHUNK, LANES), BF16),
                        pltpu.VMEM((3, 2 * per_step, 8, LANES), F32),
                        pltpu.VMEM((2, 2 * per_step, HGRN_CHUNK, HGRN_CHUNK), BF16),
                        pltpu.VMEM((2, 2 * per_step, A_DV, A_DK), F32)],
        compiler_params=pltpu.CompilerParams(
            dimension_semantics=("arbitrary", "arbitrary"), vmem_limit_bytes=VMEM_LIMIT),
        name="hgrn",
    )(qa, hif, lof, kf, hib, lob, kb, va, ga, dstat, mstat, nw,
      jnp.asarray(tabf, BF16), jnp.asarray(tabb, BF16),
      jnp.asarray(ftabf, BF16), jnp.asarray(ftabb, BF16), jnp.asarray(lvf), jnp.asarray(lvb))


def _ret_kernel(q_ref, k_ref, v_ref, g_ref, lg_ref, y_ref,
                o_scr, s_scr, dec_scr, qd_scr, kd_scr, a_scr, u_scr):
    C = RET_CHUNK
    T = q_ref.shape[2]
    n_chunks = T // C
    per_step = a_scr.shape[1] // 4
    n_groups = n_chunks // per_step
    s_scr[...] = jnp.zeros_like(s_scr)

    lane = lax.broadcasted_iota(jnp.int32, (1, LANES), 1)
    head_mask = [(lane % (LANES // 2)) < (LANES // 4), (lane % (LANES // 2)) >= (LANES // 4)]
    tpos = lax.broadcasted_iota(jnp.int32, (C, C), 0)
    spos = lax.broadcasted_iota(jnp.int32, (C, C), 1)
    pos = lax.broadcasted_iota(jnp.int32, (C, 1), 0).astype(F32)
    s_dec = []
    for d in range(2):
        dist = ((tpos - spos) if d == 0 else (spos - tpos)).astype(F32)
        after = pos if d == 0 else (C - 1.0 - pos)
        for i in range(2):
            lg = lg_ref[0, 2 * d + i:2 * d + i + 1, :]
            dec_scr[2 * d + i] = jnp.exp(jnp.where(dist >= 0, dist * lg, NEG_BIG))
            qd_scr[2 * d + i] = jnp.where(head_mask[i], jnp.exp((after + 1.0) * lg), 0.0).astype(BF16)
            kd_scr[2 * d + i] = jnp.where(head_mask[i], jnp.exp((C - 1.0 - after) * lg), 0.0).astype(BF16)
            s_dec.append(jnp.exp(C * lg))

    def chunk_rows(ci):
        return pl.ds(pl.multiple_of(ci * C, C), C)

    def units(g):
        out = []
        for u in range(per_step):
            j = g * per_step + u
            out += [(j, 0), (n_chunks - 1 - j, 1)]
        return out

    def scores_issue(us):
        out = []
        for ci, d in us:
            rows = chunk_rows(ci)
            q = q_ref[0, 0, rows, :]
            k = k_ref[0, 0, rows, :]
            for i in range(2):
                qm = q * (qd_scr[2 * d + i] != 0).astype(BF16)
                km = k * kd_scr[2 * d + i]
                out.append((_dot_nt(qm, k), _dot_tn(km, v_ref[0, i, rows, :])))
        return out

    def scores_store(us, issued, slot):
        for n, ((ci, d), i) in enumerate((u, i) for u in us for i in range(2)):
            sc, upd = issued[n]
            a_scr[slot, n] = (sc * dec_scr[2 * d + i]).astype(BF16)
            u_scr[slot, n] = upd

    def finish_issue(us, slot):
        outs = []
        for n, ((ci, d), i) in enumerate((u, i) for u in us for i in range(2)):
            rows = chunk_rows(ci)
            st = s_scr[2 * d + i]
            qd = q_ref[0, 0, rows, :] * qd_scr[2 * d + i]
            lhs = jnp.concatenate([a_scr[slot, n], qd], axis=1)
            rhs = jnp.concatenate([v_ref[0, i, rows, :], st.astype(BF16)], axis=0)
            outs.append((i, rows, _dot(lhs, rhs)))
            s_scr[2 * d + i] = st * s_dec[2 * d + i] + u_scr[slot, n]
        return outs

    def first_visit(i, rows, o):
        o_scr[i, rows, :] = o

    def finalize(i, rows, o):
        tot = o_scr[i, rows, :] + o
        mu = jnp.mean(tot, axis=-1, keepdims=True)
        cen = tot - mu
        var = jnp.mean(cen * cen, axis=-1, keepdims=True)
        y = cen * lax.rsqrt(var + LN_EPS) * g_ref[0, i, rows, :].astype(F32)
        y_ref[0, i, rows, :] = y.astype(BF16)

    def step(g, finish):
        nxt = units(jnp.minimum(g + 1, n_groups - 1))
        sc = scores_issue(nxt)
        outs = finish_issue(units(g), g % 2)
        scores_store(nxt, sc, (g + 1) % 2)
        for i, rows, o in outs:
            finish(i, rows, o)

    def loop(lo, hi, finish):
        def wrapped(g, carry):
            step(g, finish)
            return carry
        lax.fori_loop(lo, hi, wrapped, 0)

    scores_store(units(0), scores_issue(units(0)), 0)
    loop(0, n_groups // 2, first_visit)
    loop(n_groups // 2, n_groups, finalize)


def _ret_call(qr, kr, vr, gr, lg_rows):
    B, P, T, _ = qr.shape
    n_chunks = T // RET_CHUNK
    assert n_chunks % 2 == 0, T
    per_step = max(u for u in (RET_STEP_CHUNKS, 1) if n_chunks % (2 * u) == 0)
    n_units = 4 * per_step
    seq1 = pl.BlockSpec((1, 1, T, LANES), lambda b, p: (b, p, 0, 0))
    seq2 = pl.BlockSpec((1, 2, T, LANES), lambda b, p: (b, p, 0, 0))
    return pl.pallas_call(
        _ret_kernel,
        grid=(B, P),
        in_specs=[seq1, seq1, seq2, seq2,
                  pl.BlockSpec((1, 4, LANES), lambda b, p: (p, 0, 0))],
        out_specs=seq2,
        out_shape=jax.ShapeDtypeStruct((B, R_HEADS, T, LANES), BF16),
        scratch_shapes=[pltpu.VMEM((2, T, LANES), F32),
                        pltpu.VMEM((4, LANES, R_DV), F32),
                        pltpu.VMEM((4, RET_CHUNK, RET_CHUNK), F32),
                        pltpu.VMEM((4, RET_CHUNK, LANES), BF16),
                        pltpu.VMEM((4, RET_CHUNK, LANES), BF16),
                        pltpu.VMEM((2, n_units, RET_CHUNK, RET_CHUNK), BF16),
                        pltpu.VMEM((2, n_units, LANES, R_DV), F32)],
        compiler_params=pltpu.CompilerParams(
            dimension_semantics=("arbitrary", "arbitrary"), vmem_limit_bytes=VMEM_LIMIT),
        name="retention",
    )(qr, kr, vr, gr, lg_rows)


def _outproj_kernel(ya_ref, yr_ref, mg_ref, x_ref, gate_ref, wpa_ref, wpb_ref, wout_ref,
                    lng_ref, lnb_ref, o_ref):
    ya = jnp.concatenate([ya_ref[0, h] for h in range(A_HEADS)], axis=1)
    yr = jnp.concatenate([yr_ref[0, h] for h in range(R_HEADS)], axis=1)
    pa = _dot(ya, wpa_ref[...])
    pr = _dot(yr, wpb_ref[...])
    mg = mg_ref[0]
    m = mg[:, :D_MODEL].astype(F32) * pa + mg[:, D_MODEL:].astype(F32) * pr
    s = _dot(m.astype(BF16), wout_ref[...])
    h = DEEPNORM_ALPHA * x_ref[0] + gate_ref[0] * s
    mu = jnp.mean(h, axis=-1, keepdims=True)
    cen = h - mu
    var = jnp.mean(cen * cen, axis=-1, keepdims=True)
    o_ref[0] = cen * lax.rsqrt(var + LN_EPS) * lng_ref[...] + lnb_ref[...]


def _outproj_call(ya, yr, mg, x, gate1p, w_pa, w_pb, w_out, ln_g, ln_b):
    B, T, D = x.shape
    rows = min(OUT_ROWS, T)
    hspec = pl.BlockSpec((1, A_HEADS, rows, LANES), lambda b, t: (b, 0, t, 0))
    return pl.pallas_call(
        _outproj_kernel,
        grid=(B, T // rows),
        in_specs=[hspec, hspec,
                  pl.BlockSpec((1, rows, 2 * D), lambda b, t: (b, t, 0)),
                  pl.BlockSpec((1, rows, D), lambda b, t: (b, t, 0)),
                  pl.BlockSpec((1, 1, D), lambda b, t: (b, 0, 0)),
                  _const_spec((D, D)), _const_spec((D, D)), _const_spec((D, D)),
                  _const_spec((1, D)), _const_spec((1, D))],
        out_specs=pl.BlockSpec((1, rows, D), lambda b, t: (b, t, 0)),
        out_shape=jax.ShapeDtypeStruct((B, T, D), F32),
        compiler_params=pltpu.CompilerParams(
            dimension_semantics=("arbitrary", "arbitrary"), vmem_limit_bytes=VMEM_LIMIT),
        name="outproj",
    )(ya, yr, mg, x, gate1p, w_pa, w_pb, w_out, ln_g, ln_b)


def _rope_perm():
    half = R_DK // 2
    perm = []
    for p in range(R_PAIRS):
        for part in range(2):
            for i in range(2):
                h = 2 * p + i
                perm.extend(h * R_DK + part * half + np.arange(half))
    return np.asarray(perm, np.int32)


def _rope_tables(T):
    half = R_DK // 2
    pos = jnp.arange(T, dtype=F32)
    inv = ROPE_BASE ** (-jnp.arange(0, R_DK, 2, dtype=F32) / R_DK)
    ang = pos[:, None] * inv[None, :]
    cos, sin = jnp.cos(ang), jnp.sin(ang)
    return jnp.tile(cos, (1, LANES // half)), jnp.concatenate([-sin, -sin, sin, sin], axis=1)


def _trunk(x, ada, params):
    B, T, D = x.shape
    cos_p, sin_p = _rope_tables(T)
    for l in range(DEPTH):
        p = params[l]
        shift = ada[l, :, :D].reshape(B, 1, D)
        sc1p = (1.0 + ada[l, :, D:2 * D]).reshape(B, 1, D)
        gate1p = (1.0 + ada[l, :, 2 * D:]).reshape(B, 1, D)
        qa, hif, lof, kf, hib, lob, kb, va, ga, qr, kr, vr, gr, mg, dstat, mstat = _inproj_call(
            x, sc1p, shift, p["w_in"], p["w_mg"], p["b_mg"], p["lbm"], p["oml"], cos_p, sin_p)
        ya = _hgrn_call(qa, hif, lof, kf, hib, lob, kb, va, ga, dstat, mstat, p["nw"])
        yr = _ret_call(qr, kr, vr, gr, p["lg_rows"])
        x = _outproj_call(ya, yr, mg, x, gate1p, p["w_pa"], p["w_pb"], p["w_out"],
                          p["ln_g"], p["ln_b"])
    return x


def kernel(x_prompt, x_sample, c_prompt, c_sample, w_ada, b_ada, w_in, hgrn_lb, a_norm_w, ret_decay,
           w_pa, w_pb, w_mg, b_mg, w_out, ln_g, ln_b):
    D = D_MODEL
    pr = jax.nn.softmax(hgrn_lb.astype(F32), axis=0)
    lbs = jnp.cumsum(pr, axis=0) - pr[0:1]
    perm = _rope_perm()
    params = []
    for l in range(DEPTH):
        lb = lbs[l]
        lbm = jnp.maximum(lb, LOG_FLOOR)
        wl = w_in[l]
        wl = jnp.concatenate([wl[:, :OFF_RQ], wl[:, OFF_RQ + perm], wl[:, OFF_RK + perm],
                              wl[:, OFF_RV:]], axis=1)
        lg = jax.nn.log_sigmoid(ret_decay[l].astype(F32))
        lg_rows = jnp.broadcast_to(
            lg.reshape(2, R_PAIRS, 2).transpose(1, 0, 2).reshape(R_PAIRS, 4, 1), (R_PAIRS, 4, LANES))
        params.append(dict(
            w_in=wl.astype(BF16), w_mg=w_mg[l].astype(BF16), b_mg=b_mg[l].reshape(1, 2 * D),
            lbm=lbm.reshape(1, D), oml=(1.0 - lb).reshape(1, D),
            nw=a_norm_w[l].reshape(1, A_DV), lg_rows=lg_rows,
            w_pa=w_pa[l].astype(BF16), w_pb=w_pb[l].astype(BF16), w_out=w_out[l].astype(BF16),
            ln_g=ln_g[l].reshape(1, D), ln_b=ln_b[l].reshape(1, D)))
    nbp = x_prompt.shape[0]
    ada = _ada_call(jnp.concatenate([c_prompt, c_sample], axis=0), w_ada, b_ada)
    y_prompt = _trunk(x_prompt, ada[:, :nbp], params)
    y_sample = _trunk(x_sample, ada[:, nbp:], params)
    return (y_prompt, y_sample)
```

```python
import functools

import numpy as np
import jax
import jax.numpy as jnp
from jax import lax
from jax.experimental import pallas as pl
from jax.experimental.pallas import tpu as pltpu

F32 = jnp.float32
BF16 = jnp.bfloat16

D_MODEL = 1024
DEPTH = 2
A_HEADS = 8
A_DK = 128
A_DV = 128
R_HEADS = 8
R_DK = 64
R_DV = 128
R_PAIRS = R_HEADS // 2
LN_EPS = 1e-5
LOG_FLOOR = 1e-30
NEG_BIG = -1e30
DEEPNORM_ALPHA = (2.0 * DEPTH) ** 0.25
ROPE_BASE = 10000.0

OFF_AQ, OFF_FF, OFF_FB, OFF_AI, OFF_AG, OFF_RQ, OFF_RK, OFF_RV, OFF_RG = (
    0, 1024, 2048, 3072, 4096, 5120, 5632, 6144, 7168)
IN_WIDTH = 8192

LANES = 128
HGRN_CHUNK = 64
HGRN_LEVELS = 6
RET_CHUNK = 128
HGRN_SAFE_DECAY = 75.0
HGRN_SAFE_MAG = 1e5
HGRN_FAST_UNROLL = 8
IN_ROLL_LAG = 1
HGRN_ROLL_LAG = 12
RET_STEP_CHUNKS = 4
RET_ROLL_LAG = 4
IN_ROWS = 256
OUT_ROWS = 512
VMEM_LIMIT = 56 * 1024 * 1024


def _dot(a, b):
    return jnp.dot(a, b, preferred_element_type=F32)


def _dot_nt(a, b):
    return lax.dot_general(a, b, (((1,), (1,)), ((), ())), preferred_element_type=F32)


def _dot_tn(a, b):
    return lax.dot_general(a, b, (((0,), (0,)), ((), ())), preferred_element_type=F32)


def _sigmoid(z):
    return 1.0 / (1.0 + jnp.exp(-z))


def _const_spec(shape):
    nd = len(shape)
    return pl.BlockSpec(shape, lambda *_: (0,) * nd, pipeline_mode=pl.Buffered(1))


def _ada_kernel(c_ref, w_ref, b_ref, o_ref):
    c = c_ref[...]
    a = c * _sigmoid(c)
    o_ref[0] = jnp.dot(a, w_ref[0], preferred_element_type=F32,
                       precision=lax.Precision.HIGHEST) + b_ref[0]


def _ada_call(c, w_ada, b_ada):
    nb = c.shape[0]
    d = D_MODEL
    return pl.pallas_call(
        _ada_kernel,
        grid=(DEPTH, 3),
        in_specs=[
            pl.BlockSpec((nb, d), lambda l, j: (0, 0)),
            pl.BlockSpec((1, d, d), lambda l, j: (l, 0, j)),
            pl.BlockSpec((1, 1, d), lambda l, j: (l, 0, j)),
        ],
        out_specs=pl.BlockSpec((1, nb, d), lambda l, j: (l, 0, j)),
        out_shape=jax.ShapeDtypeStruct((DEPTH, nb, 3 * d), F32),
        compiler_params=pltpu.CompilerParams(
            dimension_semantics=("arbitrary", "arbitrary"), vmem_limit_bytes=VMEM_LIMIT),
        name="adaln",
    )(c, w_ada, b_ada.reshape(DEPTH, 1, 3 * d))


def _inproj_kernel(x_ref, sc_ref, sh_ref, win_ref, wmg_ref, bmg_ref, lbm_ref, oml_ref,
                   cos_ref, sin_ref,
                   qa_ref, hif_ref, lof_ref, kf_ref, hib_ref, lob_ref, kb_ref, va_ref, ga_ref,
                   qr_ref, kr_ref, vr_ref, gr_ref, mg_ref, dstat_ref, mstat_ref):
    rows = x_ref.shape[1]
    u = (x_ref[0] * sc_ref[0] + sh_ref[0]).astype(BF16)

    def col_max(z):
        return jnp.max(jnp.abs(z), axis=0, keepdims=True)

    def store_heads(ref, g, z):
        for i in range(2):
            ref[0, 2 * g + i] = z[:, i * LANES:(i + 1) * LANES].astype(BF16)

    tasks = []
    stats = {}

    def q_epilogue(g, z):
        z = z * _sigmoid(z) * (A_DK ** -0.5)
        stats[g] = [col_max(z), None]
        store_heads(qa_ref, g, z)

    def forget_epilogue(g, hi_ref, lo_ref, k_ref, z):
        lbm = lbm_ref[:, 2 * LANES * g:2 * LANES * (g + 1)]
        oml = oml_ref[:, 2 * LANES * g:2 * LANES * (g + 1)]
        f = lbm + oml * _sigmoid(z)
        lf = jnp.log(f)
        hi = lf.astype(BF16)
        halfsum = jnp.sum(lf.reshape(2 * rows // HGRN_CHUNK, HGRN_CHUNK // 2, 2 * LANES), axis=1)
        halfmin = jnp.min(halfsum, axis=0, keepdims=True)
        stats[g][1] = halfmin if stats[g][1] is None else jnp.minimum(stats[g][1], halfmin)
        store_heads(hi_ref, g, hi)
        store_heads(lo_ref, g, lf - hi.astype(F32))
        store_heads(k_ref, g, (oml + lbm) - f)

    def v_epilogue(g, z):
        mag, dmin = stats.pop(g)
        mag = jnp.maximum(mag, col_max(z))
        store_heads(va_ref, g, z)
        for i in range(2):
            lanes = slice(i * LANES, (i + 1) * LANES)
            dstat_ref[0, 0, 2 * g + i:2 * g + i + 1, :] = dmin[:, lanes]
            mstat_ref[0, 0, 2 * g + i:2 * g + i + 1, :] = mag[:, lanes]

    def silu_epilogue(ref, g, z):
        store_heads(ref, g, z * _sigmoid(z))

    def rope_epilogue(ref, g, scale, z):
        for i in range(2):
            zz = z[:, i * LANES:(i + 1) * LANES]
            rot = pltpu.roll(zz, LANES // 2, 1)
            ref[0, 2 * g + i] = ((zz * cos_ref[...] + rot * sin_ref[...]) * scale).astype(BF16)

    def merge_epilogue(g, z):
        c = 2 * LANES * g
        mg_ref[0, :, c:c + 2 * LANES] = _sigmoid(z + bmg_ref[:, c:c + 2 * LANES]).astype(BF16)

    part = functools.partial
    for g in range(A_HEADS // 2):
        c = 2 * LANES * g
        tasks += [(win_ref, OFF_AQ + c, part(q_epilogue, g)),
                  (win_ref, OFF_FF + c, part(forget_epilogue, g, hif_ref, lof_ref, kf_ref)),
                  (win_ref, OFF_RV + c, part(store_heads, vr_ref, g)),
                  (win_ref, OFF_FB + c, part(forget_epilogue, g, hib_ref, lob_ref, kb_ref)),
                  (win_ref, OFF_AI + c, part(v_epilogue, g)),
                  (win_ref, OFF_AG + c, part(silu_epilogue, ga_ref, g)),
                  (win_ref, OFF_RG + c, part(silu_epilogue, gr_ref, g))]
    for g in range(R_PAIRS // 2):
        c = 2 * LANES * g
        tasks += [(win_ref, OFF_RQ + c, part(rope_epilogue, qr_ref, g, R_DK ** -0.5)),
                  (win_ref, OFF_RK + c, part(rope_epilogue, kr_ref, g, 1.0))]
    for g in range(2 * D_MODEL // (2 * LANES)):
        tasks.append((wmg_ref, 2 * LANES * g, part(merge_epilogue, g)))

    pending = []
    for w_ref, c0, epilogue in tasks:
        pending.append((epilogue, _dot(u, w_ref[:, c0:c0 + 2 * LANES])))
        if len(pending) > IN_ROLL_LAG:
            ep, z = pending.pop(0)
            ep(z)
    for ep, z in pending:
        ep(z)


def _inproj_call(x, sc1p, shift, w_in, w_mg, b_mg, lbm, oml, cos_p, sin_p):
    B, T, D = x.shape
    rows = min(IN_ROWS, T)
    grid = (B, T // rows)
    hd = lambda n, dt: jax.ShapeDtypeStruct((B, n, T, LANES), dt)
    hspec = lambda n: pl.BlockSpec((1, n, rows, LANES), lambda b, t: (b, 0, t, 0))
    bvec = pl.BlockSpec((1, 1, D), lambda b, t: (b, 0, 0))
    stat_spec = pl.BlockSpec((1, 1, A_HEADS, LANES), lambda b, t: (b, t, 0, 0))
    stat_shape = jax.ShapeDtypeStruct((B, T // rows, A_HEADS, LANES), F32)
    return pl.pallas_call(
        _inproj_kernel,
        grid=grid,
        in_specs=[
            pl.BlockSpec((1, rows, D), lambda b, t: (b, t, 0)),
            bvec, bvec,
            _const_spec((D, IN_WIDTH)),
            _const_spec((D, 2 * D)),
            _const_spec((1, 2 * D)),
            _const_spec((1, D)),
            _const_spec((1, D)),
            pl.BlockSpec((rows, LANES), lambda b, t: (t, 0)),
            pl.BlockSpec((rows, LANES), lambda b, t: (t, 0)),
        ],
        out_specs=[hspec(A_HEADS)] * 9 + [
            hspec(R_PAIRS), hspec(R_PAIRS), hspec(R_HEADS), hspec(R_HEADS),
            pl.BlockSpec((1, rows, 2 * D), lambda b, t: (b, t, 0)), stat_spec, stat_spec],
        out_shape=[hd(A_HEADS, BF16)] * 9 + [
            hd(R_PAIRS, BF16), hd(R_PAIRS, BF16), hd(R_HEADS, BF16), hd(R_HEADS, BF16),
            jax.ShapeDtypeStruct((B, T, 2 * D), BF16), stat_shape, stat_shape],
        compiler_params=pltpu.CompilerParams(
            dimension_semantics=("arbitrary", "arbitrary"), vmem_limit_bytes=VMEM_LIMIT),
        name="inproj",
    )(x, sc1p, shift, w_in, w_mg, b_mg, lbm, oml, cos_p, sin_p)


def _hgrn_tables(reverse):
    C = HGRN_CHUNK
    idx = np.arange(C)
    cum = (idx[None, :] >= idx[:, None]) if reverse else (idx[None, :] <= idx[:, None])
    cum = cum.astype(np.float64)
    end = 0 if reverse else C - 1
    blocks = []
    for j in range(HGRN_LEVELS):
        m = C >> (j + 1)
        base = idx - idx % (2 * m)
        ref = base + (m if reverse else m - 1)
        blocks.append(cum - cum[ref])
    blocks.append(cum)
    blocks.append(cum[end][None, :] - cum)
    safe_tab = np.concatenate(blocks, axis=0)
    mid = C // 2 if reverse else C // 2 - 1
    fast_tab = np.tile(cum - cum[mid][None, :], (1, 2))
    x = idx[:, None] ^ idx[None, :]
    bl = np.zeros_like(x)
    nz = x > 0
    bl[nz] = np.floor(np.log2(x[nz])).astype(np.int64) + 1
    lvl = HGRN_LEVELS - bl
    coupled = (idx[:, None] <= idx[None, :]) if reverse else (idx[:, None] >= idx[None, :])
    lvl = np.where(coupled, lvl, -1)
    return safe_tab.astype(np.float32), fast_tab.astype(np.float32), lvl.astype(np.int32), end


def _hgrn_kernel(q_ref, hif_ref, lof_ref, kf_ref, hib_ref, lob_ref, kb_ref, v_ref, g_ref,
                 dstat_ref, mstat_ref, nw_ref,
                 tabf_ref, tabb_ref, ftabf_ref, ftabb_ref, lvf_ref, lvb_ref,
                 y_ref, o_scr, sf_scr, sb_scr, ops_scr, dec_scr, a_scr, u_scr, *, ends):
    C = HGRN_CHUNK
    T = q_ref.shape[2]
    n_chunks = T // C
    sf_scr[...] = jnp.zeros_like(sf_scr)
    sb_scr[...] = jnp.zeros_like(sb_scr)
    nw = nw_ref[...]

    def load(ci, gate_refs):
        rows = pl.ds(pl.multiple_of(ci * C, C), C)
        hi_ref, lo_ref, k_ref = gate_refs
        return (rows, q_ref[0, 0, rows, :], k_ref[0, 0, rows, :], v_ref[0, 0, rows, :],
                hi_ref[0, 0, rows, :], lo_ref[0, 0, rows, :])

    def advance(s_scr, a, v, qi, kd, s_decay):
        st = s_scr[...]
        o = _dot(a.astype(BF16), v) + _dot_nt(qi, st.astype(BF16))
        s_scr[...] = st * s_decay + _dot_tn(v, kd)
        return o


    def chunk_rows(ci):
        return pl.ds(pl.multiple_of(ci * C, C), C)

    def prepare_task(c, ci, p, slot):
        start = C - 1 - p[4]

        def issue():
            _, q, k, _, hi, lo = load(ci, p[0])
            lf_start = (hi[start:start + 1].astype(F32) + lo[start:start + 1].astype(F32))
            return q, k, lf_start, _dot(p[1][...], jnp.concatenate([hi, lo], axis=0))

        def consume(issued):
            q, k, lf_start, d = issued
            b_mid = lf_start - d[start:start + 1]
            d_end = d[p[4]:p[4] + 1]
            qh = q * jnp.exp(d).astype(BF16)
            kh = k * jnp.exp(-d).astype(BF16)
            ops_scr[slot, c, 0] = qh
            ops_scr[slot, c, 1] = kh
            ops_scr[slot, c, 2] = qh * jnp.exp(b_mid).astype(BF16)
            ops_scr[slot, c, 3] = kh * jnp.exp(d_end).astype(BF16)
            dec_scr[slot, c] = jnp.broadcast_to(jnp.exp(b_mid + d_end), (8, LANES))
        return issue, consume

    def scores_task(c, ci, p, ops_slot, slot):
        def issue():
            return (_dot_nt(ops_scr[ops_slot, c, 0], ops_scr[ops_slot, c, 1]),
                    _dot_tn(v_ref[0, 0, chunk_rows(ci), :], ops_scr[ops_slot, c, 3]))

        def consume(issued):
            a_scr[slot, c] = jnp.where(p[2][...] >= 0, issued[0], 0.0).astype(BF16)
            u_scr[slot, c] = issued[1]
        return issue, consume

    def finish_task(c, ci, p, ops_slot, slot, finish):
        rows = chunk_rows(ci)

        def issue():
            st = p[3][...]
            o = (_dot(a_scr[slot, c], v_ref[0, 0, rows, :])
                 + _dot_nt(ops_scr[ops_slot, c, 2], st.astype(BF16)))
            p[3][...] = st * dec_scr[ops_slot, c, 0:1, :] + u_scr[slot, c]
            return o

        def consume(o):
            finish(rows, o)
        return issue, consume

    def rolling(tasks, lag):
        pending = []
        for issue, consume in tasks:
            pending.append((consume, issue()))
            if len(pending) > lag:
                cons, res = pending.pop(0)
                cons(res)
        for cons, res in pending:
            cons(res)

    def safe_chunk(ci, lf_ref, tab_ref, lv_ref, s_scr, end):
        rows, qb, kb, v, hi, lo = load(ci, lf_ref)
        q = qb.astype(F32)
        k = kb.astype(F32)
        r = _dot(tab_ref[...], jnp.concatenate([hi, lo], axis=1))
        dm = r[:, :LANES] + r[:, LANES:]
        lv = lv_ref[...]
        a = jnp.where(lv == HGRN_LEVELS, _dot_nt(qb, kb), 0.0)
        for j in range(HGRN_LEVELS):
            e = jnp.exp(-jnp.abs(dm[j * C:(j + 1) * C]))
            aj = _dot_nt((q * e).astype(BF16), (k * e).astype(BF16))
            a = jnp.where(lv == j, aj, a)
        bq = dm[HGRN_LEVELS * C:(HGRN_LEVELS + 1) * C]
        bk = dm[(HGRN_LEVELS + 1) * C:(HGRN_LEVELS + 2) * C]
        qi = (q * jnp.exp(bq)).astype(BF16)
        kd = (k * jnp.exp(bk)).astype(BF16)
        return rows, advance(s_scr, a, v, qi, kd, jnp.exp(bq[end:end + 1, :]))

    def finalize(rows, o):
        tot = o_scr[rows, :] + o
        ms = jnp.mean(tot * tot, axis=-1, keepdims=True)
        y = tot * lax.rsqrt(ms + LN_EPS) * nw * g_ref[0, 0, rows, :].astype(F32)
        y_ref[0, 0, rows, :] = y.astype(BF16)

    def first_visit(rows, o):
        o_scr[rows, :] = o

    def chain_groups(tf_ref, tb_ref, per_step):
        fwd = ((hif_ref, lof_ref, kf_ref), tf_ref, lvf_ref, sf_scr, ends[0])
        bwd = ((hib_ref, lob_ref, kb_ref), tb_ref, lvb_ref, sb_scr, ends[1])

        def chains(g):
            out = []
            for u in range(per_step):
                i = g * per_step + u
                out += [(i, fwd), (n_chunks - 1 - i, bwd)]
            return out
        return chains

    def loop(lo, hi, body, finish):
        def wrapped(g, carry):
            body(g, finish)
            return carry
        lax.fori_loop(lo, hi, wrapped, 0)

    def run_safe():
        chains = chain_groups(tabf_ref, tabb_ref, 1)

        def step(g, finish):
            for ci, p in chains(g):
                finish(*safe_chunk(ci, *p))

        loop(0, n_chunks // 2, step, first_visit)
        loop(n_chunks // 2, n_chunks, step, finalize)

    def run_fast():
        per_step = ops_scr.shape[1] // 2
        chains = chain_groups(ftabf_ref, ftabb_ref, per_step)
        n_groups = n_chunks // per_step
        last = n_groups - 1

        def step(g, finish):
            g1 = jnp.minimum(g + 1, last)
            g2 = jnp.minimum(g + 2, last)
            tasks = []
            for c, ((c1, p1), (c2, p2), (c0, p0)) in enumerate(zip(chains(g1), chains(g2), chains(g))):
                tasks.append(scores_task(c, c1, p1, g1 % 3, (g + 1) % 2))
                tasks.append(prepare_task(c, c2, p2, (g + 2) % 3))
                tasks.append(finish_task(c, c0, p0, g % 3, g % 2, finish))
            rolling(tasks, HGRN_ROLL_LAG)

        for g in range(2):
            rolling([prepare_task(c, ci, p, g) for c, (ci, p) in enumerate(chains(g))], HGRN_ROLL_LAG)
        rolling([scores_task(c, ci, p, 0, 0) for c, (ci, p) in enumerate(chains(0))], HGRN_ROLL_LAG)
        loop(0, n_groups // 2, step, first_visit)
        loop(n_groups // 2, n_groups, step, finalize)

    is_head = lax.broadcasted_iota(jnp.int32, (A_HEADS, LANES), 0) == pl.program_id(1)
    dec = jnp.min(jnp.where(is_head, jnp.min(dstat_ref[0], axis=0), 0.0))
    mag = jnp.max(jnp.where(is_head, jnp.max(mstat_ref[0], axis=0), 0.0))
    is_safe = jnp.logical_and(dec > -HGRN_SAFE_DECAY, mag < HGRN_SAFE_MAG)

    pl.when(is_safe)(run_fast)
    pl.when(jnp.logical_not(is_safe))(run_safe)


def _hgrn_call(qa, hif, lof, kf, hib, lob, kb, va, ga, dstat, mstat, nw):
    B, H, T, _ = qa.shape
    stat = pl.BlockSpec((1,) + dstat.shape[1:], lambda b, h: (b, 0, 0, 0))
    tabf, ftabf, lvf, endf = _hgrn_tables(False)
    tabb, ftabb, lvb, endb = _hgrn_tables(True)
    seq = pl.BlockSpec((1, 1, T, LANES), lambda b, h: (b, h, 0, 0))
    n_chunks = T // HGRN_CHUNK
    assert n_chunks % 4 == 0, T
    per_step = max(u for u in (HGRN_FAST_UNROLL, 2, 1) if n_chunks % (4 * u) == 0)
    return pl.pallas_call(
        functools.partial(_hgrn_kernel, ends=(endf, endb)),
        grid=(B, H),
        in_specs=[seq] * 9 + [stat, stat,
                  _const_spec((1, LANES)),
                  _const_spec(tabf.shape), _const_spec(tabb.shape),
                  _const_spec(ftabf.shape), _const_spec(ftabb.shape),
                  _const_spec(lvf.shape), _const_spec(lvb.shape)],
        out_specs=seq,
        out_shape=jax.ShapeDtypeStruct((B, H, T, LANES), BF16),
        scratch_shapes=[pltpu.VMEM((T, LANES), F32),
                        pltpu.VMEM((A_DV, A_DK), F32),
                        pltpu.VMEM((A_DV, A_DK), F32),
                        pltpu.VMEM((3, 2 * per_step, 4, HGRN_CHUNK, LANES), BF16),
                        pltpu.VMEM((3, 2 * per_step, 8, LANES), F32),
                        pltpu.VMEM((2, 2 * per_step, HGRN_CHUNK, HGRN_CHUNK), BF16),
                        pltpu.VMEM((2, 2 * per_step, A_DV, A_DK), F32)],
        compiler_params=pltpu.CompilerParams(
            dimension_semantics=("arbitrary", "arbitrary"), vmem_limit_bytes=VMEM_LIMIT),
        name="hgrn",
    )(qa, hif, lof, kf, hib, lob, kb, va, ga, dstat, mstat, nw,
      jnp.asarray(tabf, BF16), jnp.asarray(tabb, BF16),
      jnp.asarray(ftabf, BF16), jnp.asarray(ftabb, BF16), jnp.asarray(lvf), jnp.asarray(lvb))


def _ret_kernel(q_ref, k_ref, v_ref, g_ref, lg_ref, y_ref,
                o_scr, s_scr, dec_scr, qd_scr, kd_scr, hm_scr, a_scr, u_scr):
    C = RET_CHUNK
    T = q_ref.shape[2]
    n_chunks = T // C
    per_step = a_scr.shape[1] // 4
    n_groups = n_chunks // per_step
    s_scr[...] = jnp.zeros_like(s_scr)

    lane = lax.broadcasted_iota(jnp.int32, (1, LANES), 1)
    head_mask = [(lane % (LANES // 2)) < (LANES // 4), (lane % (LANES // 2)) >= (LANES // 4)]
    tpos = lax.broadcasted_iota(jnp.int32, (C, C), 0)
    spos = lax.broadcasted_iota(jnp.int32, (C, C), 1)
    pos = lax.broadcasted_iota(jnp.int32, (C, 1), 0).astype(F32)
    s_dec = []
    for i in range(2):
        hm_scr[i] = jnp.broadcast_to(jnp.where(head_mask[i], 1.0, 0.0), (C, LANES)).astype(BF16)
    for d in range(2):
        dist = ((tpos - spos) if d == 0 else (spos - tpos)).astype(F32)
        after = pos if d == 0 else (C - 1.0 - pos)
        for i in range(2):
            lg = lg_ref[0, 2 * d + i:2 * d + i + 1, :]
            dec_scr[2 * d + i] = jnp.exp(jnp.where(dist >= 0, dist * lg, NEG_BIG))
            qd_scr[2 * d + i] = jnp.where(head_mask[i], jnp.exp((after + 1.0) * lg), 0.0).astype(BF16)
            kd_scr[2 * d + i] = jnp.where(head_mask[i], jnp.exp((C - 1.0 - after) * lg), 0.0).astype(BF16)
            s_dec.append(jnp.exp(C * lg))

    def chunk_rows(ci):
        return pl.ds(pl.multiple_of(ci * C, C), C)

    def units(g):
        out = []
        for u in range(per_step):
            j = g * per_step + u
            out += [(j, 0), (n_chunks - 1 - j, 1)]
        return out

    def head_units(us):
        return list(enumerate((ci, d, i) for ci, d in us for i in range(2)))

    def scores_task(n, ci, d, i, slot):
        rows = chunk_rows(ci)

        def issue():
            q = q_ref[0, 0, rows, :]
            k = k_ref[0, 0, rows, :]
            qm = q * hm_scr[i]
            km = k * kd_scr[2 * d + i]
            return _dot_nt(qm, k), _dot_tn(km, v_ref[0, i, rows, :])

        def consume(issued):
            a_scr[slot, n] = (issued[0] * dec_scr[2 * d + i]).astype(BF16)
            u_scr[slot, n] = issued[1]
        return issue, consume

    def finish_task(n, ci, d, i, slot, finish):
        rows = chunk_rows(ci)

        def issue():
            st = s_scr[2 * d + i]
            qd = q_ref[0, 0, rows, :] * qd_scr[2 * d + i]
            lhs = jnp.concatenate([a_scr[slot, n], qd], axis=1)
            rhs = jnp.concatenate([v_ref[0, i, rows, :], st.astype(BF16)], axis=0)
            s_scr[2 * d + i] = st * s_dec[2 * d + i] + u_scr[slot, n]
            return _dot(lhs, rhs)

        def consume(o):
            finish(i, rows, o)
        return issue, consume

    def rolling(tasks, lag):
        pending = []
        for issue, consume in tasks:
            pending.append((consume, issue()))
            if len(pending) > lag:
                cons, res = pending.pop(0)
                cons(res)
        for cons, res in pending:
            cons(res)

    def first_visit(i, rows, o):
        o_scr[i, rows, :] = o

    def finalize(i, rows, o):
        tot = o_scr[i, rows, :] + o
        mu = jnp.mean(tot, axis=-1, keepdims=True)
        var = jnp.mean(tot * tot, axis=-1, keepdims=True) - mu * mu
        y = (tot - mu) * lax.rsqrt(var + LN_EPS) * g_ref[0, i, rows, :].astype(F32)
        y_ref[0, i, rows, :] = y.astype(BF16)

    def step(g, finish):
        nxt = head_units(units(jnp.minimum(g + 1, n_groups - 1)))
        cur = head_units(units(g))
        tasks = []
        for (n, (c1, d1, i1)), (_, (c0, d0, i0)) in zip(nxt, cur):
            tasks.append(scores_task(n, c1, d1, i1, (g + 1) % 2))
            tasks.append(finish_task(n, c0, d0, i0, g % 2, finish))
        rolling(tasks, RET_ROLL_LAG)

    def loop(lo, hi, finish):
        def wrapped(g, carry):
            step(g, finish)
            return carry
        lax.fori_loop(lo, hi, wrapped, 0)

    rolling([scores_task(n, ci, d, i, 0) for n, (ci, d, i) in head_units(units(0))], RET_ROLL_LAG)
    loop(0, n_groups // 2, first_visit)
    loop(n_groups // 2, n_groups, finalize)


def _ret_call(qr, kr, vr, gr, lg_rows):
    B, P, T, _ = qr.shape
    n_chunks = T // RET_CHUNK
    assert n_chunks % 2 == 0, T
    per_step = max(u for u in (RET_STEP_CHUNKS, 1) if n_chunks % (2 * u) == 0)
    n_units = 4 * per_step
    seq1 = pl.BlockSpec((1, 1, T, LANES), lambda b, p: (b, p, 0, 0))
    seq2 = pl.BlockSpec((1, 2, T, LANES), lambda b, p: (b, p, 0, 0))
    return pl.pallas_call(
        _ret_kernel,
        grid=(B, P),
        in_specs=[seq1, seq1, seq2, seq2,
                  pl.BlockSpec((1, 4, LANES), lambda b, p: (p, 0, 0))],
        out_specs=seq2,
        out_shape=jax.ShapeDtypeStruct((B, R_HEADS, T, LANES), BF16),
        scratch_shapes=[pltpu.VMEM((2, T, LANES), F32),
                        pltpu.VMEM((4, LANES, R_DV), F32),
                        pltpu.VMEM((4, RET_CHUNK, RET_CHUNK), F32),
                        pltpu.VMEM((4, RET_CHUNK, LANES), BF16),
                        pltpu.VMEM((4, RET_CHUNK, LANES), BF16),
                        pltpu.VMEM((2, RET_CHUNK, LANES), BF16),
                        pltpu.VMEM((2, n_units, RET_CHUNK, RET_CHUNK), BF16),
                        pltpu.VMEM((2, n_units, LANES, R_DV), F32)],
        compiler_params=pltpu.CompilerParams(
            dimension_semantics=("arbitrary", "arbitrary"), vmem_limit_bytes=VMEM_LIMIT),
        name="retention",
    )(qr, kr, vr, gr, lg_rows)


def _outproj_kernel(ya_ref, yr_ref, mg_ref, x_ref, gate_ref, wpa_ref, wpb_ref, wout_ref,
                    lng_ref, lnb_ref, o_ref):
    ya = jnp.concatenate([ya_ref[0, h] for h in range(A_HEADS)], axis=1)
    yr = jnp.concatenate([yr_ref[0, h] for h in range(R_HEADS)], axis=1)
    pa = _dot(ya, wpa_ref[...])
    pr = _dot(yr, wpb_ref[...])
    mg = mg_ref[0]
    m = mg[:, :D_MODEL].astype(F32) * pa + mg[:, D_MODEL:].astype(F32) * pr
    s = _dot(m.astype(BF16), wout_ref[...])
    h = DEEPNORM_ALPHA * x_ref[0] + gate_ref[0] * s
    mu = jnp.mean(h, axis=-1, keepdims=True)
    cen = h - mu
    var = jnp.mean(cen * cen, axis=-1, keepdims=True)
    o_ref[0] = cen * lax.rsqrt(var + LN_EPS) * lng_ref[...] + lnb_ref[...]


def _outproj_call(ya, yr, mg, x, gate1p, w_pa, w_pb, w_out, ln_g, ln_b):
    B, T, D = x.shape
    rows = min(OUT_ROWS, T)
    hspec = pl.BlockSpec((1, A_HEADS, rows, LANES), lambda b, t: (b, 0, t, 0))
    return pl.pallas_call(
        _outproj_kernel,
        grid=(B, T // rows),
        in_specs=[hspec, hspec,
                  pl.BlockSpec((1, rows, 2 * D), lambda b, t: (b, t, 0)),
                  pl.BlockSpec((1, rows, D), lambda b, t: (b, t, 0)),
                  pl.BlockSpec((1, 1, D), lambda b, t: (b, 0, 0)),
                  _const_spec((D, D)), _const_spec((D, D)), _const_spec((D, D)),
                  _const_spec((1, D)), _const_spec((1, D))],
        out_specs=pl.BlockSpec((1, rows, D), lambda b, t: (b, t, 0)),
        out_shape=jax.ShapeDtypeStruct((B, T, D), F32),
        compiler_params=pltpu.CompilerParams(
            dimension_semantics=("arbitrary", "arbitrary"), vmem_limit_bytes=VMEM_LIMIT),
        name="outproj",
    )(ya, yr, mg, x, gate1p, w_pa, w_pb, w_out, ln_g, ln_b)


def _rope_perm():
    half = R_DK // 2
    perm = []
    for p in range(R_PAIRS):
        for part in range(2):
            for i in range(2):
                h = 2 * p + i
                perm.extend(h * R_DK + part * half + np.arange(half))
    return np.asarray(perm, np.int32)


def _rope_tables(T):
    half = R_DK // 2
    pos = jnp.arange(T, dtype=F32)
    inv = ROPE_BASE ** (-jnp.arange(0, R_DK, 2, dtype=F32) / R_DK)
    ang = pos[:, None] * inv[None, :]
    cos, sin = jnp.cos(ang), jnp.sin(ang)
    return jnp.tile(cos, (1, LANES // half)), jnp.concatenate([-sin, -sin, sin, sin], axis=1)


def _trunk(x, ada, params):
    B, T, D = x.shape
    cos_p, sin_p = _rope_tables(T)
    for l in range(DEPTH):
        p = params[l]
        shift = ada[l, :, :D].reshape(B, 1, D)
        sc1p = (1.0 + ada[l, :, D:2 * D]).reshape(B, 1, D)
        gate1p = (1.0 + ada[l, :, 2 * D:]).reshape(B, 1, D)
        qa, hif, lof, kf, hib, lob, kb, va, ga, qr, kr, vr, gr, mg, dstat, mstat = _inproj_call(
            x, sc1p, shift, p["w_in"], p["w_mg"], p["b_mg"], p["lbm"], p["oml"], cos_p, sin_p)
        ya = _hgrn_call(qa, hif, lof, kf, hib, lob, kb, va, ga, dstat, mstat, p["nw"])
        yr = _ret_call(qr, kr, vr, gr, p["lg_rows"])
        x = _outproj_call(ya, yr, mg, x, gate1p, p["w_pa"], p["w_pb"], p["w_out"],
                          p["ln_g"], p["ln_b"])
    return x


def kernel(x_prompt, x_sample, c_prompt, c_sample, w_ada, b_ada, w_in, hgrn_lb, a_norm_w, ret_decay,
           w_pa, w_pb, w_mg, b_mg, w_out, ln_g, ln_b):
    D = D_MODEL
    pr = jax.nn.softmax(hgrn_lb.astype(F32), axis=0)
    lbs = jnp.cumsum(pr, axis=0) - pr[0:1]
    perm = _rope_perm()
    params = []
    for l in range(DEPTH):
        lb = lbs[l]
        lbm = jnp.maximum(lb, LOG_FLOOR)
        wl = w_in[l]
        wl = jnp.concatenate([wl[:, :OFF_RQ], wl[:, OFF_RQ + perm], wl[:, OFF_RK + perm],
                              wl[:, OFF_RV:]], axis=1)
        lg = jax.nn.log_sigmoid(ret_decay[l].astype(F32))
        lg_rows = jnp.broadcast_to(
            lg.reshape(2, R_PAIRS, 2).transpose(1, 0, 2).reshape(R_PAIRS, 4, 1), (R_PAIRS, 4, LANES))
        params.append(dict(
            w_in=wl.astype(BF16), w_mg=w_mg[l].astype(BF16), b_mg=b_mg[l].reshape(1, 2 * D),
            lbm=lbm.reshape(1, D), oml=(1.0 - lb).reshape(1, D),
            nw=a_norm_w[l].reshape(1, A_DV), lg_rows=lg_rows,
            w_pa=w_pa[l].astype(BF16), w_pb=w_pb[l].astype(BF16), w_out=w_out[l].astype(BF16),
            ln_g=ln_g[l].reshape(1, D), ln_b=ln_b[l].reshape(1, D)))
    nbp = x_prompt.shape[0]
    ada = _ada_call(jnp.concatenate([c_prompt, c_sample], axis=0), w_ada, b_ada)
    y_prompt = _trunk(x_prompt, ada[:, :nbp], params)
    y_sample = _trunk(x_sample, ada[:, nbp:], params)
    return (y_prompt, y_sample)
```

```python
import functools

import numpy as np
import jax
import jax.numpy as jnp
from jax import lax
from jax.experimental import pallas as pl
from jax.experimental.pallas import tpu as pltpu

F32 = jnp.float32
BF16 = jnp.bfloat16

D_MODEL = 1024
DEPTH = 2
A_HEADS = 8
A_DK = 128
A_DV = 128
R_HEADS = 8
R_DK = 64
R_DV = 128
R_PAIRS = R_HEADS // 2
LN_EPS = 1e-5
LOG_FLOOR = 1e-30
NEG_BIG = -1e30
DEEPNORM_ALPHA = (2.0 * DEPTH) ** 0.25
ROPE_BASE = 10000.0

OFF_AQ, OFF_FF, OFF_FB, OFF_AI, OFF_AG, OFF_RQ, OFF_RK, OFF_RV, OFF_RG = (
    0, 1024, 2048, 3072, 4096, 5120, 5632, 6144, 7168)
IN_WIDTH = 8192

LANES = 128
HGRN_CHUNK = 64
HGRN_LEVELS = 6
RET_CHUNK = 128
HGRN_SAFE_DECAY = 75.0
HGRN_SAFE_MAG = 1e5
HGRN_FAST_UNROLL = 8
IN_ROLL_LAG = 1
HGRN_ROLL_LAG = 12
RET_STEP_CHUNKS = 4
RET_ROLL_LAG = 4
IN_ROWS = 256
OUT_ROWS = 1024
OUT_SUB_ROWS = 256
VMEM_LIMIT = 56 * 1024 * 1024


def _dot(a, b):
    return jnp.dot(a, b, preferred_element_type=F32)


def _dot_nt(a, b):
    return lax.dot_general(a, b, (((1,), (1,)), ((), ())), preferred_element_type=F32)


def _dot_tn(a, b):
    return lax.dot_general(a, b, (((0,), (0,)), ((), ())), preferred_element_type=F32)


def _sigmoid(z):
    return 1.0 / (1.0 + jnp.exp(-z))


def _const_spec(shape):
    nd = len(shape)
    return pl.BlockSpec(shape, lambda *_: (0,) * nd, pipeline_mode=pl.Buffered(1))


def _ada_kernel(c_ref, w_ref, b_ref, o_ref):
    c = c_ref[...]
    a = c * _sigmoid(c)
    o_ref[0] = jnp.dot(a, w_ref[0], preferred_element_type=F32,
                       precision=lax.Precision.HIGHEST) + b_ref[0]


def _ada_call(c, w_ada, b_ada):
    nb = c.shape[0]
    d = D_MODEL
    return pl.pallas_call(
        _ada_kernel,
        grid=(DEPTH, 3),
        in_specs=[
            pl.BlockSpec((nb, d), lambda l, j: (0, 0)),
            pl.BlockSpec((1, d, d), lambda l, j: (l, 0, j)),
            pl.BlockSpec((1, 1, d), lambda l, j: (l, 0, j)),
        ],
        out_specs=pl.BlockSpec((1, nb, d), lambda l, j: (l, 0, j)),
        out_shape=jax.ShapeDtypeStruct((DEPTH, nb, 3 * d), F32),
        compiler_params=pltpu.CompilerParams(
            dimension_semantics=("arbitrary", "arbitrary"), vmem_limit_bytes=VMEM_LIMIT),
        name="adaln",
    )(c, w_ada, b_ada.reshape(DEPTH, 1, 3 * d))


def _inproj_kernel(x_ref, sc_ref, sh_ref, win_ref, wmg_ref, bmg_ref, lbm_ref, oml_ref,
                   cos_ref, sin_ref,
                   qa_ref, lff_ref, kf_ref, lfb_ref, kb_ref, va_ref, ga_ref,
                   qr_ref, kr_ref, vr_ref, gr_ref, mg_ref, dstat_ref, mstat_ref):
    rows = x_ref.shape[1]
    u = (x_ref[0] * sc_ref[0] + sh_ref[0]).astype(BF16)

    def col_max(z):
        return jnp.max(jnp.abs(z), axis=0, keepdims=True)

    def store_heads(ref, g, z):
        for i in range(2):
            ref[0, 2 * g + i] = z[:, i * LANES:(i + 1) * LANES].astype(BF16)

    tasks = []
    stats = {}

    def q_epilogue(g, z):
        z = z * _sigmoid(z) * (A_DK ** -0.5)
        stats[g] = [col_max(z), None]
        store_heads(qa_ref, g, z)

    def forget_epilogue(g, lf_ref, k_ref, z):
        lbm = lbm_ref[:, 2 * LANES * g:2 * LANES * (g + 1)]
        oml = oml_ref[:, 2 * LANES * g:2 * LANES * (g + 1)]
        f = lbm + oml * _sigmoid(z)
        lf = jnp.log(f)
        halfsum = jnp.sum(lf.reshape(2 * rows // HGRN_CHUNK, HGRN_CHUNK // 2, 2 * LANES), axis=1)
        halfmin = jnp.min(halfsum, axis=0, keepdims=True)
        for i in range(2):
            lf_ref[0, 2 * g + i] = lf[:, i * LANES:(i + 1) * LANES]
        store_heads(k_ref, g, (oml + lbm) - f)
        stats[g][1] = halfmin if stats[g][1] is None else jnp.minimum(stats[g][1], halfmin)

    def v_epilogue(g, z):
        mag, dmin = stats.pop(g)
        mag = jnp.maximum(mag, col_max(z))
        store_heads(va_ref, g, z)
        for i in range(2):
            lanes = slice(i * LANES, (i + 1) * LANES)
            dstat_ref[0, 0, 2 * g + i:2 * g + i + 1, :] = dmin[:, lanes]
            mstat_ref[0, 0, 2 * g + i:2 * g + i + 1, :] = mag[:, lanes]

    def silu_epilogue(ref, g, z):
        store_heads(ref, g, z * _sigmoid(z))

    def rope_epilogue(ref, g, scale, z):
        for i in range(2):
            zz = z[:, i * LANES:(i + 1) * LANES]
            rot = pltpu.roll(zz, LANES // 2, 1)
            ref[0, 2 * g + i] = ((zz * cos_ref[...] + rot * sin_ref[...]) * scale).astype(BF16)

    def merge_epilogue(g, z):
        c = 2 * LANES * g
        mg_ref[0, :, c:c + 2 * LANES] = _sigmoid(z + bmg_ref[:, c:c + 2 * LANES]).astype(BF16)

    part = functools.partial
    for g in range(A_HEADS // 2):
        c = 2 * LANES * g
        tasks += [(win_ref, OFF_AQ + c, part(q_epilogue, g)),
                  (win_ref, OFF_FF + c, part(forget_epilogue, g, lff_ref, kf_ref)),
                  (win_ref, OFF_RV + c, part(store_heads, vr_ref, g)),
                  (win_ref, OFF_FB + c, part(forget_epilogue, g, lfb_ref, kb_ref)),
                  (win_ref, OFF_AI + c, part(v_epilogue, g)),
                  (win_ref, OFF_AG + c, part(silu_epilogue, ga_ref, g)),
                  (win_ref, OFF_RG + c, part(silu_epilogue, gr_ref, g))]
    for g in range(R_PAIRS // 2):
        c = 2 * LANES * g
        tasks += [(win_ref, OFF_RQ + c, part(rope_epilogue, qr_ref, g, R_DK ** -0.5)),
                  (win_ref, OFF_RK + c, part(rope_epilogue, kr_ref, g, 1.0))]
    for g in range(2 * D_MODEL // (2 * LANES)):
        tasks.append((wmg_ref, 2 * LANES * g, part(merge_epilogue, g)))

    pending = []
    for w_ref, c0, epilogue in tasks:
        pending.append((epilogue, _dot(u, w_ref[:, c0:c0 + 2 * LANES])))
        if len(pending) > IN_ROLL_LAG:
            ep, z = pending.pop(0)
            ep(z)
    for ep, z in pending:
        ep(z)
    assert not stats


def _inproj_call(x, sc1p, shift, w_in, w_mg, b_mg, lbm, oml, cos_p, sin_p):
    B, T, D = x.shape
    rows = min(IN_ROWS, T)
    grid = (B, T // rows)
    hd = lambda n, dt: jax.ShapeDtypeStruct((B, n, T, LANES), dt)
    hspec = lambda n: pl.BlockSpec((1, n, rows, LANES), lambda b, t: (b, 0, t, 0))
    bvec = pl.BlockSpec((1, 1, D), lambda b, t: (b, 0, 0))
    stat_spec = pl.BlockSpec((1, 1, A_HEADS, LANES), lambda b, t: (b, t, 0, 0))
    stat_shape = jax.ShapeDtypeStruct((B, T // rows, A_HEADS, LANES), F32)
    return pl.pallas_call(
        _inproj_kernel,
        grid=grid,
        in_specs=[
            pl.BlockSpec((1, rows, D), lambda b, t: (b, t, 0)),
            bvec, bvec,
            _const_spec((D, IN_WIDTH)),
            _const_spec((D, 2 * D)),
            _const_spec((1, 2 * D)),
            _const_spec((1, D)),
            _const_spec((1, D)),
            pl.BlockSpec((rows, LANES), lambda b, t: (t, 0)),
            pl.BlockSpec((rows, LANES), lambda b, t: (t, 0)),
        ],
        out_specs=[hspec(A_HEADS)] * 7 + [
            hspec(R_PAIRS), hspec(R_PAIRS), hspec(R_HEADS), hspec(R_HEADS),
            pl.BlockSpec((1, rows, 2 * D), lambda b, t: (b, t, 0)), stat_spec, stat_spec],
        out_shape=[hd(A_HEADS, BF16), hd(A_HEADS, F32), hd(A_HEADS, BF16), hd(A_HEADS, F32),
                   hd(A_HEADS, BF16), hd(A_HEADS, BF16), hd(A_HEADS, BF16),
            hd(R_PAIRS, BF16), hd(R_PAIRS, BF16), hd(R_HEADS, BF16), hd(R_HEADS, BF16),
            jax.ShapeDtypeStruct((B, T, 2 * D), BF16), stat_shape, stat_shape],
        compiler_params=pltpu.CompilerParams(
            dimension_semantics=("arbitrary", "arbitrary"), vmem_limit_bytes=VMEM_LIMIT),
        name="inproj",
    )(x, sc1p, shift, w_in, w_mg, b_mg, lbm, oml, cos_p, sin_p)


def _hgrn_tables(reverse):
    C = HGRN_CHUNK
    idx = np.arange(C)
    cum = (idx[None, :] >= idx[:, None]) if reverse else (idx[None, :] <= idx[:, None])
    cum = cum.astype(np.float64)
    end = 0 if reverse else C - 1
    blocks = []
    for j in range(HGRN_LEVELS):
        m = C >> (j + 1)
        base = idx - idx % (2 * m)
        ref = base + (m if reverse else m - 1)
        blocks.append(cum - cum[ref])
    blocks.append(cum)
    blocks.append(cum[end][None, :] - cum)
    safe_tab = np.concatenate(blocks, axis=0)
    mid = C // 2 if reverse else C // 2 - 1
    fast_tab = np.tile(cum - cum[mid][None, :], (1, 2))
    x = idx[:, None] ^ idx[None, :]
    bl = np.zeros_like(x)
    nz = x > 0
    bl[nz] = np.floor(np.log2(x[nz])).astype(np.int64) + 1
    lvl = HGRN_LEVELS - bl
    coupled = (idx[:, None] <= idx[None, :]) if reverse else (idx[:, None] >= idx[None, :])
    lvl = np.where(coupled, lvl, -1)
    return safe_tab.astype(np.float32), fast_tab.astype(np.float32), lvl.astype(np.int32), end


def _hgrn_kernel(q_ref, lff_ref, kf_ref, lfb_ref, kb_ref, v_ref, g_ref,
                 dstat_ref, mstat_ref, nw_ref,
                 tabf_ref, tabb_ref, ftabf_ref, ftabb_ref, lvf_ref, lvb_ref,
                 y_ref, o_scr, sf_scr, sb_scr, ops_scr, dec_scr, a_scr, u_scr, *, ends):
    C = HGRN_CHUNK
    T = q_ref.shape[2]
    n_chunks = T // C
    sf_scr[...] = jnp.zeros_like(sf_scr)
    sb_scr[...] = jnp.zeros_like(sb_scr)
    nw = nw_ref[...]

    def load(ci, gate_refs):
        rows = pl.ds(pl.multiple_of(ci * C, C), C)
        lf_ref, k_ref = gate_refs
        lf = lf_ref[0, 0, rows, :]
        hi = lf.astype(BF16)
        lo = (lf - hi.astype(F32)).astype(BF16)
        return (rows, q_ref[0, 0, rows, :], k_ref[0, 0, rows, :], v_ref[0, 0, rows, :],
                (lf, hi, lo))

    def advance(s_scr, a, v, qi, kd, s_decay):
        st = s_scr[...]
        o = _dot(a.astype(BF16), v) + _dot_nt(qi, st.astype(BF16))
        s_scr[...] = st * s_decay + _dot_tn(v, kd)
        return o


    def chunk_rows(ci):
        return pl.ds(pl.multiple_of(ci * C, C), C)

    def prepare_task(c, ci, p, slot):
        start = C - 1 - p[4]

        def issue():
            _, q, k, _, (lf, hi, lo) = load(ci, p[0])
            return q, k, lf[start:start + 1], _dot(p[1][...], jnp.concatenate([hi, lo], axis=0))

        def consume(issued):
            q, k, lf_start, d = issued
            b_mid = lf_start - d[start:start + 1]
            d_end = d[p[4]:p[4] + 1]
            qh = q * jnp.exp(d).astype(BF16)
            kh = k * jnp.exp(-d).astype(BF16)
            ops_scr[slot, c, 0] = qh
            ops_scr[slot, c, 1] = kh
            ops_scr[slot, c, 2] = qh * jnp.exp(b_mid).astype(BF16)
            ops_scr[slot, c, 3] = kh * jnp.exp(d_end).astype(BF16)
            dec_scr[slot, c] = jnp.broadcast_to(jnp.exp(b_mid + d_end), (8, LANES))
        return issue, consume

    def scores_task(c, ci, p, ops_slot, slot):
        def issue():
            return (_dot_nt(ops_scr[ops_slot, c, 0], ops_scr[ops_slot, c, 1]),
                    _dot_tn(v_ref[0, 0, chunk_rows(ci), :], ops_scr[ops_slot, c, 3]))

        def consume(issued):
            a_scr[slot, c] = jnp.where(p[2][...] >= 0, issued[0], 0.0).astype(BF16)
            u_scr[slot, c] = issued[1]
        return issue, consume

    def finish_task(c, ci, p, ops_slot, slot, finish):
        rows = chunk_rows(ci)

        def issue():
            st = p[3][...]
            o = (_dot(a_scr[slot, c], v_ref[0, 0, rows, :])
                 + _dot_nt(ops_scr[ops_slot, c, 2], st.astype(BF16)))
            p[3][...] = st * dec_scr[ops_slot, c, 0:1, :] + u_scr[slot, c]
            return o

        def consume(o):
            finish(rows, o)
        return issue, consume

    def rolling(tasks, lag):
        pending = []
        for issue, consume in tasks:
            pending.append((consume, issue()))
            if len(pending) > lag:
                cons, res = pending.pop(0)
                cons(res)
        for cons, res in pending:
            cons(res)

    def safe_chunk(ci, lf_ref, tab_ref, lv_ref, s_scr, end):
        rows, qb, kb, v, (_, hi, lo) = load(ci, lf_ref)
        q = qb.astype(F32)
        k = kb.astype(F32)
        r = _dot(tab_ref[...], jnp.concatenate([hi, lo], axis=1))
        dm = r[:, :LANES] + r[:, LANES:]
        lv = lv_ref[...]
        a = jnp.where(lv == HGRN_LEVELS, _dot_nt(qb, kb), 0.0)
        for j in range(HGRN_LEVELS):
            e = jnp.exp(-jnp.abs(dm[j * C:(j + 1) * C]))
            aj = _dot_nt((q * e).astype(BF16), (k * e).astype(BF16))
            a = jnp.where(lv == j, aj, a)
        bq = dm[HGRN_LEVELS * C:(HGRN_LEVELS + 1) * C]
        bk = dm[(HGRN_LEVELS + 1) * C:(HGRN_LEVELS + 2) * C]
        qi = (q * jnp.exp(bq)).astype(BF16)
        kd = (k * jnp.exp(bk)).astype(BF16)
        return rows, advance(s_scr, a, v, qi, kd, jnp.exp(bq[end:end + 1, :]))

    def finalize(rows, o):
        tot = o_scr[rows, :] + o
        ms = jnp.mean(tot * tot, axis=-1, keepdims=True)
        y = tot * lax.rsqrt(ms + LN_EPS) * nw * g_ref[0, 0, rows, :].astype(F32)
        y_ref[0, 0, rows, :] = y.astype(BF16)

    def first_visit(rows, o):
        o_scr[rows, :] = o

    def chain_groups(tf_ref, tb_ref, per_step):
        fwd = ((lff_ref, kf_ref), tf_ref, lvf_ref, sf_scr, ends[0])
        bwd = ((lfb_ref, kb_ref), tb_ref, lvb_ref, sb_scr, ends[1])

        def chains(g):
            out = []
            for u in range(per_step):
                i = g * per_step + u
                out += [(i, fwd), (n_chunks - 1 - i, bwd)]
            return out
        return chains

    def loop(lo, hi, body, finish):
        def wrapped(g, carry):
            body(g, finish)
            return carry
        lax.fori_loop(lo, hi, wrapped, 0)

    def run_safe():
        chains = chain_groups(tabf_ref, tabb_ref, 1)

        def step(g, finish):
            for ci, p in chains(g):
                finish(*safe_chunk(ci, *p))

        loop(0, n_chunks // 2, step, first_visit)
        loop(n_chunks // 2, n_chunks, step, finalize)

    def run_fast():
        per_step = ops_scr.shape[1] // 2
        chains = chain_groups(ftabf_ref, ftabb_ref, per_step)
        n_groups = n_chunks // per_step
        last = n_groups - 1

        def step(g, finish):
            g1 = jnp.minimum(g + 1, last)
            g2 = jnp.minimum(g + 2, last)
            tasks = []
            for c, ((c1, p1), (c2, p2), (c0, p0)) in enumerate(zip(chains(g1), chains(g2), chains(g))):
                tasks.append(scores_task(c, c1, p1, g1 % 3, (g + 1) % 2))
                tasks.append(prepare_task(c, c2, p2, (g + 2) % 3))
                tasks.append(finish_task(c, c0, p0, g % 3, g % 2, finish))
            rolling(tasks, HGRN_ROLL_LAG)

        for g in range(2):
            rolling([prepare_task(c, ci, p, g) for c, (ci, p) in enumerate(chains(g))], HGRN_ROLL_LAG)
        rolling([scores_task(c, ci, p, 0, 0) for c, (ci, p) in enumerate(chains(0))], HGRN_ROLL_LAG)
        loop(0, n_groups // 2, step, first_visit)
        loop(n_groups // 2, n_groups, step, finalize)

    is_head = lax.broadcasted_iota(jnp.int32, (A_HEADS, LANES), 0) == pl.program_id(1)
    dec = jnp.min(jnp.where(is_head, jnp.min(dstat_ref[0], axis=0), 0.0))
    mag = jnp.max(jnp.where(is_head, jnp.max(mstat_ref[0], axis=0), 0.0))
    is_safe = jnp.logical_and(dec > -HGRN_SAFE_DECAY, mag < HGRN_SAFE_MAG)

    pl.when(is_safe)(run_fast)
    pl.when(jnp.logical_not(is_safe))(run_safe)


def _hgrn_call(qa, lff, kf, lfb, kb, va, ga, dstat, mstat, nw):
    B, H, T, _ = qa.shape
    stat = pl.BlockSpec((1,) + dstat.shape[1:], lambda b, h: (b, 0, 0, 0))
    tabf, ftabf, lvf, endf = _hgrn_tables(False)
    tabb, ftabb, lvb, endb = _hgrn_tables(True)
    seq = pl.BlockSpec((1, 1, T, LANES), lambda b, h: (b, h, 0, 0))
    n_chunks = T // HGRN_CHUNK
    assert n_chunks % 4 == 0, T
    per_step = max(u for u in (HGRN_FAST_UNROLL, 2, 1) if n_chunks % (4 * u) == 0)
    return pl.pallas_call(
        functools.partial(_hgrn_kernel, ends=(endf, endb)),
        grid=(B, H),
        in_specs=[seq] * 7 + [stat, stat,
                  _const_spec((1, LANES)),
                  _const_spec(tabf.shape), _const_spec(tabb.shape),
                  _const_spec(ftabf.shape), _const_spec(ftabb.shape),
                  _const_spec(lvf.shape), _const_spec(lvb.shape)],
        out_specs=seq,
        out_shape=jax.ShapeDtypeStruct((B, H, T, LANES), BF16),
        scratch_shapes=[pltpu.VMEM((T, LANES), F32),
                        pltpu.VMEM((A_DV, A_DK), F32),
                        pltpu.VMEM((A_DV, A_DK), F32),
                        pltpu.VMEM((3, 2 * per_step, 4, HGRN_CHUNK, LANES), BF16),
                        pltpu.VMEM((3, 2 * per_step, 8, LANES), F32),
                        pltpu.VMEM((2, 2 * per_step, HGRN_CHUNK, HGRN_CHUNK), BF16),
                        pltpu.VMEM((2, 2 * per_step, A_DV, A_DK), F32)],
        compiler_params=pltpu.CompilerParams(
            dimension_semantics=("arbitrary", "arbitrary"), vmem_limit_bytes=VMEM_LIMIT),
        name="hgrn",
    )(qa, lff, kf, lfb, kb, va, ga, dstat, mstat, nw,
      jnp.asarray(tabf, BF16), jnp.asarray(tabb, BF16),
      jnp.asarray(ftabf, BF16), jnp.asarray(ftabb, BF16), jnp.asarray(lvf), jnp.asarray(lvb))


def _ret_kernel(q_ref, k_ref, v_ref, g_ref, lg_ref, y_ref,
                o_scr, s_scr, dec_scr, qd_scr, kd_scr, hm_scr, a_scr, u_scr):
    C = RET_CHUNK
    T = q_ref.shape[2]
    n_chunks = T // C
    per_step = a_scr.shape[1] // 4
    n_groups = n_chunks // per_step
    s_scr[...] = jnp.zeros_like(s_scr)

    lane = lax.broadcasted_iota(jnp.int32, (1, LANES), 1)
    head_mask = [(lane % (LANES // 2)) < (LANES // 4), (lane % (LANES // 2)) >= (LANES // 4)]
    tpos = lax.broadcasted_iota(jnp.int32, (C, C), 0)
    spos = lax.broadcasted_iota(jnp.int32, (C, C), 1)
    pos = lax.broadcasted_iota(jnp.int32, (C, 1), 0).astype(F32)
    s_dec = []
    for i in range(2):
        hm_scr[i] = jnp.broadcast_to(jnp.where(head_mask[i], 1.0, 0.0), (C, LANES)).astype(BF16)
    for d in range(2):
        dist = ((tpos - spos) if d == 0 else (spos - tpos)).astype(F32)
        after = pos if d == 0 else (C - 1.0 - pos)
        for i in range(2):
            lg = lg_ref[0, 2 * d + i:2 * d + i + 1, :]
            dec_scr[2 * d + i] = jnp.exp(jnp.where(dist >= 0, dist * lg, NEG_BIG))
            qd_scr[2 * d + i] = jnp.where(head_mask[i], jnp.exp((after + 1.0) * lg), 0.0).astype(BF16)
            kd_scr[2 * d + i] = jnp.where(head_mask[i], jnp.exp((C - 1.0 - after) * lg), 0.0).astype(BF16)
            s_dec.append(jnp.exp(C * lg))

    def chunk_rows(ci):
        return pl.ds(pl.multiple_of(ci * C, C), C)

    def units(g):
        out = []
        for u in range(per_step):
            j = g * per_step + u
            out += [(j, 0), (n_chunks - 1 - j, 1)]
        return out

    def head_units(us):
        return list(enumerate((ci, d, i) for ci, d in us for i in range(2)))

    def scores_task(n, ci, d, i, slot):
        rows = chunk_rows(ci)

        def issue():
            q = q_ref[0, 0, rows, :]
            k = k_ref[0, 0, rows, :]
            qm = q * hm_scr[i]
            km = k * kd_scr[2 * d + i]
            return _dot_nt(qm, k), _dot_tn(km, v_ref[0, i, rows, :])

        def consume(issued):
            a_scr[slot, n] = (issued[0] * dec_scr[2 * d + i]).astype(BF16)
            u_scr[slot, n] = issued[1]
        return issue, consume

    def finish_task(n, ci, d, i, slot, finish):
        rows = chunk_rows(ci)

        def issue():
            st = s_scr[2 * d + i]
            qd = q_ref[0, 0, rows, :] * qd_scr[2 * d + i]
            lhs = jnp.concatenate([a_scr[slot, n], qd], axis=1)
            rhs = jnp.concatenate([v_ref[0, i, rows, :], st.astype(BF16)], axis=0)
            s_scr[2 * d + i] = st * s_dec[2 * d + i] + u_scr[slot, n]
            return _dot(lhs, rhs)

        def consume(o):
            finish(i, rows, o)
        return issue, consume

    def rolling(tasks, lag):
        pending = []
        for issue, consume in tasks:
            pending.append((consume, issue()))
            if len(pending) > lag:
                cons, res = pending.pop(0)
                cons(res)
        for cons, res in pending:
            cons(res)

    def first_visit(i, rows, o):
        o_scr[i, rows, :] = o

    def finalize(i, rows, o):
        tot = o_scr[i, rows, :] + o
        mu = jnp.mean(tot, axis=-1, keepdims=True)
        var = jnp.mean(tot * tot, axis=-1, keepdims=True) - mu * mu
        y = (tot - mu) * lax.rsqrt(var + LN_EPS) * g_ref[0, i, rows, :].astype(F32)
        y_ref[0, i, rows, :] = y.astype(BF16)

    def step(g, finish):
        nxt = head_units(units(jnp.minimum(g + 1, n_groups - 1)))
        cur = head_units(units(g))
        tasks = []
        for (n, (c1, d1, i1)), (_, (c0, d0, i0)) in zip(nxt, cur):
            tasks.append(scores_task(n, c1, d1, i1, (g + 1) % 2))
            tasks.append(finish_task(n, c0, d0, i0, g % 2, finish))
        rolling(tasks, RET_ROLL_LAG)

    def loop(lo, hi, finish):
        def wrapped(g, carry):
            step(g, finish)
            return carry
        lax.fori_loop(lo, hi, wrapped, 0)

    rolling([scores_task(n, ci, d, i, 0) for n, (ci, d, i) in head_units(units(0))], RET_ROLL_LAG)
    loop(0, n_groups // 2, first_visit)
    loop(n_groups // 2, n_groups, finalize)


def _ret_call(qr, kr, vr, gr, lg_rows):
    B, P, T, _ = qr.shape
    n_chunks = T // RET_CHUNK
    assert n_chunks % 2 == 0, T
    per_step = max(u for u in (RET_STEP_CHUNKS, 1) if n_chunks % (2 * u) == 0)
    n_units = 4 * per_step
    seq1 = pl.BlockSpec((1, 1, T, LANES), lambda b, p: (b, p, 0, 0))
    seq2 = pl.BlockSpec((1, 2, T, LANES), lambda b, p: (b, p, 0, 0))
    return pl.pallas_call(
        _ret_kernel,
        grid=(B, P),
        in_specs=[seq1, seq1, seq2, seq2,
                  pl.BlockSpec((1, 4, LANES), lambda b, p: (p, 0, 0))],
        out_specs=seq2,
        out_shape=jax.ShapeDtypeStruct((B, R_HEADS, T, LANES), BF16),
        scratch_shapes=[pltpu.VMEM((2, T, LANES), F32),
                        pltpu.VMEM((4, LANES, R_DV), F32),
                        pltpu.VMEM((4, RET_CHUNK, RET_CHUNK), F32),
                        pltpu.VMEM((4, RET_CHUNK, LANES), BF16),
                        pltpu.VMEM((4, RET_CHUNK, LANES), BF16),
                        pltpu.VMEM((2, RET_CHUNK, LANES), BF16),
                        pltpu.VMEM((2, n_units, RET_CHUNK, RET_CHUNK), BF16),
                        pltpu.VMEM((2, n_units, LANES, R_DV), F32)],
        compiler_params=pltpu.CompilerParams(
            dimension_semantics=("arbitrary", "arbitrary"), vmem_limit_bytes=VMEM_LIMIT),
        name="retention",
    )(qr, kr, vr, gr, lg_rows)


def _outproj_kernel(ya_ref, yr_ref, mg_ref, x_ref, gate_ref, wpa_ref, wpb_ref, wout_ref,
                    lng_ref, lnb_ref, o_ref, m_scr):
    rows = x_ref.shape[1]
    sub = min(OUT_SUB_ROWS, rows)

    def branch_task(r):
        rs = slice(r * sub, (r + 1) * sub)

        def issue():
            ya = jnp.concatenate([ya_ref[0, h, rs, :] for h in range(A_HEADS)], axis=1)
            yr = jnp.concatenate([yr_ref[0, h, rs, :] for h in range(R_HEADS)], axis=1)
            return _dot(ya, wpa_ref[...]), _dot(yr, wpb_ref[...])

        def consume(issued):
            pa, pr = issued
            mg = mg_ref[0, rs, :]
            m = mg[:, :D_MODEL].astype(F32) * pa + mg[:, D_MODEL:].astype(F32) * pr
            m_scr[rs, :] = m.astype(BF16)
        return issue, consume

    def out_task(r):
        rs = slice(r * sub, (r + 1) * sub)

        def issue():
            return _dot(m_scr[rs, :], wout_ref[...])

        def consume(s):
            h = DEEPNORM_ALPHA * x_ref[0, rs, :] + gate_ref[0] * s
            mu = jnp.mean(h, axis=-1, keepdims=True)
            cen = h - mu
            var = jnp.mean(cen * cen, axis=-1, keepdims=True)
            o_ref[0, rs, :] = cen * lax.rsqrt(var + LN_EPS) * lng_ref[...] + lnb_ref[...]
        return issue, consume

    n_sub = rows // sub
    tasks = [branch_task(0)]
    for r in range(n_sub):
        if r + 1 < n_sub:
            tasks.append(branch_task(r + 1))
        tasks.append(out_task(r))
    lag = 1 if n_sub > 1 else 0
    pending = []
    for issue, consume in tasks:
        pending.append((consume, issue()))
        if len(pending) > lag:
            cons, res = pending.pop(0)
            cons(res)
    for cons, res in pending:
        cons(res)


def _outproj_call(ya, yr, mg, x, gate1p, w_pa, w_pb, w_out, ln_g, ln_b):
    B, T, D = x.shape
    rows = min(OUT_ROWS, T)
    hspec = pl.BlockSpec((1, A_HEADS, rows, LANES), lambda b, t: (b, 0, t, 0))
    return pl.pallas_call(
        _outproj_kernel,
        grid=(B, T // rows),
        in_specs=[hspec, hspec,
                  pl.BlockSpec((1, rows, 2 * D), lambda b, t: (b, t, 0)),
                  pl.BlockSpec((1, rows, D), lambda b, t: (b, t, 0)),
                  pl.BlockSpec((1, 1, D), lambda b, t: (b, 0, 0)),
                  _const_spec((D, D)), _const_spec((D, D)), _const_spec((D, D)),
                  _const_spec((1, D)), _const_spec((1, D))],
        out_specs=pl.BlockSpec((1, rows, D), lambda b, t: (b, t, 0)),
        out_shape=jax.ShapeDtypeStruct((B, T, D), F32),
        scratch_shapes=[pltpu.VMEM((rows, D), BF16)],
        compiler_params=pltpu.CompilerParams(
            dimension_semantics=("arbitrary", "arbitrary"), vmem_limit_bytes=VMEM_LIMIT),
        name="outproj",
    )(ya, yr, mg, x, gate1p, w_pa, w_pb, w_out, ln_g, ln_b)


def _rope_perm():
    half = R_DK // 2
    perm = []
    for p in range(R_PAIRS):
        for part in range(2):
            for i in range(2):
                h = 2 * p + i
                perm.extend(h * R_DK + part * half + np.arange(half))
    return np.asarray(perm, np.int32)


def _rope_tables(T):
    half = R_DK // 2
    pos = jnp.arange(T, dtype=F32)
    inv = ROPE_BASE ** (-jnp.arange(0, R_DK, 2, dtype=F32) / R_DK)
    ang = pos[:, None] * inv[None, :]
    cos, sin = jnp.cos(ang), jnp.sin(ang)
    return jnp.tile(cos, (1, LANES // half)), jnp.concatenate([-sin, -sin, sin, sin], axis=1)


def _trunk(x, ada, params):
    B, T, D = x.shape
    cos_p, sin_p = _rope_tables(T)
    for l in range(DEPTH):
        p = params[l]
        shift = ada[l, :, :D].reshape(B, 1, D)
        sc1p = (1.0 + ada[l, :, D:2 * D]).reshape(B, 1, D)
        gate1p = (1.0 + ada[l, :, 2 * D:]).reshape(B, 1, D)
        qa, lff, kf, lfb, kb, va, ga, qr, kr, vr, gr, mg, dstat, mstat = _inproj_call(
            x, sc1p, shift, p["w_in"], p["w_mg"], p["b_mg"], p["lbm"], p["oml"], cos_p, sin_p)
        ya = _hgrn_call(qa, lff, kf, lfb, kb, va, ga, dstat, mstat, p["nw"])
        yr = _ret_call(qr, kr, vr, gr, p["lg_rows"])
        x = _outproj_call(ya, yr, mg, x, gate1p, p["w_pa"], p["w_pb"], p["w_out"],
                          p["ln_g"], p["ln_b"])
    return x


def kernel(x_prompt, x_sample, c_prompt, c_sample, w_ada, b_ada, w_in, hgrn_lb, a_norm_w, ret_decay,
           w_pa, w_pb, w_mg, b_mg, w_out, ln_g, ln_b):
    D = D_MODEL
    pr = jax.nn.softmax(hgrn_lb.astype(F32), axis=0)
    lbs = jnp.cumsum(pr, axis=0) - pr[0:1]
    perm = _rope_perm()
    params = []
    for l in range(DEPTH):
        lb = lbs[l]
        lbm = jnp.maximum(lb, LOG_FLOOR)
        wl = w_in[l]
        wl = jnp.concatenate([wl[:, :OFF_RQ], wl[:, OFF_RQ + perm], wl[:, OFF_RK + perm],
                              wl[:, OFF_RV:]], axis=1)
        lg = jax.nn.log_sigmoid(ret_decay[l].astype(F32))
        lg_rows = jnp.broadcast_to(
            lg.reshape(2, R_PAIRS, 2).transpose(1, 0, 2).reshape(R_PAIRS, 4, 1), (R_PAIRS, 4, LANES))
        params.append(dict(
            w_in=wl.astype(BF16), w_mg=w_mg[l].astype(BF16), b_mg=b_mg[l].reshape(1, 2 * D),
            lbm=lbm.reshape(1, D), oml=(1.0 - lb).reshape(1, D),
            nw=a_norm_w[l].reshape(1, A_DV), lg_rows=lg_rows,
            w_pa=w_pa[l].astype(BF16), w_pb=w_pb[l].astype(BF16), w_out=w_out[l].astype(BF16),
            ln_g=ln_g[l].reshape(1, D), ln_b=ln_b[l].reshape(1, D)))
    nbp = x_prompt.shape[0]
    ada = _ada_call(jnp.concatenate([c_prompt, c_sample], axis=0), w_ada, b_ada)
    y_prompt = _trunk(x_prompt, ada[:, :nbp], params)
    y_sample = _trunk(x_sample, ada[:, nbp:], params)
    return (y_prompt, y_sample)
```

```python
import functools

import numpy as np
import jax
import jax.numpy as jnp
from jax import lax
from jax.experimental import pallas as pl
from jax.experimental.pallas import tpu as pltpu

F32 = jnp.float32
BF16 = jnp.bfloat16

D_MODEL = 1024
DEPTH = 2
A_HEADS = 8
A_DK = 128
A_DV = 128
R_HEADS = 8
R_DK = 64
R_DV = 128
R_PAIRS = R_HEADS // 2
LN_EPS = 1e-5
LOG_FLOOR = 1e-30
NEG_BIG = -1e30
DEEPNORM_ALPHA = (2.0 * DEPTH) ** 0.25
ROPE_BASE = 10000.0

OFF_AQ, OFF_FF, OFF_FB, OFF_AI, OFF_AG, OFF_RQ, OFF_RK, OFF_RV, OFF_RG = (
    0, 1024, 2048, 3072, 4096, 5120, 5632, 6144, 7168)
IN_WIDTH = 8192

LANES = 128
HGRN_CHUNK = 64
HGRN_LEVELS = 6
RET_CHUNK = 128
HGRN_SAFE_DECAY = 75.0
HGRN_SAFE_MAG = 1e5
HGRN_FAST_UNROLL = 8
IN_ROLL_LAG = 1
HGRN_ROLL_LAG = 12
RET_STEP_CHUNKS = 4
RET_ROLL_LAG = 4
IN_ROWS = 256
OUT_ROWS = 1024
OUT_SUB_ROWS = 256
VMEM_LIMIT = 56 * 1024 * 1024


def _dot(a, b):
    return jnp.dot(a, b, preferred_element_type=F32)


def _dot_nt(a, b):
    return lax.dot_general(a, b, (((1,), (1,)), ((), ())), preferred_element_type=F32)


def _dot_tn(a, b):
    return lax.dot_general(a, b, (((0,), (0,)), ((), ())), preferred_element_type=F32)


def _sigmoid(z):
    return 1.0 / (1.0 + jnp.exp(-z))


def _const_spec(shape):
    nd = len(shape)
    return pl.BlockSpec(shape, lambda *_: (0,) * nd, pipeline_mode=pl.Buffered(1))


def _layer_spec(shape, layer):
    nd = len(shape)
    return pl.BlockSpec((None,) + tuple(shape), lambda *_: (layer,) + (0,) * nd,
                        pipeline_mode=pl.Buffered(1))


def _ada_kernel(c_ref, w_ref, b_ref, o_ref):
    c = c_ref[...]
    a = c * _sigmoid(c)
    o_ref[0] = jnp.dot(a, w_ref[0], preferred_element_type=F32,
                       precision=lax.Precision.HIGHEST) + b_ref[0]


def _ada_call(c, w_ada, b_ada):
    nb = c.shape[0]
    d = D_MODEL
    return pl.pallas_call(
        _ada_kernel,
        grid=(DEPTH, 3),
        in_specs=[
            pl.BlockSpec((nb, d), lambda l, j: (0, 0)),
            pl.BlockSpec((1, d, d), lambda l, j: (l, 0, j)),
            pl.BlockSpec((1, 1, d), lambda l, j: (l, 0, j)),
        ],
        out_specs=pl.BlockSpec((1, nb, d), lambda l, j: (l, 0, j)),
        out_shape=jax.ShapeDtypeStruct((DEPTH, nb, 3 * d), F32),
        compiler_params=pltpu.CompilerParams(
            dimension_semantics=("arbitrary", "arbitrary"), vmem_limit_bytes=VMEM_LIMIT),
        name="adaln",
    )(c, w_ada, b_ada.reshape(DEPTH, 1, 3 * d))


def _inproj_kernel(x_ref, sc_ref, sh_ref, win_ref, wrope_ref, wmg_ref, bmg_ref, lbm_ref, oml_ref,
                   cos_ref, sin_ref,
                   qa_ref, lff_ref, kf_ref, lfb_ref, kb_ref, va_ref, ga_ref,
                   qr_ref, kr_ref, vr_ref, gr_ref, mg_ref, dstat_ref, mstat_ref):
    rows = x_ref.shape[1]
    u = (x_ref[0] * sc_ref[0] + sh_ref[0]).astype(BF16)

    def col_max(z):
        return jnp.max(jnp.abs(z), axis=0, keepdims=True)

    def store_heads(ref, g, z):
        for i in range(2):
            ref[0, 2 * g + i] = z[:, i * LANES:(i + 1) * LANES].astype(BF16)

    tasks = []
    stats = {}

    def q_epilogue(g, z):
        z = z * _sigmoid(z) * (A_DK ** -0.5)
        stats[g] = [col_max(z), None]
        store_heads(qa_ref, g, z)

    def forget_epilogue(g, lf_ref, k_ref, z):
        lbm = lbm_ref[:, 2 * LANES * g:2 * LANES * (g + 1)]
        oml = oml_ref[:, 2 * LANES * g:2 * LANES * (g + 1)]
        f = lbm + oml * _sigmoid(z)
        lf = jnp.log(f)
        halfsum = jnp.sum(lf.reshape(2 * rows // HGRN_CHUNK, HGRN_CHUNK // 2, 2 * LANES), axis=1)
        halfmin = jnp.min(halfsum, axis=0, keepdims=True)
        for i in range(2):
            lf_ref[0, 2 * g + i] = lf[:, i * LANES:(i + 1) * LANES]
        store_heads(k_ref, g, (oml + lbm) - f)
        stats[g][1] = halfmin if stats[g][1] is None else jnp.minimum(stats[g][1], halfmin)

    def v_epilogue(g, z):
        mag, dmin = stats.pop(g)
        mag = jnp.maximum(mag, col_max(z))
        store_heads(va_ref, g, z)
        for i in range(2):
            lanes = slice(i * LANES, (i + 1) * LANES)
            dstat_ref[0, 0, 2 * g + i:2 * g + i + 1, :] = dmin[:, lanes]
            mstat_ref[0, 0, 2 * g + i:2 * g + i + 1, :] = mag[:, lanes]

    def silu_epilogue(ref, g, z):
        store_heads(ref, g, z * _sigmoid(z))

    def rope_epilogue(ref, g, scale, z):
        for i in range(2):
            zz = z[:, i * LANES:(i + 1) * LANES]
            rot = pltpu.roll(zz, LANES // 2, 1)
            ref[0, 2 * g + i] = ((zz * cos_ref[...] + rot * sin_ref[...]) * scale).astype(BF16)

    def merge_epilogue(g, z):
        c = 2 * LANES * g
        mg_ref[0, :, c:c + 2 * LANES] = _sigmoid(z + bmg_ref[:, c:c + 2 * LANES]).astype(BF16)

    part = functools.partial
    for g in range(A_HEADS // 2):
        c = 2 * LANES * g
        tasks += [(win_ref, OFF_AQ + c, part(q_epilogue, g)),
                  (win_ref, OFF_FF + c, part(forget_epilogue, g, lff_ref, kf_ref)),
                  (win_ref, OFF_RV + c, part(store_heads, vr_ref, g)),
                  (win_ref, OFF_FB + c, part(forget_epilogue, g, lfb_ref, kb_ref)),
                  (win_ref, OFF_AI + c, part(v_epilogue, g)),
                  (win_ref, OFF_AG + c, part(silu_epilogue, ga_ref, g)),
                  (win_ref, OFF_RG + c, part(silu_epilogue, gr_ref, g))]
    for g in range(R_PAIRS // 2):
        c = 2 * LANES * g
        tasks += [(wrope_ref, c, part(rope_epilogue, qr_ref, g, R_DK ** -0.5)),
                  (wrope_ref, R_HEADS * R_DK + c, part(rope_epilogue, kr_ref, g, 1.0))]
    for g in range(2 * D_MODEL // (2 * LANES)):
        tasks.append((wmg_ref, 2 * LANES * g, part(merge_epilogue, g)))

    pending = []
    for w_ref, c0, epilogue in tasks:
        pending.append((epilogue, _dot(u, w_ref[:, c0:c0 + 2 * LANES])))
        if len(pending) > IN_ROLL_LAG:
            ep, z = pending.pop(0)
            ep(z)
    for ep, z in pending:
        ep(z)
    assert not stats


def _inproj_call(layer, x, sc1p, shift, w_in, w_rope, w_mg, b_mg, lbm, oml, cos_p, sin_p):
    B, T, D = x.shape
    rows = min(IN_ROWS, T)
    grid = (B, T // rows)
    hd = lambda n, dt: jax.ShapeDtypeStruct((B, n, T, LANES), dt)
    hspec = lambda n: pl.BlockSpec((1, n, rows, LANES), lambda b, t: (b, 0, t, 0))
    bvec = pl.BlockSpec((1, 1, D), lambda b, t: (b, 0, 0))
    stat_spec = pl.BlockSpec((1, 1, A_HEADS, LANES), lambda b, t: (b, t, 0, 0))
    stat_shape = jax.ShapeDtypeStruct((B, T // rows, A_HEADS, LANES), F32)
    return pl.pallas_call(
        _inproj_kernel,
        grid=grid,
        in_specs=[
            pl.BlockSpec((1, rows, D), lambda b, t: (b, t, 0)),
            bvec, bvec,
            _layer_spec((D, IN_WIDTH), layer),
            _layer_spec((D, 2 * R_HEADS * R_DK), layer),
            _layer_spec((D, 2 * D), layer),
            _const_spec((1, 2 * D)),
            _const_spec((1, D)),
            _const_spec((1, D)),
            pl.BlockSpec((rows, LANES), lambda b, t: (t, 0)),
            pl.BlockSpec((rows, LANES), lambda b, t: (t, 0)),
        ],
        out_specs=[hspec(A_HEADS)] * 7 + [
            hspec(R_PAIRS), hspec(R_PAIRS), hspec(R_HEADS), hspec(R_HEADS),
            pl.BlockSpec((1, rows, 2 * D), lambda b, t: (b, t, 0)), stat_spec, stat_spec],
        out_shape=[hd(A_HEADS, BF16), hd(A_HEADS, F32), hd(A_HEADS, BF16), hd(A_HEADS, F32),
                   hd(A_HEADS, BF16), hd(A_HEADS, BF16), hd(A_HEADS, BF16),
            hd(R_PAIRS, BF16), hd(R_PAIRS, BF16), hd(R_HEADS, BF16), hd(R_HEADS, BF16),
            jax.ShapeDtypeStruct((B, T, 2 * D), BF16), stat_shape, stat_shape],
        compiler_params=pltpu.CompilerParams(
            dimension_semantics=("arbitrary", "arbitrary"), vmem_limit_bytes=VMEM_LIMIT),
        name="inproj",
    )(x, sc1p, shift, w_in, w_rope, w_mg, b_mg, lbm, oml, cos_p, sin_p)


def _hgrn_tables(reverse):
    C = HGRN_CHUNK
    idx = np.arange(C)
    cum = (idx[None, :] >= idx[:, None]) if reverse else (idx[None, :] <= idx[:, None])
    cum = cum.astype(np.float64)
    end = 0 if reverse else C - 1
    blocks = []
    for j in range(HGRN_LEVELS):
        m = C >> (j + 1)
        base = idx - idx % (2 * m)
        ref = base + (m if reverse else m - 1)
        blocks.append(cum - cum[ref])
    blocks.append(cum)
    blocks.append(cum[end][None, :] - cum)
    safe_tab = np.concatenate(blocks, axis=0)
    mid = C // 2 if reverse else C // 2 - 1
    fast_tab = np.tile(cum - cum[mid][None, :], (1, 2))
    x = idx[:, None] ^ idx[None, :]
    bl = np.zeros_like(x)
    nz = x > 0
    bl[nz] = np.floor(np.log2(x[nz])).astype(np.int64) + 1
    lvl = HGRN_LEVELS - bl
    coupled = (idx[:, None] <= idx[None, :]) if reverse else (idx[:, None] >= idx[None, :])
    lvl = np.where(coupled, lvl, -1)
    return safe_tab.astype(np.float32), fast_tab.astype(np.float32), lvl.astype(np.int32), end


def _hgrn_kernel(q_ref, lff_ref, kf_ref, lfb_ref, kb_ref, v_ref, g_ref,
                 dstat_ref, mstat_ref, nw_ref,
                 tabf_ref, tabb_ref, ftabf_ref, ftabb_ref, lvf_ref, lvb_ref,
                 y_ref, o_scr, sf_scr, sb_scr, ops_scr, dec_scr, a_scr, u_scr, *, ends):
    C = HGRN_CHUNK
    T = q_ref.shape[2]
    n_chunks = T // C
    sf_scr[...] = jnp.zeros_like(sf_scr)
    sb_scr[...] = jnp.zeros_like(sb_scr)
    nw = nw_ref[...]

    def load(ci, gate_refs):
        rows = pl.ds(pl.multiple_of(ci * C, C), C)
        lf_ref, k_ref = gate_refs
        lf = lf_ref[0, 0, rows, :]
        hi = lf.astype(BF16)
        lo = (lf - hi.astype(F32)).astype(BF16)
        return (rows, q_ref[0, 0, rows, :], k_ref[0, 0, rows, :], v_ref[0, 0, rows, :],
                (lf, hi, lo))

    def advance(s_scr, a, v, qi, kd, s_decay):
        st = s_scr[...]
        o = _dot(a.astype(BF16), v) + _dot_nt(qi, st.astype(BF16))
        s_scr[...] = st * s_decay + _dot_tn(v, kd)
        return o


    def chunk_rows(ci):
        return pl.ds(pl.multiple_of(ci * C, C), C)

    def prepare_task(c, ci, p, slot):
        start = C - 1 - p[4]

        def issue():
            _, q, k, _, (lf, hi, lo) = load(ci, p[0])
            return q, k, lf[start:start + 1], _dot(p[1][...], jnp.concatenate([hi, lo], axis=0))

        def consume(issued):
            q, k, lf_start, d = issued
            b_mid = lf_start - d[start:start + 1]
            d_end = d[p[4]:p[4] + 1]
            qh = q * jnp.exp(d).astype(BF16)
            kh = k * jnp.exp(-d).astype(BF16)
            ops_scr[slot, c, 0] = qh
            ops_scr[slot, c, 1] = kh
            ops_scr[slot, c, 2] = qh * jnp.exp(b_mid).astype(BF16)
            ops_scr[slot, c, 3] = kh * jnp.exp(d_end).astype(BF16)
            dec_scr[slot, c] = jnp.broadcast_to(jnp.exp(b_mid + d_end), (8, LANES))
        return issue, consume

    def scores_task(c, ci, p, ops_slot, slot):
        def issue():
            return (_dot_nt(ops_scr[ops_slot, c, 0], ops_scr[ops_slot, c, 1]),
                    _dot_tn(v_ref[0, 0, chunk_rows(ci), :], ops_scr[ops_slot, c, 3]))

        def consume(issued):
            a_scr[slot, c] = jnp.where(p[2][...] >= 0, issued[0], 0.0).astype(BF16)
            u_scr[slot, c] = issued[1]
        return issue, consume

    def finish_task(c, ci, p, ops_slot, slot, finish):
        rows = chunk_rows(ci)

        def issue():
            st = p[3][...]
            o = (_dot(a_scr[slot, c], v_ref[0, 0, rows, :])
                 + _dot_nt(ops_scr[ops_slot, c, 2], st.astype(BF16)))
            p[3][...] = st * dec_scr[ops_slot, c, 0:1, :] + u_scr[slot, c]
            return o

        def consume(o):
            finish(rows, o)
        return issue, consume

    def rolling(tasks, lag):
        pending = []
        for issue, consume in tasks:
            pending.append((consume, issue()))
            if len(pending) > lag:
                cons, res = pending.pop(0)
                cons(res)
        for cons, res in pending:
            cons(res)

    def safe_chunk(ci, lf_ref, tab_ref, lv_ref, s_scr, end):
        rows, qb, kb, v, (_, hi, lo) = load(ci, lf_ref)
        q = qb.astype(F32)
        k = kb.astype(F32)
        r = _dot(tab_ref[...], jnp.concatenate([hi, lo], axis=1))
        dm = r[:, :LANES] + r[:, LANES:]
        lv = lv_ref[...]
        a = jnp.where(lv == HGRN_LEVELS, _dot_nt(qb, kb), 0.0)
        for j in range(HGRN_LEVELS):
            e = jnp.exp(-jnp.abs(dm[j * C:(j + 1) * C]))
            aj = _dot_nt((q * e).astype(BF16), (k * e).astype(BF16))
            a = jnp.where(lv == j, aj, a)
        bq = dm[HGRN_LEVELS * C:(HGRN_LEVELS + 1) * C]
        bk = dm[(HGRN_LEVELS + 1) * C:(HGRN_LEVELS + 2) * C]
        qi = (q * jnp.exp(bq)).astype(BF16)
        kd = (k * jnp.exp(bk)).astype(BF16)
        return rows, advance(s_scr, a, v, qi, kd, jnp.exp(bq[end:end + 1, :]))

    def finalize(rows, o):
        tot = o_scr[rows, :] + o
        ms = jnp.mean(tot * tot, axis=-1, keepdims=True)
        y = tot * lax.rsqrt(ms + LN_EPS) * nw * g_ref[0, 0, rows, :].astype(F32)
        y_ref[0, 0, rows, :] = y.astype(BF16)

    def first_visit(rows, o):
        o_scr[rows, :] = o

    def chain_groups(tf_ref, tb_ref, per_step):
        fwd = ((lff_ref, kf_ref), tf_ref, lvf_ref, sf_scr, ends[0])
        bwd = ((lfb_ref, kb_ref), tb_ref, lvb_ref, sb_scr, ends[1])

        def chains(g):
            out = []
            for u in range(per_step):
                i = g * per_step + u
                out += [(i, fwd), (n_chunks - 1 - i, bwd)]
            return out
        return chains

    def loop(lo, hi, body, finish):
        def wrapped(g, carry):
            body(g, finish)
            return carry
        lax.fori_loop(lo, hi, wrapped, 0)

    def run_safe():
        chains = chain_groups(tabf_ref, tabb_ref, 1)

        def step(g, finish):
            for ci, p in chains(g):
                finish(*safe_chunk(ci, *p))

        loop(0, n_chunks // 2, step, first_visit)
        loop(n_chunks // 2, n_chunks, step, finalize)

    def run_fast():
        per_step = ops_scr.shape[1] // 2
        chains = chain_groups(ftabf_ref, ftabb_ref, per_step)
        n_groups = n_chunks // per_step
        last = n_groups - 1

        def step(g, finish):
            g1 = jnp.minimum(g + 1, last)
            g2 = jnp.minimum(g + 2, last)
            tasks = []
            for c, ((c1, p1), (c2, p2), (c0, p0)) in enumerate(zip(chains(g1), chains(g2), chains(g))):
                tasks.append(scores_task(c, c1, p1, g1 % 3, (g + 1) % 2))
                tasks.append(prepare_task(c, c2, p2, (g + 2) % 3))
                tasks.append(finish_task(c, c0, p0, g % 3, g % 2, finish))
            rolling(tasks, HGRN_ROLL_LAG)

        fill = [prepare_task(c, ci, p, g) for g in range(2) for c, (ci, p) in enumerate(chains(g))]
        fill += [scores_task(c, ci, p, 0, 0) for c, (ci, p) in enumerate(chains(0))]
        rolling(fill, min(HGRN_ROLL_LAG, 2 * per_step))
        loop(0, n_groups // 2, step, first_visit)
        loop(n_groups // 2, n_groups, step, finalize)

    is_head = lax.broadcasted_iota(jnp.int32, (A_HEADS, LANES), 0) == pl.program_id(1)
    dec = jnp.min(jnp.where(is_head, jnp.min(dstat_ref[0], axis=0), 0.0))
    mag = jnp.max(jnp.where(is_head, jnp.max(mstat_ref[0], axis=0), 0.0))
    is_safe = jnp.logical_and(dec > -HGRN_SAFE_DECAY, mag < HGRN_SAFE_MAG)

    pl.when(is_safe)(run_fast)
    pl.when(jnp.logical_not(is_safe))(run_safe)


def _hgrn_call(qa, lff, kf, lfb, kb, va, ga, dstat, mstat, nw):
    B, H, T, _ = qa.shape
    stat = pl.BlockSpec((1,) + dstat.shape[1:], lambda b, h: (b, 0, 0, 0))
    tabf, ftabf, lvf, endf = _hgrn_tables(False)
    tabb, ftabb, lvb, endb = _hgrn_tables(True)
    seq = pl.BlockSpec((1, 1, T, LANES), lambda b, h: (b, h, 0, 0))
    n_chunks = T // HGRN_CHUNK
    assert n_chunks % 4 == 0, T
    per_step = max(u for u in (HGRN_FAST_UNROLL, 2, 1) if n_chunks % (4 * u) == 0)
    return pl.pallas_call(
        functools.partial(_hgrn_kernel, ends=(endf, endb)),
        grid=(B, H),
        in_specs=[seq] * 7 + [stat, stat,
                  _const_spec((1, LANES)),
                  _const_spec(tabf.shape), _const_spec(tabb.shape),
                  _const_spec(ftabf.shape), _const_spec(ftabb.shape),
                  _const_spec(lvf.shape), _const_spec(lvb.shape)],
        out_specs=seq,
        out_shape=jax.ShapeDtypeStruct((B, H, T, LANES), BF16),
        scratch_shapes=[pltpu.VMEM((T, LANES), F32),
                        pltpu.VMEM((A_DV, A_DK), F32),
                        pltpu.VMEM((A_DV, A_DK), F32),
                        pltpu.VMEM((3, 2 * per_step, 4, HGRN_CHUNK, LANES), BF16),
                        pltpu.VMEM((3, 2 * per_step, 8, LANES), F32),
                        pltpu.VMEM((2, 2 * per_step, HGRN_CHUNK, HGRN_CHUNK), BF16),
                        pltpu.VMEM((2, 2 * per_step, A_DV, A_DK), F32)],
        compiler_params=pltpu.CompilerParams(
            dimension_semantics=("arbitrary", "arbitrary"), vmem_limit_bytes=VMEM_LIMIT),
        name="hgrn",
    )(qa, lff, kf, lfb, kb, va, ga, dstat, mstat, nw,
      jnp.asarray(tabf, BF16), jnp.asarray(tabb, BF16),
      jnp.asarray(ftabf, BF16), jnp.asarray(ftabb, BF16), jnp.asarray(lvf), jnp.asarray(lvb))


def _ret_kernel(q_ref, k_ref, v_ref, g_ref, lg_ref, y_ref,
                o_scr, s_scr, dec_scr, qd_scr, kd_scr, hm_scr, a_scr, u_scr):
    C = RET_CHUNK
    T = q_ref.shape[2]
    n_chunks = T // C
    per_step = a_scr.shape[1] // 4
    n_groups = n_chunks // per_step
    s_scr[...] = jnp.zeros_like(s_scr)

    lane = lax.broadcasted_iota(jnp.int32, (1, LANES), 1)
    head_mask = [(lane % (LANES // 2)) < (LANES // 4), (lane % (LANES // 2)) >= (LANES // 4)]
    tpos = lax.broadcasted_iota(jnp.int32, (C, C), 0)
    spos = lax.broadcasted_iota(jnp.int32, (C, C), 1)
    pos = lax.broadcasted_iota(jnp.int32, (C, 1), 0).astype(F32)
    s_dec = []
    for i in range(2):
        hm_scr[i] = jnp.broadcast_to(jnp.where(head_mask[i], 1.0, 0.0), (C, LANES)).astype(BF16)
    for d in range(2):
        dist = ((tpos - spos) if d == 0 else (spos - tpos)).astype(F32)
        after = pos if d == 0 else (C - 1.0 - pos)
        for i in range(2):
            lg = lg_ref[0, 2 * d + i:2 * d + i + 1, :]
            dec_scr[2 * d + i] = jnp.exp(jnp.where(dist >= 0, dist * lg, NEG_BIG))
            qd_scr[2 * d + i] = jnp.where(head_mask[i], jnp.exp((after + 1.0) * lg), 0.0).astype(BF16)
            kd_scr[2 * d + i] = jnp.where(head_mask[i], jnp.exp((C - 1.0 - after) * lg), 0.0).astype(BF16)
            s_dec.append(jnp.exp(C * lg))

    def chunk_rows(ci):
        return pl.ds(pl.multiple_of(ci * C, C), C)

    def units(g):
        out = []
        for u in range(per_step):
            j = g * per_step + u
            out += [(j, 0), (n_chunks - 1 - j, 1)]
        return out

    def scores_task(m, ci, d, slot):
        rows = chunk_rows(ci)

        def issue():
            q = q_ref[0, 0, rows, :]
            k = k_ref[0, 0, rows, :]
            qm = jnp.concatenate([q * hm_scr[0], q * hm_scr[1]], axis=0)
            km = k * (kd_scr[2 * d] + kd_scr[2 * d + 1])
            vv = jnp.concatenate([v_ref[0, 0, rows, :], v_ref[0, 1, rows, :]], axis=1)
            return _dot_nt(qm, k), _dot_tn(km, vv)

        def consume(issued):
            sc, upd = issued
            for i in range(2):
                a_scr[slot, 2 * m + i] = (sc[i * C:(i + 1) * C] * dec_scr[2 * d + i]).astype(BF16)
                u_scr[slot, 2 * m + i] = upd[:, i * R_DV:(i + 1) * R_DV]
        return issue, consume

    def finish_task(n, ci, d, i, slot, finish):
        rows = chunk_rows(ci)

        def issue():
            st = s_scr[2 * d + i]
            qd = q_ref[0, 0, rows, :] * qd_scr[2 * d + i]
            lhs = jnp.concatenate([a_scr[slot, n], qd], axis=1)
            rhs = jnp.concatenate([v_ref[0, i, rows, :], st.astype(BF16)], axis=0)
            s_scr[2 * d + i] = st * s_dec[2 * d + i] + u_scr[slot, n]
            return _dot(lhs, rhs)

        def consume(o):
            finish(i, rows, o)
        return issue, consume

    def rolling(tasks, lag):
        pending = []
        for issue, consume in tasks:
            pending.append((consume, issue()))
            if len(pending) > lag:
                cons, res = pending.pop(0)
                cons(res)
        for cons, res in pending:
            cons(res)

    def first_visit(i, rows, o):
        o_scr[i, rows, :] = o

    def finalize(i, rows, o):
        tot = o_scr[i, rows, :] + o
        mu = jnp.mean(tot, axis=-1, keepdims=True)
        var = jnp.mean(tot * tot, axis=-1, keepdims=True) - mu * mu
        y = (tot - mu) * lax.rsqrt(var + LN_EPS) * g_ref[0, i, rows, :].astype(F32)
        y_ref[0, i, rows, :] = y.astype(BF16)

    def step(g, finish):
        nxt = units(jnp.minimum(g + 1, n_groups - 1))
        cur = units(g)
        tasks = []
        for m, ((c1, d1), (c0, d0)) in enumerate(zip(nxt, cur)):
            tasks.append(scores_task(m, c1, d1, (g + 1) % 2))
            for i in range(2):
                tasks.append(finish_task(2 * m + i, c0, d0, i, g % 2, finish))
        rolling(tasks, RET_ROLL_LAG)

    def loop(lo, hi, finish):
        def wrapped(g, carry):
            step(g, finish)
            return carry
        lax.fori_loop(lo, hi, wrapped, 0)

    rolling([scores_task(m, ci, d, 0) for m, (ci, d) in enumerate(units(0))], RET_ROLL_LAG)
    loop(0, n_groups // 2, first_visit)
    loop(n_groups // 2, n_groups, finalize)


def _ret_call(qr, kr, vr, gr, lg_rows):
    B, P, T, _ = qr.shape
    n_chunks = T // RET_CHUNK
    assert n_chunks % 2 == 0, T
    per_step = max(u for u in (RET_STEP_CHUNKS, 1) if n_chunks % (2 * u) == 0)
    n_units = 4 * per_step
    seq1 = pl.BlockSpec((1, 1, T, LANES), lambda b, p: (b, p, 0, 0))
    seq2 = pl.BlockSpec((1, 2, T, LANES), lambda b, p: (b, p, 0, 0))
    return pl.pallas_call(
        _ret_kernel,
        grid=(B, P),
        in_specs=[seq1, seq1, seq2, seq2,
                  pl.BlockSpec((1, 4, LANES), lambda b, p: (p, 0, 0))],
        out_specs=seq2,
        out_shape=jax.ShapeDtypeStruct((B, R_HEADS, T, LANES), BF16),
        scratch_shapes=[pltpu.VMEM((2, T, LANES), F32),
                        pltpu.VMEM((4, LANES, R_DV), F32),
                        pltpu.VMEM((4, RET_CHUNK, RET_CHUNK), F32),
                        pltpu.VMEM((4, RET_CHUNK, LANES), BF16),
                        pltpu.VMEM((4, RET_CHUNK, LANES), BF16),
                        pltpu.VMEM((2, RET_CHUNK, LANES), BF16),
                        pltpu.VMEM((2, n_units, RET_CHUNK, RET_CHUNK), BF16),
                        pltpu.VMEM((2, n_units, LANES, R_DV), F32)],
        compiler_params=pltpu.CompilerParams(
            dimension_semantics=("arbitrary", "arbitrary"), vmem_limit_bytes=VMEM_LIMIT),
        name="retention",
    )(qr, kr, vr, gr, lg_rows)


def _outproj_kernel(ya_ref, yr_ref, mg_ref, x_ref, gate_ref, wpa_ref, wpb_ref, wout_ref,
                    lng_ref, lnb_ref, o_ref, m_scr):
    rows = x_ref.shape[1]
    sub = min(OUT_SUB_ROWS, rows)

    def branch_task(r):
        rs = slice(r * sub, (r + 1) * sub)

        def issue():
            ya = jnp.concatenate([ya_ref[0, h, rs, :] for h in range(A_HEADS)], axis=1)
            yr = jnp.concatenate([yr_ref[0, h, rs, :] for h in range(R_HEADS)], axis=1)
            return _dot(ya, wpa_ref[...]), _dot(yr, wpb_ref[...])

        def consume(issued):
            pa, pr = issued
            mg = mg_ref[0, rs, :]
            m = mg[:, :D_MODEL].astype(F32) * pa + mg[:, D_MODEL:].astype(F32) * pr
            m_scr[rs, :] = m.astype(BF16)
        return issue, consume

    def out_task(r):
        rs = slice(r * sub, (r + 1) * sub)

        def issue():
            return _dot(m_scr[rs, :], wout_ref[...])

        def consume(s):
            h = DEEPNORM_ALPHA * x_ref[0, rs, :] + gate_ref[0] * s
            mu = jnp.mean(h, axis=-1, keepdims=True)
            cen = h - mu
            var = jnp.mean(cen * cen, axis=-1, keepdims=True)
            o_ref[0, rs, :] = cen * lax.rsqrt(var + LN_EPS) * lng_ref[...] + lnb_ref[...]
        return issue, consume

    n_sub = rows // sub
    tasks = [branch_task(0)]
    for r in range(n_sub):
        if r + 1 < n_sub:
            tasks.append(branch_task(r + 1))
        tasks.append(out_task(r))
    lag = 1 if n_sub > 1 else 0
    pending = []
    for issue, consume in tasks:
        pending.append((consume, issue()))
        if len(pending) > lag:
            cons, res = pending.pop(0)
            cons(res)
    for cons, res in pending:
        cons(res)


def _outproj_call(layer, ya, yr, mg, x, gate1p, w_pa, w_pb, w_out, ln_g, ln_b):
    B, T, D = x.shape
    rows = min(OUT_ROWS, T)
    hspec = pl.BlockSpec((1, A_HEADS, rows, LANES), lambda b, t: (b, 0, t, 0))
    return pl.pallas_call(
        _outproj_kernel,
        grid=(B, T // rows),
        in_specs=[hspec, hspec,
                  pl.BlockSpec((1, rows, 2 * D), lambda b, t: (b, t, 0)),
                  pl.BlockSpec((1, rows, D), lambda b, t: (b, t, 0)),
                  pl.BlockSpec((1, 1, D), lambda b, t: (b, 0, 0)),
                  _layer_spec((D, D), layer), _layer_spec((D, D), layer), _layer_spec((D, D), layer),
                  _const_spec((1, D)), _const_spec((1, D))],
        out_specs=pl.BlockSpec((1, rows, D), lambda b, t: (b, t, 0)),
        out_shape=jax.ShapeDtypeStruct((B, T, D), F32),
        scratch_shapes=[pltpu.VMEM((rows, D), BF16)],
        compiler_params=pltpu.CompilerParams(
            dimension_semantics=("arbitrary", "arbitrary"), vmem_limit_bytes=VMEM_LIMIT),
        name="outproj",
    )(ya, yr, mg, x, gate1p, w_pa, w_pb, w_out, ln_g, ln_b)


def _rope_perm():
    half = R_DK // 2
    perm = []
    for p in range(R_PAIRS):
        for part in range(2):
            for i in range(2):
                h = 2 * p + i
                perm.extend(h * R_DK + part * half + np.arange(half))
    return np.asarray(perm, np.int32)


def _rope_tables(T):
    half = R_DK // 2
    pos = jnp.arange(T, dtype=F32)
    inv = ROPE_BASE ** (-jnp.arange(0, R_DK, 2, dtype=F32) / R_DK)
    ang = pos[:, None] * inv[None, :]
    cos, sin = jnp.cos(ang), jnp.sin(ang)
    return jnp.tile(cos, (1, LANES // half)), jnp.concatenate([-sin, -sin, sin, sin], axis=1)


def _trunk(x, ada, wts, params):
    B, T, D = x.shape
    cos_p, sin_p = _rope_tables(T)
    for l in range(DEPTH):
        p = params[l]
        shift = ada[l, :, :D].reshape(B, 1, D)
        sc1p = (1.0 + ada[l, :, D:2 * D]).reshape(B, 1, D)
        gate1p = (1.0 + ada[l, :, 2 * D:]).reshape(B, 1, D)
        qa, lff, kf, lfb, kb, va, ga, qr, kr, vr, gr, mg, dstat, mstat = _inproj_call(
            l, x, sc1p, shift, wts["w_in"], wts["w_rope"], wts["w_mg"], p["b_mg"], p["lbm"],
            p["oml"], cos_p, sin_p)
        ya = _hgrn_call(qa, lff, kf, lfb, kb, va, ga, dstat, mstat, p["nw"])
        yr = _ret_call(qr, kr, vr, gr, p["lg_rows"])
        x = _outproj_call(l, ya, yr, mg, x, gate1p, wts["w_pa"], wts["w_pb"], wts["w_out"],
                          p["ln_g"], p["ln_b"])
    return x


def kernel(x_prompt, x_sample, c_prompt, c_sample, w_ada, b_ada, w_in, hgrn_lb, a_norm_w, ret_decay,
           w_pa, w_pb, w_mg, b_mg, w_out, ln_g, ln_b):
    D = D_MODEL
    pr = jax.nn.softmax(hgrn_lb.astype(F32), axis=0)
    lbs = jnp.cumsum(pr, axis=0) - pr[0:1]
    perm = _rope_perm()
    rope_cols = np.concatenate([OFF_RQ + perm, OFF_RK + perm])
    wts = dict(w_in=w_in.astype(BF16), w_rope=w_in[:, :, rope_cols].astype(BF16),
               w_mg=w_mg.astype(BF16), w_pa=w_pa.astype(BF16), w_pb=w_pb.astype(BF16),
               w_out=w_out.astype(BF16))
    params = []
    for l in range(DEPTH):
        lb = lbs[l]
        lbm = jnp.maximum(lb, LOG_FLOOR)
        lg = jax.nn.log_sigmoid(ret_decay[l].astype(F32))
        lg_rows = jnp.broadcast_to(
            lg.reshape(2, R_PAIRS, 2).transpose(1, 0, 2).reshape(R_PAIRS, 4, 1), (R_PAIRS, 4, LANES))
        params.append(dict(
            b_mg=b_mg[l].reshape(1, 2 * D), lbm=lbm.reshape(1, D), oml=(1.0 - lb).reshape(1, D),
            nw=a_norm_w[l].reshape(1, A_DV), lg_rows=lg_rows,
            ln_g=ln_g[l].reshape(1, D), ln_b=ln_b[l].reshape(1, D)))
    nbp = x_prompt.shape[0]
    ada = _ada_call(jnp.concatenate([c_prompt, c_sample], axis=0), w_ada, b_ada)
    y_prompt = _trunk(x_prompt, ada[:, :nbp], wts, params)
    y_sample = _trunk(x_sample, ada[:, nbp:], wts, params)
    return (y_prompt, y_sample)
```

```python
import functools

import numpy as np
import jax
import jax.numpy as jnp
from jax import lax
from jax.experimental import pallas as pl
from jax.experimental.pallas import tpu as pltpu

F32 = jnp.float32
BF16 = jnp.bfloat16

D_MODEL = 1024
DEPTH = 2
A_HEADS = 8
A_DK = 128
A_DV = 128
R_HEADS = 8
R_DK = 64
R_DV = 128
R_PAIRS = R_HEADS // 2
LN_EPS = 1e-5
LOG_FLOOR = 1e-30
NEG_BIG = -1e30
DEEPNORM_ALPHA = (2.0 * DEPTH) ** 0.25
ROPE_BASE = 10000.0

OFF_AQ, OFF_FF, OFF_FB, OFF_AI, OFF_AG, OFF_RQ, OFF_RK, OFF_RV, OFF_RG = (
    0, 1024, 2048, 3072, 4096, 5120, 5632, 6144, 7168)
IN_WIDTH = 8192

LANES = 128
HGRN_CHUNK = 64
HGRN_LEVELS = 6
RET_CHUNK = 128
HGRN_SAFE_DECAY = 75.0
HGRN_SAFE_MAG = 1e5
HGRN_FAST_UNROLL = 8
IN_ROLL_LAG = 1
HGRN_ROLL_LAG = 12
RET_STEP_CHUNKS = 4
RET_ROLL_LAG = 4
IN_ROWS = 256
OUT_ROWS = 1024
OUT_SUB_ROWS = 256
VMEM_LIMIT = 56 * 1024 * 1024


def _dot(a, b):
    return jnp.dot(a, b, preferred_element_type=F32)


def _dot_nt(a, b):
    return lax.dot_general(a, b, (((1,), (1,)), ((), ())), preferred_element_type=F32)


def _dot_tn(a, b):
    return lax.dot_general(a, b, (((0,), (0,)), ((), ())), preferred_element_type=F32)


def _sigmoid(z):
    return 1.0 / (1.0 + jnp.exp(-z))


def _const_spec(shape):
    nd = len(shape)
    return pl.BlockSpec(shape, lambda *_: (0,) * nd, pipeline_mode=pl.Buffered(1))


def _ada_kernel(c_ref, w_ref, b_ref, o_ref):
    c = c_ref[...]
    a = c * _sigmoid(c)
    o_ref[0] = jnp.dot(a, w_ref[0], preferred_element_type=F32,
                       precision=lax.Precision.HIGHEST) + b_ref[0]


def _ada_call(c, w_ada, b_ada):
    nb = c.shape[0]
    d = D_MODEL
    return pl.pallas_call(
        _ada_kernel,
        grid=(DEPTH, 3),
        in_specs=[
            pl.BlockSpec((nb, d), lambda l, j: (0, 0)),
            pl.BlockSpec((1, d, d), lambda l, j: (l, 0, j)),
            pl.BlockSpec((1, 1, d), lambda l, j: (l, 0, j)),
        ],
        out_specs=pl.BlockSpec((1, nb, d), lambda l, j: (l, 0, j)),
        out_shape=jax.ShapeDtypeStruct((DEPTH, nb, 3 * d), F32),
        compiler_params=pltpu.CompilerParams(
            dimension_semantics=("arbitrary", "arbitrary"), vmem_limit_bytes=VMEM_LIMIT),
        name="adaln",
    )(c, w_ada, b_ada.reshape(DEPTH, 1, 3 * d))


def _inproj_kernel(x_ref, sc_ref, sh_ref, win_ref, wrope_ref, wmg_ref, bmg_ref, lbm_ref, oml_ref,
                   cos_ref, sin_ref,
                   qa_ref, lff_ref, kf_ref, lfb_ref, kb_ref, va_ref, ga_ref,
                   qr_ref, kr_ref, vr_ref, gr_ref, mg_ref, dstat_ref, mstat_ref):
    rows = x_ref.shape[1]
    u = (x_ref[0] * sc_ref[0] + sh_ref[0]).astype(BF16)

    def col_max(z):
        return jnp.max(jnp.abs(z), axis=0, keepdims=True)

    def store_heads(ref, g, z):
        for i in range(2):
            ref[0, 2 * g + i] = z[:, i * LANES:(i + 1) * LANES].astype(BF16)

    tasks = []
    stats = {}

    def q_epilogue(g, z):
        z = z * _sigmoid(z) * (A_DK ** -0.5)
        stats[g] = [col_max(z), None]
        store_heads(qa_ref, g, z)

    def forget_epilogue(g, lf_ref, k_ref, z):
        lbm = lbm_ref[:, 2 * LANES * g:2 * LANES * (g + 1)]
        oml = oml_ref[:, 2 * LANES * g:2 * LANES * (g + 1)]
        f = lbm + oml * _sigmoid(z)
        lf = jnp.log(f)
        halfsum = jnp.sum(lf.reshape(2 * rows // HGRN_CHUNK, HGRN_CHUNK // 2, 2 * LANES), axis=1)
        halfmin = jnp.min(halfsum, axis=0, keepdims=True)
        for i in range(2):
            lf_ref[0, 2 * g + i] = lf[:, i * LANES:(i + 1) * LANES]
        store_heads(k_ref, g, (oml + lbm) - f)
        stats[g][1] = halfmin if stats[g][1] is None else jnp.minimum(stats[g][1], halfmin)

    def v_epilogue(g, z):
        mag, dmin = stats.pop(g)
        mag = jnp.maximum(mag, col_max(z))
        store_heads(va_ref, g, z)
        for i in range(2):
            lanes = slice(i * LANES, (i + 1) * LANES)
            dstat_ref[0, 0, 2 * g + i:2 * g + i + 1, :] = dmin[:, lanes]
            mstat_ref[0, 0, 2 * g + i:2 * g + i + 1, :] = mag[:, lanes]

    def silu_epilogue(ref, g, z):
        store_heads(ref, g, z * _sigmoid(z))

    def rope_epilogue(ref, g, scale, z):
        for i in range(2):
            zz = z[:, i * LANES:(i + 1) * LANES]
            rot = pltpu.roll(zz, LANES // 2, 1)
            ref[0, 2 * g + i] = ((zz * cos_ref[...] + rot * sin_ref[...]) * scale).astype(BF16)

    def merge_epilogue(g, z):
        c = 2 * LANES * g
        mg_ref[0, :, c:c + 2 * LANES] = _sigmoid(z + bmg_ref[:, c:c + 2 * LANES]).astype(BF16)

    part = functools.partial
    for g in range(A_HEADS // 2):
        c = 2 * LANES * g
        tasks += [(win_ref, OFF_AQ + c, part(q_epilogue, g)),
                  (win_ref, OFF_FF + c, part(forget_epilogue, g, lff_ref, kf_ref)),
                  (win_ref, OFF_RV + c, part(store_heads, vr_ref, g)),
                  (win_ref, OFF_FB + c, part(forget_epilogue, g, lfb_ref, kb_ref)),
                  (win_ref, OFF_AI + c, part(v_epilogue, g)),
                  (win_ref, OFF_AG + c, part(silu_epilogue, ga_ref, g)),
                  (win_ref, OFF_RG + c, part(silu_epilogue, gr_ref, g))]
    for g in range(R_PAIRS // 2):
        c = 2 * LANES * g
        tasks += [(wrope_ref, c, part(rope_epilogue, qr_ref, g, R_DK ** -0.5)),
                  (wrope_ref, R_HEADS * R_DK + c, part(rope_epilogue, kr_ref, g, 1.0))]
    for g in range(2 * D_MODEL // (2 * LANES)):
        tasks.append((wmg_ref, 2 * LANES * g, part(merge_epilogue, g)))

    pending = []
    for w_ref, c0, epilogue in tasks:
        pending.append((epilogue, _dot(u, w_ref[:, c0:c0 + 2 * LANES])))
        if len(pending) > IN_ROLL_LAG:
            ep, z = pending.pop(0)
            ep(z)
    for ep, z in pending:
        ep(z)
    assert not stats


def _inproj_call(x, sc1p, shift, w_in, w_rope, w_mg, b_mg, lbm, oml, cos_p, sin_p):
    B, T, D = x.shape
    rows = min(IN_ROWS, T)
    grid = (B, T // rows)
    hd = lambda n, dt: jax.ShapeDtypeStruct((B, n, T, LANES), dt)
    hspec = lambda n: pl.BlockSpec((1, n, rows, LANES), lambda b, t: (b, 0, t, 0))
    bvec = pl.BlockSpec((1, 1, D), lambda b, t: (b, 0, 0))
    stat_spec = pl.BlockSpec((1, 1, A_HEADS, LANES), lambda b, t: (b, t, 0, 0))
    stat_shape = jax.ShapeDtypeStruct((B, T // rows, A_HEADS, LANES), F32)
    return pl.pallas_call(
        _inproj_kernel,
        grid=grid,
        in_specs=[
            pl.BlockSpec((1, rows, D), lambda b, t: (b, t, 0)),
            bvec, bvec,
            _const_spec((D, IN_WIDTH)),
            _const_spec((D, 2 * R_HEADS * R_DK)),
            _const_spec((D, 2 * D)),
            _const_spec((1, 2 * D)),
            _const_spec((1, D)),
            _const_spec((1, D)),
            pl.BlockSpec((rows, LANES), lambda b, t: (t, 0)),
            pl.BlockSpec((rows, LANES), lambda b, t: (t, 0)),
        ],
        out_specs=[hspec(A_HEADS)] * 7 + [
            hspec(R_PAIRS), hspec(R_PAIRS), hspec(R_HEADS), hspec(R_HEADS),
            pl.BlockSpec((1, rows, 2 * D), lambda b, t: (b, t, 0)), stat_spec, stat_spec],
        out_shape=[hd(A_HEADS, BF16), hd(A_HEADS, F32), hd(A_HEADS, BF16), hd(A_HEADS, F32),
                   hd(A_HEADS, BF16), hd(A_HEADS, BF16), hd(A_HEADS, BF16),
            hd(R_PAIRS, BF16), hd(R_PAIRS, BF16), hd(R_HEADS, BF16), hd(R_HEADS, BF16),
            jax.ShapeDtypeStruct((B, T, 2 * D), BF16), stat_shape, stat_shape],
        compiler_params=pltpu.CompilerParams(
            dimension_semantics=("arbitrary", "arbitrary"), vmem_limit_bytes=VMEM_LIMIT),
        name="inproj",
    )(x, sc1p, shift, w_in, w_rope, w_mg, b_mg, lbm, oml, cos_p, sin_p)


def _hgrn_tables(reverse):
    C = HGRN_CHUNK
    idx = np.arange(C)
    cum = (idx[None, :] >= idx[:, None]) if reverse else (idx[None, :] <= idx[:, None])
    cum = cum.astype(np.float64)
    end = 0 if reverse else C - 1
    blocks = []
    for j in range(HGRN_LEVELS):
        m = C >> (j + 1)
        base = idx - idx % (2 * m)
        ref = base + (m if reverse else m - 1)
        blocks.append(cum - cum[ref])
    blocks.append(cum)
    blocks.append(cum[end][None, :] - cum)
    safe_tab = np.concatenate(blocks, axis=0)
    mid = C // 2 if reverse else C // 2 - 1
    fast_tab = np.tile(cum - cum[mid][None, :], (1, 2))
    x = idx[:, None] ^ idx[None, :]
    bl = np.zeros_like(x)
    nz = x > 0
    bl[nz] = np.floor(np.log2(x[nz])).astype(np.int64) + 1
    lvl = HGRN_LEVELS - bl
    coupled = (idx[:, None] <= idx[None, :]) if reverse else (idx[:, None] >= idx[None, :])
    lvl = np.where(coupled, lvl, -1)
    return safe_tab.astype(np.float32), fast_tab.astype(np.float32), lvl.astype(np.int32), end


def _hgrn_kernel(q_ref, lff_ref, kf_ref, lfb_ref, kb_ref, v_ref, g_ref,
                 dstat_ref, mstat_ref, nw_ref,
                 tabf_ref, tabb_ref, ftabf_ref, ftabb_ref, lvf_ref, lvb_ref,
                 y_ref, o_scr, sf_scr, sb_scr, ops_scr, dec_scr, a_scr, u_scr, *, ends):
    C = HGRN_CHUNK
    T = q_ref.shape[2]
    n_chunks = T // C
    sf_scr[...] = jnp.zeros_like(sf_scr)
    sb_scr[...] = jnp.zeros_like(sb_scr)
    nw = nw_ref[...]

    def load(ci, gate_refs):
        rows = pl.ds(pl.multiple_of(ci * C, C), C)
        lf_ref, k_ref = gate_refs
        lf = lf_ref[0, 0, rows, :]
        hi = lf.astype(BF16)
        lo = (lf - hi.astype(F32)).astype(BF16)
        return (rows, q_ref[0, 0, rows, :], k_ref[0, 0, rows, :], v_ref[0, 0, rows, :],
                (lf, hi, lo))

    def advance(s_scr, a, v, qi, kd, s_decay):
        st = s_scr[...]
        o = _dot(a.astype(BF16), v) + _dot_nt(qi, st.astype(BF16))
        s_scr[...] = st * s_decay + _dot_tn(v, kd)
        return o


    def chunk_rows(ci):
        return pl.ds(pl.multiple_of(ci * C, C), C)

    def prepare_task(c, ci, p, slot):
        start = C - 1 - p[4]

        def issue():
            _, q, k, _, (lf, hi, lo) = load(ci, p[0])
            return q, k, lf[start:start + 1], _dot(p[1][...], jnp.concatenate([hi, lo], axis=0))

        def consume(issued):
            q, k, lf_start, d = issued
            b_mid = lf_start - d[start:start + 1]
            d_end = d[p[4]:p[4] + 1]
            qh = q * jnp.exp(d).astype(BF16)
            kh = k * jnp.exp(-d).astype(BF16)
            ops_scr[slot, c, 0] = qh
            ops_scr[slot, c, 1] = kh
            ops_scr[slot, c, 2] = qh * jnp.exp(b_mid).astype(BF16)
            ops_scr[slot, c, 3] = kh * jnp.exp(d_end).astype(BF16)
            dec_scr[slot, c] = jnp.broadcast_to(jnp.exp(b_mid + d_end), (8, LANES))
        return issue, consume

    def scores_task(c, ci, p, ops_slot, slot):
        def issue():
            return (_dot_nt(ops_scr[ops_slot, c, 0], ops_scr[ops_slot, c, 1]),
                    _dot_tn(v_ref[0, 0, chunk_rows(ci), :], ops_scr[ops_slot, c, 3]))

        def consume(issued):
            a_scr[slot, c] = jnp.where(p[2][...] >= 0, issued[0], 0.0).astype(BF16)
            u_scr[slot, c] = issued[1]
        return issue, consume

    def finish_task(c, ci, p, ops_slot, slot, finish):
        rows = chunk_rows(ci)

        def issue():
            st = p[3][...]
            o = (_dot(a_scr[slot, c], v_ref[0, 0, rows, :])
                 + _dot_nt(ops_scr[ops_slot, c, 2], st.astype(BF16)))
            p[3][...] = st * dec_scr[ops_slot, c, 0:1, :] + u_scr[slot, c]
            return o

        def consume(o):
            finish(rows, o)
        return issue, consume

    def rolling(tasks, lag):
        pending = []
        for issue, consume in tasks:
            pending.append((consume, issue()))
            if len(pending) > lag:
                cons, res = pending.pop(0)
                cons(res)
        for cons, res in pending:
            cons(res)

    def safe_chunk(ci, lf_ref, tab_ref, lv_ref, s_scr, end):
        rows, qb, kb, v, (_, hi, lo) = load(ci, lf_ref)
        q = qb.astype(F32)
        k = kb.astype(F32)
        r = _dot(tab_ref[...], jnp.concatenate([hi, lo], axis=1))
        dm = r[:, :LANES] + r[:, LANES:]
        lv = lv_ref[...]
        a = jnp.where(lv == HGRN_LEVELS, _dot_nt(qb, kb), 0.0)
        for j in range(HGRN_LEVELS):
            e = jnp.exp(-jnp.abs(dm[j * C:(j + 1) * C]))
            aj = _dot_nt((q * e).astype(BF16), (k * e).astype(BF16))
            a = jnp.where(lv == j, aj, a)
        bq = dm[HGRN_LEVELS * C:(HGRN_LEVELS + 1) * C]
        bk = dm[(HGRN_LEVELS + 1) * C:(HGRN_LEVELS + 2) * C]
        qi = (q * jnp.exp(bq)).astype(BF16)
        kd = (k * jnp.exp(bk)).astype(BF16)
        return rows, advance(s_scr, a, v, qi, kd, jnp.exp(bq[end:end + 1, :]))

    def finalize(rows, o):
        tot = o_scr[rows, :] + o
        ms = jnp.mean(tot * tot, axis=-1, keepdims=True)
        y = tot * lax.rsqrt(ms + LN_EPS) * nw * g_ref[0, 0, rows, :].astype(F32)
        y_ref[0, 0, rows, :] = y.astype(BF16)

    def first_visit(rows, o):
        o_scr[rows, :] = o

    def chain_groups(tf_ref, tb_ref, per_step):
        fwd = ((lff_ref, kf_ref), tf_ref, lvf_ref, sf_scr, ends[0])
        bwd = ((lfb_ref, kb_ref), tb_ref, lvb_ref, sb_scr, ends[1])

        def chains(g):
            out = []
            for u in range(per_step):
                i = g * per_step + u
                out += [(i, fwd), (n_chunks - 1 - i, bwd)]
            return out
        return chains

    def loop(lo, hi, body, finish):
        def wrapped(g, carry):
            body(g, finish)
            return carry
        lax.fori_loop(lo, hi, wrapped, 0)

    def run_safe():
        chains = chain_groups(tabf_ref, tabb_ref, 1)

        def step(g, finish):
            for ci, p in chains(g):
                finish(*safe_chunk(ci, *p))

        loop(0, n_chunks // 2, step, first_visit)
        loop(n_chunks // 2, n_chunks, step, finalize)

    def run_fast():
        per_step = ops_scr.shape[1] // 2
        chains = chain_groups(ftabf_ref, ftabb_ref, per_step)
        n_groups = n_chunks // per_step
        last = n_groups - 1

        def step(g, finish):
            g1 = jnp.minimum(g + 1, last)
            g2 = jnp.minimum(g + 2, last)
            tasks = []
            for c, ((c1, p1), (c2, p2), (c0, p0)) in enumerate(zip(chains(g1), chains(g2), chains(g))):
                tasks.append(scores_task(c, c1, p1, g1 % 3, (g + 1) % 2))
                tasks.append(prepare_task(c, c2, p2, (g + 2) % 3))
                tasks.append(finish_task(c, c0, p0, g % 3, g % 2, finish))
            rolling(tasks, HGRN_ROLL_LAG)

        fill = [prepare_task(c, ci, p, g) for g in range(2) for c, (ci, p) in enumerate(chains(g))]
        fill += [scores_task(c, ci, p, 0, 0) for c, (ci, p) in enumerate(chains(0))]
        rolling(fill, min(HGRN_ROLL_LAG, 2 * per_step))
        loop(0, n_groups // 2, step, first_visit)
        loop(n_groups // 2, n_groups, step, finalize)

    is_head = lax.broadcasted_iota(jnp.int32, (A_HEADS, LANES), 0) == pl.program_id(1)
    dec = jnp.min(jnp.where(is_head, jnp.min(dstat_ref[0], axis=0), 0.0))
    mag = jnp.max(jnp.where(is_head, jnp.max(mstat_ref[0], axis=0), 0.0))
    is_safe = jnp.logical_and(dec > -HGRN_SAFE_DECAY, mag < HGRN_SAFE_MAG)

    pl.when(is_safe)(run_fast)
    pl.when(jnp.logical_not(is_safe))(run_safe)


def _hgrn_call(qa, lff, kf, lfb, kb, va, ga, dstat, mstat, nw):
    B, H, T, _ = qa.shape
    stat = pl.BlockSpec((1,) + dstat.shape[1:], lambda b, h: (b, 0, 0, 0))
    tabf, ftabf, lvf, endf = _hgrn_tables(False)
    tabb, ftabb, lvb, endb = _hgrn_tables(True)
    seq = pl.BlockSpec((1, 1, T, LANES), lambda b, h: (b, h, 0, 0))
    n_chunks = T // HGRN_CHUNK
    assert n_chunks % 4 == 0, T
    per_step = max(u for u in (HGRN_FAST_UNROLL, 2, 1) if n_chunks % (4 * u) == 0)
    return pl.pallas_call(
        functools.partial(_hgrn_kernel, ends=(endf, endb)),
        grid=(B, H),
        in_specs=[seq] * 7 + [stat, stat,
                  _const_spec((1, LANES)),
                  _const_spec(tabf.shape), _const_spec(tabb.shape),
                  _const_spec(ftabf.shape), _const_spec(ftabb.shape),
                  _const_spec(lvf.shape), _const_spec(lvb.shape)],
        out_specs=seq,
        out_shape=jax.ShapeDtypeStruct((B, H, T, LANES), BF16),
        scratch_shapes=[pltpu.VMEM((T, LANES), F32),
                        pltpu.VMEM((A_DV, A_DK), F32),
                        pltpu.VMEM((A_DV, A_DK), F32),
                        pltpu.VMEM((3, 2 * per_step, 4, HGRN_CHUNK, LANES), BF16),
                        pltpu.VMEM((3, 2 * per_step, 8, LANES), F32),
                        pltpu.VMEM((2, 2 * per_step, HGRN_CHUNK, HGRN_CHUNK), BF16),
                        pltpu.VMEM((2, 2 * per_step, A_DV, A_DK), F32)],
        compiler_params=pltpu.CompilerParams(
            dimension_semantics=("arbitrary", "arbitrary"), vmem_limit_bytes=VMEM_LIMIT),
        name="hgrn",
    )(qa, lff, kf, lfb, kb, va, ga, dstat, mstat, nw,
      jnp.asarray(tabf, BF16), jnp.asarray(tabb, BF16),
      jnp.asarray(ftabf, BF16), jnp.asarray(ftabb, BF16), jnp.asarray(lvf), jnp.asarray(lvb))


def _ret_kernel(q_ref, k_ref, v_ref, g_ref, lg_ref, y_ref,
                o_scr, s_scr, dec_scr, qd_scr, kd_scr, hm_scr, a_scr, u_scr):
    C = RET_CHUNK
    T = q_ref.shape[2]
    n_chunks = T // C
    per_step = a_scr.shape[1] // 4
    n_groups = n_chunks // per_step
    s_scr[...] = jnp.zeros_like(s_scr)

    lane = lax.broadcasted_iota(jnp.int32, (1, LANES), 1)
    head_mask = [(lane % (LANES // 2)) < (LANES // 4), (lane % (LANES // 2)) >= (LANES // 4)]
    tpos = lax.broadcasted_iota(jnp.int32, (C, C), 0)
    spos = lax.broadcasted_iota(jnp.int32, (C, C), 1)
    pos = lax.broadcasted_iota(jnp.int32, (C, 1), 0).astype(F32)
    s_dec = []
    for i in range(2):
        hm_scr[i] = jnp.broadcast_to(jnp.where(head_mask[i], 1.0, 0.0), (C, LANES)).astype(BF16)
    for d in range(2):
        dist = ((tpos - spos) if d == 0 else (spos - tpos)).astype(F32)
        after = pos if d == 0 else (C - 1.0 - pos)
        for i in range(2):
            lg = lg_ref[0, 2 * d + i:2 * d + i + 1, :]
            dec_scr[2 * d + i] = jnp.exp(jnp.where(dist >= 0, dist * lg, NEG_BIG))
            qd_scr[2 * d + i] = jnp.where(head_mask[i], jnp.exp((after + 1.0) * lg), 0.0).astype(BF16)
            kd_scr[2 * d + i] = jnp.where(head_mask[i], jnp.exp((C - 1.0 - after) * lg), 0.0).astype(BF16)
            s_dec.append(jnp.exp(C * lg))

    def chunk_rows(ci):
        return pl.ds(pl.multiple_of(ci * C, C), C)

    def units(g):
        out = []
        for u in range(per_step):
            j = g * per_step + u
            out += [(j, 0), (n_chunks - 1 - j, 1)]
        return out

    def scores_task(m, ci, d, slot):
        rows = chunk_rows(ci)

        def issue():
            q = q_ref[0, 0, rows, :]
            k = k_ref[0, 0, rows, :]
            qm = jnp.concatenate([q * hm_scr[0], q * hm_scr[1]], axis=0)
            km = k * (kd_scr[2 * d] + kd_scr[2 * d + 1])
            vv = jnp.concatenate([v_ref[0, 0, rows, :], v_ref[0, 1, rows, :]], axis=1)
            return _dot_nt(qm, k), _dot_tn(km, vv)

        def consume(issued):
            sc, upd = issued
            for i in range(2):
                a_scr[slot, 2 * m + i] = (sc[i * C:(i + 1) * C] * dec_scr[2 * d + i]).astype(BF16)
                u_scr[slot, 2 * m + i] = upd[:, i * R_DV:(i + 1) * R_DV]
        return issue, consume

    def finish_task(n, ci, d, i, slot, finish):
        rows = chunk_rows(ci)

        def issue():
            st = s_scr[2 * d + i]
            qd = q_ref[0, 0, rows, :] * qd_scr[2 * d + i]
            lhs = jnp.concatenate([a_scr[slot, n], qd], axis=1)
            rhs = jnp.concatenate([v_ref[0, i, rows, :], st.astype(BF16)], axis=0)
            s_scr[2 * d + i] = st * s_dec[2 * d + i] + u_scr[slot, n]
            return _dot(lhs, rhs)

        def consume(o):
            finish(i, rows, o)
        return issue, consume

    def rolling(tasks, lag):
        pending = []
        for issue, consume in tasks:
            pending.append((consume, issue()))
            if len(pending) > lag:
                cons, res = pending.pop(0)
                cons(res)
        for cons, res in pending:
            cons(res)

    def first_visit(i, rows, o):
        o_scr[i, rows, :] = o

    def finalize(i, rows, o):
        tot = o_scr[i, rows, :] + o
        mu = jnp.mean(tot, axis=-1, keepdims=True)
        var = jnp.mean(tot * tot, axis=-1, keepdims=True) - mu * mu
        y = (tot - mu) * lax.rsqrt(var + LN_EPS) * g_ref[0, i, rows, :].astype(F32)
        y_ref[0, i, rows, :] = y.astype(BF16)

    def step(g, finish):
        nxt = units(jnp.minimum(g + 1, n_groups - 1))
        cur = units(g)
        tasks = []
        for m, ((c1, d1), (c0, d0)) in enumerate(zip(nxt, cur)):
            tasks.append(scores_task(m, c1, d1, (g + 1) % 2))
            for i in range(2):
                tasks.append(finish_task(2 * m + i, c0, d0, i, g % 2, finish))
        rolling(tasks, RET_ROLL_LAG)

    def loop(lo, hi, finish):
        def wrapped(g, carry):
            step(g, finish)
            return carry
        lax.fori_loop(lo, hi, wrapped, 0)

    rolling([scores_task(m, ci, d, 0) for m, (ci, d) in enumerate(units(0))], RET_ROLL_LAG)
    loop(0, n_groups // 2, first_visit)
    loop(n_groups // 2, n_groups, finalize)


def _ret_call(qr, kr, vr, gr, lg_rows):
    B, P, T, _ = qr.shape
    n_chunks = T // RET_CHUNK
    assert n_chunks % 2 == 0, T
    per_step = max(u for u in (RET_STEP_CHUNKS, 1) if n_chunks % (2 * u) == 0)
    n_units = 4 * per_step
    seq1 = pl.BlockSpec((1, 1, T, LANES), lambda b, p: (b, p, 0, 0))
    seq2 = pl.BlockSpec((1, 2, T, LANES), lambda b, p: (b, p, 0, 0))
    return pl.pallas_call(
        _ret_kernel,
        grid=(B, P),
        in_specs=[seq1, seq1, seq2, seq2,
                  pl.BlockSpec((1, 4, LANES), lambda b, p: (p, 0, 0))],
        out_specs=seq2,
        out_shape=jax.ShapeDtypeStruct((B, R_HEADS, T, LANES), BF16),
        scratch_shapes=[pltpu.VMEM((2, T, LANES), F32),
                        pltpu.VMEM((4, LANES, R_DV), F32),
                        pltpu.VMEM((4, RET_CHUNK, RET_CHUNK), F32),
                        pltpu.VMEM((4, RET_CHUNK, LANES), BF16),
                        pltpu.VMEM((4, RET_CHUNK, LANES), BF16),
                        pltpu.VMEM((2, RET_CHUNK, LANES), BF16),
                        pltpu.VMEM((2, n_units, RET_CHUNK, RET_CHUNK), BF16),
                        pltpu.VMEM((2, n_units, LANES, R_DV), F32)],
        compiler_params=pltpu.CompilerParams(
            dimension_semantics=("arbitrary", "arbitrary"), vmem_limit_bytes=VMEM_LIMIT),
        name="retention",
    )(qr, kr, vr, gr, lg_rows)


def _outproj_kernel(ya_ref, yr_ref, mg_ref, x_ref, gate_ref, wpa_ref, wpb_ref, wout_ref,
                    lng_ref, lnb_ref, o_ref, m_scr):
    rows = x_ref.shape[1]
    sub = min(OUT_SUB_ROWS, rows)

    def branch_task(r):
        rs = slice(r * sub, (r + 1) * sub)

        def issue():
            ya = jnp.concatenate([ya_ref[0, h, rs, :] for h in range(A_HEADS)], axis=1)
            yr = jnp.concatenate([yr_ref[0, h, rs, :] for h in range(R_HEADS)], axis=1)
            return _dot(ya, wpa_ref[...]), _dot(yr, wpb_ref[...])

        def consume(issued):
            pa, pr = issued
            mg = mg_ref[0, rs, :]
            m = mg[:, :D_MODEL].astype(F32) * pa + mg[:, D_MODEL:].astype(F32) * pr
            m_scr[rs, :] = m.astype(BF16)
        return issue, consume

    def out_task(r):
        rs = slice(r * sub, (r + 1) * sub)

        def issue():
            return _dot(m_scr[rs, :], wout_ref[...])

        def consume(s):
            h = DEEPNORM_ALPHA * x_ref[0, rs, :] + gate_ref[0] * s
            mu = jnp.mean(h, axis=-1, keepdims=True)
            cen = h - mu
            var = jnp.mean(cen * cen, axis=-1, keepdims=True)
            o_ref[0, rs, :] = cen * lax.rsqrt(var + LN_EPS) * lng_ref[...] + lnb_ref[...]
        return issue, consume

    n_sub = rows // sub
    tasks = [branch_task(0)]
    for r in range(n_sub):
        if r + 1 < n_sub:
            tasks.append(branch_task(r + 1))
        tasks.append(out_task(r))
    lag = 1 if n_sub > 1 else 0
    pending = []
    for issue, consume in tasks:
        pending.append((consume, issue()))
        if len(pending) > lag:
            cons, res = pending.pop(0)
            cons(res)
    for cons, res in pending:
        cons(res)


def _outproj_call(ya, yr, mg, x, gate1p, w_pa, w_pb, w_out, ln_g, ln_b):
    B, T, D = x.shape
    rows = min(OUT_ROWS, T)
    hspec = pl.BlockSpec((1, A_HEADS, rows, LANES), lambda b, t: (b, 0, t, 0))
    return pl.pallas_call(
        _outproj_kernel,
        grid=(B, T // rows),
        in_specs=[hspec, hspec,
                  pl.BlockSpec((1, rows, 2 * D), lambda b, t: (b, t, 0)),
                  pl.BlockSpec((1, rows, D), lambda b, t: (b, t, 0)),
                  pl.BlockSpec((1, 1, D), lambda b, t: (b, 0, 0)),
                  _const_spec((D, D)), _const_spec((D, D)), _const_spec((D, D)),
                  _const_spec((1, D)), _const_spec((1, D))],
        out_specs=pl.BlockSpec((1, rows, D), lambda b, t: (b, t, 0)),
        out_shape=jax.ShapeDtypeStruct((B, T, D), F32),
        scratch_shapes=[pltpu.VMEM((rows, D), BF16)],
        compiler_params=pltpu.CompilerParams(
            dimension_semantics=("arbitrary", "arbitrary"), vmem_limit_bytes=VMEM_LIMIT),
        name="outproj",
    )(ya, yr, mg, x, gate1p, w_pa, w_pb, w_out, ln_g, ln_b)


def _rope_tables(T):
    half = R_DK // 2
    pos = jnp.arange(T, dtype=F32)
    inv = ROPE_BASE ** (-jnp.arange(0, R_DK, 2, dtype=F32) / R_DK)
    ang = pos[:, None] * inv[None, :]
    cos, sin = jnp.cos(ang), jnp.sin(ang)
    return jnp.tile(cos, (1, LANES // half)), jnp.concatenate([-sin, -sin, sin, sin], axis=1)


def _trunk(x, ada, params):
    B, T, D = x.shape
    cos_p, sin_p = _rope_tables(T)
    for l in range(DEPTH):
        p = params[l]
        shift = ada[l, :, :D].reshape(B, 1, D)
        sc1p = (1.0 + ada[l, :, D:2 * D]).reshape(B, 1, D)
        gate1p = (1.0 + ada[l, :, 2 * D:]).reshape(B, 1, D)
        qa, lff, kf, lfb, kb, va, ga, qr, kr, vr, gr, mg, dstat, mstat = _inproj_call(
            x, sc1p, shift, p["w_in"], p["w_rope"], p["w_mg"], p["b_mg"], p["lbm"], p["oml"],
            cos_p, sin_p)
        ya = _hgrn_call(qa, lff, kf, lfb, kb, va, ga, dstat, mstat, p["nw"])
        yr = _ret_call(qr, kr, vr, gr, p["lg_rows"])
        x = _outproj_call(ya, yr, mg, x, gate1p, p["w_pa"], p["w_pb"], p["w_out"],
                          p["ln_g"], p["ln_b"])
    return x


def kernel(x_prompt, x_sample, c_prompt, c_sample, w_ada, b_ada, w_in, hgrn_lb, a_norm_w, ret_decay,
           w_pa, w_pb, w_mg, b_mg, w_out, ln_g, ln_b):
    D = D_MODEL
    pr = jax.nn.softmax(hgrn_lb.astype(F32), axis=0)
    lbs = jnp.cumsum(pr, axis=0) - pr[0:1]
    params = []
    for l in range(DEPTH):
        lb = lbs[l]
        lbm = jnp.maximum(lb, LOG_FLOOR)
        lg = jax.nn.log_sigmoid(ret_decay[l].astype(F32))
        lg_rows = jnp.broadcast_to(
            lg.reshape(2, R_PAIRS, 2).transpose(1, 0, 2).reshape(R_PAIRS, 4, 1), (R_PAIRS, 4, LANES))
        half = R_DK // 2
        w_rope = w_in[l][:, OFF_RQ:OFF_RV].reshape(D, 2, R_PAIRS, 2, 2, half)
        w_rope = w_rope.transpose(0, 1, 2, 4, 3, 5).reshape(D, 2 * R_HEADS * R_DK)
        params.append(dict(
            w_in=w_in[l].astype(BF16), w_rope=w_rope.astype(BF16), w_mg=w_mg[l].astype(BF16),
            w_pa=w_pa[l].astype(BF16), w_pb=w_pb[l].astype(BF16), w_out=w_out[l].astype(BF16),
            b_mg=b_mg[l].reshape(1, 2 * D), lbm=lbm.reshape(1, D), oml=(1.0 - lb).reshape(1, D),
            nw=a_norm_w[l].reshape(1, A_DV), lg_rows=lg_rows,
            ln_g=ln_g[l].reshape(1, D), ln_b=ln_b[l].reshape(1, D)))
    nbp = x_prompt.shape[0]
    ada = _ada_call(jnp.concatenate([c_prompt, c_sample], axis=0), w_ada, b_ada)
    y_prompt = _trunk(x_prompt, ada[:, :nbp], params)
    y_sample = _trunk(x_sample, ada[:, nbp:], params)
    return (y_prompt, y_sample)
```

```python
import functools

import numpy as np
import jax
import jax.numpy as jnp
from jax import lax
from jax.experimental import pallas as pl
from jax.experimental.pallas import tpu as pltpu

F32 = jnp.float32
BF16 = jnp.bfloat16

D_MODEL = 1024
DEPTH = 2
A_HEADS = 8
A_DK = 128
A_DV = 128
R_HEADS = 8
R_DK = 64
R_DV = 128
R_PAIRS = R_HEADS // 2
LN_EPS = 1e-5
LOG_FLOOR = 1e-30
NEG_BIG = -1e30
DEEPNORM_ALPHA = (2.0 * DEPTH) ** 0.25
ROPE_BASE = 10000.0

OFF_AQ, OFF_FF, OFF_FB, OFF_AI, OFF_AG, OFF_RQ, OFF_RK, OFF_RV, OFF_RG = (
    0, 1024, 2048, 3072, 4096, 5120, 5632, 6144, 7168)
IN_WIDTH = 8192

LANES = 128
HGRN_CHUNK = 64
HGRN_LEVELS = 6
RET_CHUNK = 128
HGRN_SAFE_DECAY = 75.0
HGRN_SAFE_MAG = 1e5
HGRN_FAST_UNROLL = 8
IN_ROLL_LAG = 1
HGRN_ROLL_LAG = 12
RET_STEP_CHUNKS = 4
RET_ROLL_LAG = 4
IN_ROWS = 256
OUT_ROWS = 1024
OUT_SUB_ROWS = 256
VMEM_LIMIT = 56 * 1024 * 1024


def _dot(a, b):
    return jnp.dot(a, b, preferred_element_type=F32)


def _dot_nt(a, b):
    return lax.dot_general(a, b, (((1,), (1,)), ((), ())), preferred_element_type=F32)


def _dot_tn(a, b):
    return lax.dot_general(a, b, (((0,), (0,)), ((), ())), preferred_element_type=F32)


def _sigmoid(z):
    return 1.0 / (1.0 + jnp.exp(-z))


def _const_spec(shape):
    nd = len(shape)
    return pl.BlockSpec(shape, lambda *_: (0,) * nd, pipeline_mode=pl.Buffered(1))


def _ada_kernel(c_ref, w_ref, b_ref, o_ref):
    c = c_ref[...]
    a = c * _sigmoid(c)
    o_ref[0] = jnp.dot(a, w_ref[0], preferred_element_type=F32,
                       precision=lax.Precision.HIGHEST) + b_ref[0]


def _ada_call(c, w_ada, b_ada):
    nb = c.shape[0]
    d = D_MODEL
    return pl.pallas_call(
        _ada_kernel,
        grid=(DEPTH, 3),
        in_specs=[
            pl.BlockSpec((nb, d), lambda l, j: (0, 0)),
            pl.BlockSpec((1, d, d), lambda l, j: (l, 0, j)),
            pl.BlockSpec((1, 1, d), lambda l, j: (l, 0, j)),
        ],
        out_specs=pl.BlockSpec((1, nb, d), lambda l, j: (l, 0, j)),
        out_shape=jax.ShapeDtypeStruct((DEPTH, nb, 3 * d), F32),
        compiler_params=pltpu.CompilerParams(
            dimension_semantics=("arbitrary", "arbitrary"), vmem_limit_bytes=VMEM_LIMIT),
        name="adaln",
    )(c, w_ada, b_ada.reshape(DEPTH, 1, 3 * d))


def _inproj_kernel(x_ref, sc_ref, sh_ref, win_ref, wrope_ref, wmg_ref, bmg_ref, lbm_ref, oml_ref,
                   cos_ref, sin_ref,
                   qa_ref, lff_ref, kf_ref, lfb_ref, kb_ref, va_ref, ga_ref,
                   qr_ref, kr_ref, vr_ref, gr_ref, mg_ref, dstat_ref, mstat_ref):
    rows = x_ref.shape[1]
    u = (x_ref[0] * sc_ref[0] + sh_ref[0]).astype(BF16)

    def col_max(z):
        return jnp.max(jnp.abs(z), axis=0, keepdims=True)

    def store_heads(ref, g, z):
        for i in range(2):
            ref[0, 2 * g + i] = z[:, i * LANES:(i + 1) * LANES].astype(BF16)

    tasks = []
    stats = {}

    def q_epilogue(g, z):
        z = z * _sigmoid(z) * (A_DK ** -0.5)
        stats[g] = [col_max(z), None]
        store_heads(qa_ref, g, z)

    def forget_epilogue(g, lf_ref, k_ref, z):
        lbm = lbm_ref[:, 2 * LANES * g:2 * LANES * (g + 1)]
        oml = oml_ref[:, 2 * LANES * g:2 * LANES * (g + 1)]
        f = lbm + oml * _sigmoid(z)
        lf = jnp.log(f)
        halfsum = jnp.sum(lf.reshape(2 * rows // HGRN_CHUNK, HGRN_CHUNK // 2, 2 * LANES), axis=1)
        halfmin = jnp.min(halfsum, axis=0, keepdims=True)
        for i in range(2):
            lf_ref[0, 2 * g + i] = lf[:, i * LANES:(i + 1) * LANES]
        store_heads(k_ref, g, (oml + lbm) - f)
        stats[g][1] = halfmin if stats[g][1] is None else jnp.minimum(stats[g][1], halfmin)

    def v_epilogue(g, z):
        mag, dmin = stats.pop(g)
        mag = jnp.maximum(mag, col_max(z))
        store_heads(va_ref, g, z)
        for i in range(2):
            lanes = slice(i * LANES, (i + 1) * LANES)
            dstat_ref[0, 0, 2 * g + i:2 * g + i + 1, :] = dmin[:, lanes]
            mstat_ref[0, 0, 2 * g + i:2 * g + i + 1, :] = mag[:, lanes]

    def silu_epilogue(ref, g, z):
        store_heads(ref, g, z * _sigmoid(z))

    def rope_epilogue(ref, g, scale, z):
        for i in range(2):
            zz = z[:, i * LANES:(i + 1) * LANES]
            rot = pltpu.roll(zz, LANES // 2, 1)
            ref[0, 2 * g + i] = ((zz * cos_ref[...] + rot * sin_ref[...]) * scale).astype(BF16)

    def merge_epilogue(g, z):
        c = 2 * LANES * g
        mg_ref[0, :, c:c + 2 * LANES] = _sigmoid(z + bmg_ref[:, c:c + 2 * LANES]).astype(BF16)

    part = functools.partial
    for g in range(A_HEADS // 2):
        c = 2 * LANES * g
        tasks += [(win_ref, OFF_AQ + c, part(q_epilogue, g)),
                  (win_ref, OFF_FF + c, part(forget_epilogue, g, lff_ref, kf_ref)),
                  (win_ref, OFF_RV + c, part(store_heads, vr_ref, g)),
                  (win_ref, OFF_FB + c, part(forget_epilogue, g, lfb_ref, kb_ref)),
                  (win_ref, OFF_AI + c, part(v_epilogue, g)),
                  (win_ref, OFF_AG + c, part(silu_epilogue, ga_ref, g)),
                  (win_ref, OFF_RG + c, part(silu_epilogue, gr_ref, g))]
    for g in range(R_PAIRS // 2):
        c = 2 * LANES * g
        tasks += [(wrope_ref, c, part(rope_epilogue, qr_ref, g, R_DK ** -0.5)),
                  (wrope_ref, R_HEADS * R_DK + c, part(rope_epilogue, kr_ref, g, 1.0))]
    for g in range(2 * D_MODEL // (2 * LANES)):
        tasks.append((wmg_ref, 2 * LANES * g, part(merge_epilogue, g)))

    pending = []
    for w_ref, c0, epilogue in tasks:
        pending.append((epilogue, _dot(u, w_ref[:, c0:c0 + 2 * LANES])))
        if len(pending) > IN_ROLL_LAG:
            ep, z = pending.pop(0)
            ep(z)
    for ep, z in pending:
        ep(z)
    assert not stats


def _inproj_call(x, sc1p, shift, w_in, w_rope, w_mg, b_mg, lbm, oml, cos_p, sin_p):
    B, T, D = x.shape
    rows = min(IN_ROWS, T)
    grid = (B, T // rows)
    hd = lambda n, dt: jax.ShapeDtypeStruct((B, n, T, LANES), dt)
    hspec = lambda n: pl.BlockSpec((1, n, rows, LANES), lambda b, t: (b, 0, t, 0))
    bvec = pl.BlockSpec((1, 1, D), lambda b, t: (b, 0, 0))
    stat_spec = pl.BlockSpec((1, 1, A_HEADS, LANES), lambda b, t: (b, t, 0, 0))
    stat_shape = jax.ShapeDtypeStruct((B, T // rows, A_HEADS, LANES), F32)
    return pl.pallas_call(
        _inproj_kernel,
        grid=grid,
        in_specs=[
            pl.BlockSpec((1, rows, D), lambda b, t: (b, t, 0)),
            bvec, bvec,
            _const_spec((D, IN_WIDTH)),
            _const_spec((D, 2 * R_HEADS * R_DK)),
            _const_spec((D, 2 * D)),
            _const_spec((1, 2 * D)),
            _const_spec((1, D)),
            _const_spec((1, D)),
            pl.BlockSpec((rows, LANES), lambda b, t: (t, 0)),
            pl.BlockSpec((rows, LANES), lambda b, t: (t, 0)),
        ],
        out_specs=[hspec(A_HEADS)] * 7 + [
            hspec(R_PAIRS), hspec(R_PAIRS), hspec(R_HEADS), hspec(R_HEADS),
            pl.BlockSpec((1, rows, 2 * D), lambda b, t: (b, t, 0)), stat_spec, stat_spec],
        out_shape=[hd(A_HEADS, BF16), hd(A_HEADS, F32), hd(A_HEADS, BF16), hd(A_HEADS, F32),
                   hd(A_HEADS, BF16), hd(A_HEADS, BF16), hd(A_HEADS, BF16),
            hd(R_PAIRS, BF16), hd(R_PAIRS, BF16), hd(R_HEADS, BF16), hd(R_HEADS, BF16),
            jax.ShapeDtypeStruct((B, T, 2 * D), BF16), stat_shape, stat_shape],
        compiler_params=pltpu.CompilerParams(
            dimension_semantics=("arbitrary", "arbitrary"), vmem_limit_bytes=VMEM_LIMIT),
        name="inproj",
    )(x, sc1p, shift, w_in, w_rope, w_mg, b_mg, lbm, oml, cos_p, sin_p)


def _hgrn_tables(reverse):
    C = HGRN_CHUNK
    idx = np.arange(C)
    cum = (idx[None, :] >= idx[:, None]) if reverse else (idx[None, :] <= idx[:, None])
    cum = cum.astype(np.float64)
    end = 0 if reverse else C - 1
    blocks = []
    for j in range(HGRN_LEVELS):
        m = C >> (j + 1)
        base = idx - idx % (2 * m)
        ref = base + (m if reverse else m - 1)
        blocks.append(cum - cum[ref])
    blocks.append(cum)
    blocks.append(cum[end][None, :] - cum)
    safe_tab = np.concatenate(blocks, axis=0)
    mid = C // 2 if reverse else C // 2 - 1
    fast_tab = np.tile(cum - cum[mid][None, :], (1, 2))
    x = idx[:, None] ^ idx[None, :]
    bl = np.zeros_like(x)
    nz = x > 0
    bl[nz] = np.floor(np.log2(x[nz])).astype(np.int64) + 1
    lvl = HGRN_LEVELS - bl
    coupled = (idx[:, None] <= idx[None, :]) if reverse else (idx[:, None] >= idx[None, :])
    lvl = np.where(coupled, lvl, -1)
    return safe_tab.astype(np.float32), fast_tab.astype(np.float32), lvl.astype(np.int32), end


def _hgrn_kernel(q_ref, lff_ref, kf_ref, lfb_ref, kb_ref, v_ref,
                 dstat_ref, mstat_ref,
                 tabf_ref, tabb_ref, ftabf_ref, ftabb_ref, lvf_ref, lvb_ref,
                 y_ref, o_scr, sf_scr, sb_scr, ops_scr, dec_scr, a_scr, u_scr, *, ends):
    C = HGRN_CHUNK
    T = q_ref.shape[2]
    n_chunks = T // C
    sf_scr[...] = jnp.zeros_like(sf_scr)
    sb_scr[...] = jnp.zeros_like(sb_scr)

    def load(ci, gate_refs):
        rows = pl.ds(pl.multiple_of(ci * C, C), C)
        lf_ref, k_ref = gate_refs
        lf = lf_ref[0, 0, rows, :]
        hi = lf.astype(BF16)
        lo = (lf - hi.astype(F32)).astype(BF16)
        return (rows, q_ref[0, 0, rows, :], k_ref[0, 0, rows, :], v_ref[0, 0, rows, :],
                (lf, hi, lo))

    def advance(s_scr, a, v, qi, kd, s_decay):
        st = s_scr[...]
        o = _dot(a.astype(BF16), v) + _dot_nt(qi, st.astype(BF16))
        s_scr[...] = st * s_decay + _dot_tn(v, kd)
        return o


    def chunk_rows(ci):
        return pl.ds(pl.multiple_of(ci * C, C), C)

    def prepare_task(c, ci, p, slot):
        start = C - 1 - p[4]

        def issue():
            _, q, k, _, (lf, hi, lo) = load(ci, p[0])
            return q, k, lf[start:start + 1], _dot(p[1][...], jnp.concatenate([hi, lo], axis=0))

        def consume(issued):
            q, k, lf_start, d = issued
            b_mid = lf_start - d[start:start + 1]
            d_end = d[p[4]:p[4] + 1]
            qh = q * jnp.exp(d).astype(BF16)
            kh = k * jnp.exp(-d).astype(BF16)
            ops_scr[slot, c, 0] = qh
            ops_scr[slot, c, 1] = kh
            ops_scr[slot, c, 2] = qh * jnp.exp(b_mid).astype(BF16)
            ops_scr[slot, c, 3] = kh * jnp.exp(d_end).astype(BF16)
            dec_scr[slot, c] = jnp.broadcast_to(jnp.exp(b_mid + d_end), (8, LANES))
        return issue, consume

    def scores_task(c, ci, p, ops_slot, slot):
        def issue():
            return (_dot_nt(ops_scr[ops_slot, c, 0], ops_scr[ops_slot, c, 1]),
                    _dot_tn(v_ref[0, 0, chunk_rows(ci), :], ops_scr[ops_slot, c, 3]))

        def consume(issued):
            a_scr[slot, c] = jnp.where(p[2][...] >= 0, issued[0], 0.0).astype(BF16)
            u_scr[slot, c] = issued[1]
        return issue, consume

    def finish_task(c, ci, p, ops_slot, slot, finish):
        rows = chunk_rows(ci)

        def issue():
            st = p[3][...]
            o = (_dot(a_scr[slot, c], v_ref[0, 0, rows, :])
                 + _dot_nt(ops_scr[ops_slot, c, 2], st.astype(BF16)))
            p[3][...] = st * dec_scr[ops_slot, c, 0:1, :] + u_scr[slot, c]
            return o

        def consume(o):
            finish(rows, o)
        return issue, consume

    def rolling(tasks, lag):
        pending = []
        for issue, consume in tasks:
            pending.append((consume, issue()))
            if len(pending) > lag:
                cons, res = pending.pop(0)
                cons(res)
        for cons, res in pending:
            cons(res)

    def safe_chunk(ci, lf_ref, tab_ref, lv_ref, s_scr, end):
        rows, qb, kb, v, (_, hi, lo) = load(ci, lf_ref)
        q = qb.astype(F32)
        k = kb.astype(F32)
        r = _dot(tab_ref[...], jnp.concatenate([hi, lo], axis=1))
        dm = r[:, :LANES] + r[:, LANES:]
        lv = lv_ref[...]
        a = jnp.where(lv == HGRN_LEVELS, _dot_nt(qb, kb), 0.0)
        for j in range(HGRN_LEVELS):
            e = jnp.exp(-jnp.abs(dm[j * C:(j + 1) * C]))
            aj = _dot_nt((q * e).astype(BF16), (k * e).astype(BF16))
            a = jnp.where(lv == j, aj, a)
        bq = dm[HGRN_LEVELS * C:(HGRN_LEVELS + 1) * C]
        bk = dm[(HGRN_LEVELS + 1) * C:(HGRN_LEVELS + 2) * C]
        qi = (q * jnp.exp(bq)).astype(BF16)
        kd = (k * jnp.exp(bk)).astype(BF16)
        return rows, advance(s_scr, a, v, qi, kd, jnp.exp(bq[end:end + 1, :]))

    def finalize(rows, o):
        y_ref[0, 0, rows, :] = (o_scr[rows, :] + o).astype(BF16)

    def first_visit(rows, o):
        o_scr[rows, :] = o

    def chain_groups(tf_ref, tb_ref, per_step):
        fwd = ((lff_ref, kf_ref), tf_ref, lvf_ref, sf_scr, ends[0])
        bwd = ((lfb_ref, kb_ref), tb_ref, lvb_ref, sb_scr, ends[1])

        def chains(g):
            out = []
            for u in range(per_step):
                i = g * per_step + u
                out += [(i, fwd), (n_chunks - 1 - i, bwd)]
            return out
        return chains

    def loop(lo, hi, body, finish):
        def wrapped(g, carry):
            body(g, finish)
            return carry
        lax.fori_loop(lo, hi, wrapped, 0)

    def run_safe():
        chains = chain_groups(tabf_ref, tabb_ref, 1)

        def step(g, finish):
            for ci, p in chains(g):
                finish(*safe_chunk(ci, *p))

        loop(0, n_chunks // 2, step, first_visit)
        loop(n_chunks // 2, n_chunks, step, finalize)

    def run_fast():
        per_step = ops_scr.shape[1] // 2
        chains = chain_groups(ftabf_ref, ftabb_ref, per_step)
        n_groups = n_chunks // per_step
        last = n_groups - 1

        def step(g, finish):
            g1 = jnp.minimum(g + 1, last)
            g2 = jnp.minimum(g + 2, last)
            tasks = []
            for c, ((c1, p1), (c2, p2), (c0, p0)) in enumerate(zip(chains(g1), chains(g2), chains(g))):
                tasks.append(scores_task(c, c1, p1, g1 % 3, (g + 1) % 2))
                tasks.append(prepare_task(c, c2, p2, (g + 2) % 3))
                tasks.append(finish_task(c, c0, p0, g % 3, g % 2, finish))
            rolling(tasks, HGRN_ROLL_LAG)

        fill = [prepare_task(c, ci, p, g) for g in range(2) for c, (ci, p) in enumerate(chains(g))]
        fill += [scores_task(c, ci, p, 0, 0) for c, (ci, p) in enumerate(chains(0))]
        rolling(fill, min(HGRN_ROLL_LAG, 2 * per_step))
        loop(0, n_groups // 2, step, first_visit)
        loop(n_groups // 2, n_groups, step, finalize)

    is_head = lax.broadcasted_iota(jnp.int32, (A_HEADS, LANES), 0) == pl.program_id(1)
    dec = jnp.min(jnp.where(is_head, jnp.min(dstat_ref[0], axis=0), 0.0))
    mag = jnp.max(jnp.where(is_head, jnp.max(mstat_ref[0], axis=0), 0.0))
    is_safe = jnp.logical_and(dec > -HGRN_SAFE_DECAY, mag < HGRN_SAFE_MAG)

    pl.when(is_safe)(run_fast)
    pl.when(jnp.logical_not(is_safe))(run_safe)


def _hgrn_call(qa, lff, kf, lfb, kb, va, dstat, mstat):
    B, H, T, _ = qa.shape
    stat = pl.BlockSpec((1,) + dstat.shape[1:], lambda b, h: (b, 0, 0, 0))
    tabf, ftabf, lvf, endf = _hgrn_tables(False)
    tabb, ftabb, lvb, endb = _hgrn_tables(True)
    seq = pl.BlockSpec((1, 1, T, LANES), lambda b, h: (b, h, 0, 0))
    n_chunks = T // HGRN_CHUNK
    assert n_chunks % 4 == 0, T
    per_step = max(u for u in (HGRN_FAST_UNROLL, 2, 1) if n_chunks % (4 * u) == 0)
    return pl.pallas_call(
        functools.partial(_hgrn_kernel, ends=(endf, endb)),
        grid=(B, H),
        in_specs=[seq] * 6 + [stat, stat,
                  _const_spec(tabf.shape), _const_spec(tabb.shape),
                  _const_spec(ftabf.shape), _const_spec(ftabb.shape),
                  _const_spec(lvf.shape), _const_spec(lvb.shape)],
        out_specs=seq,
        out_shape=jax.ShapeDtypeStruct((B, H, T, LANES), BF16),
        scratch_shapes=[pltpu.VMEM((T, LANES), F32),
                        pltpu.VMEM((A_DV, A_DK), F32),
                        pltpu.VMEM((A_DV, A_DK), F32),
                        pltpu.VMEM((3, 2 * per_step, 4, HGRN_CHUNK, LANES), BF16),
                        pltpu.VMEM((3, 2 * per_step, 8, LANES), F32),
                        pltpu.VMEM((2, 2 * per_step, HGRN_CHUNK, HGRN_CHUNK), BF16),
                        pltpu.VMEM((2, 2 * per_step, A_DV, A_DK), F32)],
        compiler_params=pltpu.CompilerParams(
            dimension_semantics=("arbitrary", "arbitrary"), vmem_limit_bytes=VMEM_LIMIT),
        name="hgrn",
    )(qa, lff, kf, lfb, kb, va, dstat, mstat,
      jnp.asarray(tabf, BF16), jnp.asarray(tabb, BF16),
      jnp.asarray(ftabf, BF16), jnp.asarray(ftabb, BF16), jnp.asarray(lvf), jnp.asarray(lvb))


def _ret_kernel(q_ref, k_ref, v_ref, lg_ref, y_ref,
                o_scr, s_scr, dec_scr, qd_scr, kd_scr, hm_scr, a_scr, u_scr):
    C = RET_CHUNK
    T = q_ref.shape[2]
    n_chunks = T // C
    per_step = a_scr.shape[1] // 4
    n_groups = n_chunks // per_step
    s_scr[...] = jnp.zeros_like(s_scr)

    lane = lax.broadcasted_iota(jnp.int32, (1, LANES), 1)
    head_mask = [(lane % (LANES // 2)) < (LANES // 4), (lane % (LANES // 2)) >= (LANES // 4)]
    tpos = lax.broadcasted_iota(jnp.int32, (C, C), 0)
    spos = lax.broadcasted_iota(jnp.int32, (C, C), 1)
    pos = lax.broadcasted_iota(jnp.int32, (C, 1), 0).astype(F32)
    s_dec = []
    for i in range(2):
        hm_scr[i] = jnp.broadcast_to(jnp.where(head_mask[i], 1.0, 0.0), (C, LANES)).astype(BF16)
    for d in range(2):
        dist = ((tpos - spos) if d == 0 else (spos - tpos)).astype(F32)
        after = pos if d == 0 else (C - 1.0 - pos)
        for i in range(2):
            lg = lg_ref[0, 2 * d + i:2 * d + i + 1, :]
            dec_scr[2 * d + i] = jnp.exp(jnp.where(dist >= 0, dist * lg, NEG_BIG))
            qd_scr[2 * d + i] = jnp.where(head_mask[i], jnp.exp((after + 1.0) * lg), 0.0).astype(BF16)
            kd_scr[2 * d + i] = jnp.where(head_mask[i], jnp.exp((C - 1.0 - after) * lg), 0.0).astype(BF16)
            s_dec.append(jnp.exp(C * lg))

    def chunk_rows(ci):
        return pl.ds(pl.multiple_of(ci * C, C), C)

    def units(g):
        out = []
        for u in range(per_step):
            j = g * per_step + u
            out += [(j, 0), (n_chunks - 1 - j, 1)]
        return out

    def scores_task(m, ci, d, slot):
        rows = chunk_rows(ci)

        def issue():
            q = q_ref[0, 0, rows, :]
            k = k_ref[0, 0, rows, :]
            qm = jnp.concatenate([q * hm_scr[0], q * hm_scr[1]], axis=0)
            km = k * (kd_scr[2 * d] + kd_scr[2 * d + 1])
            vv = jnp.concatenate([v_ref[0, 0, rows, :], v_ref[0, 1, rows, :]], axis=1)
            return _dot_nt(qm, k), _dot_tn(km, vv)

        def consume(issued):
            sc, upd = issued
            for i in range(2):
                a_scr[slot, 2 * m + i] = (sc[i * C:(i + 1) * C] * dec_scr[2 * d + i]).astype(BF16)
                u_scr[slot, 2 * m + i] = upd[:, i * R_DV:(i + 1) * R_DV]
        return issue, consume

    def finish_task(n, ci, d, i, slot, finish):
        rows = chunk_rows(ci)

        def issue():
            st = s_scr[2 * d + i]
            qd = q_ref[0, 0, rows, :] * qd_scr[2 * d + i]
            lhs = jnp.concatenate([a_scr[slot, n], qd], axis=1)
            rhs = jnp.concatenate([v_ref[0, i, rows, :], st.astype(BF16)], axis=0)
            s_scr[2 * d + i] = st * s_dec[2 * d + i] + u_scr[slot, n]
            return _dot(lhs, rhs)

        def consume(o):
            finish(i, rows, o)
        return issue, consume

    def rolling(tasks, lag):
        pending = []
        for issue, consume in tasks:
            pending.append((consume, issue()))
            if len(pending) > lag:
                cons, res = pending.pop(0)
                cons(res)
        for cons, res in pending:
            cons(res)

    def first_visit(i, rows, o):
        o_scr[i, rows, :] = o

    def finalize(i, rows, o):
        y_ref[0, i, rows, :] = (o_scr[i, rows, :] + o).astype(BF16)

    def step(g, finish):
        nxt = units(jnp.minimum(g + 1, n_groups - 1))
        cur = units(g)
        tasks = []
        for m, ((c1, d1), (c0, d0)) in enumerate(zip(nxt, cur)):
            tasks.append(scores_task(m, c1, d1, (g + 1) % 2))
            for i in range(2):
                tasks.append(finish_task(2 * m + i, c0, d0, i, g % 2, finish))
        rolling(tasks, RET_ROLL_LAG)

    def loop(lo, hi, finish):
        def wrapped(g, carry):
            step(g, finish)
            return carry
        lax.fori_loop(lo, hi, wrapped, 0)

    rolling([scores_task(m, ci, d, 0) for m, (ci, d) in enumerate(units(0))], RET_ROLL_LAG)
    loop(0, n_groups // 2, first_visit)
    loop(n_groups // 2, n_groups, finalize)


def _ret_call(qr, kr, vr, lg_rows):
    B, P, T, _ = qr.shape
    n_chunks = T // RET_CHUNK
    assert n_chunks % 2 == 0, T
    per_step = max(u for u in (RET_STEP_CHUNKS, 1) if n_chunks % (2 * u) == 0)
    n_units = 4 * per_step
    seq1 = pl.BlockSpec((1, 1, T, LANES), lambda b, p: (b, p, 0, 0))
    seq2 = pl.BlockSpec((1, 2, T, LANES), lambda b, p: (b, p, 0, 0))
    return pl.pallas_call(
        _ret_kernel,
        grid=(B, P),
        in_specs=[seq1, seq1, seq2,
                  pl.BlockSpec((1, 4, LANES), lambda b, p: (p, 0, 0))],
        out_specs=seq2,
        out_shape=jax.ShapeDtypeStruct((B, R_HEADS, T, LANES), BF16),
        scratch_shapes=[pltpu.VMEM((2, T, LANES), F32),
                        pltpu.VMEM((4, LANES, R_DV), F32),
                        pltpu.VMEM((4, RET_CHUNK, RET_CHUNK), F32),
                        pltpu.VMEM((4, RET_CHUNK, LANES), BF16),
                        pltpu.VMEM((4, RET_CHUNK, LANES), BF16),
                        pltpu.VMEM((2, RET_CHUNK, LANES), BF16),
                        pltpu.VMEM((2, n_units, RET_CHUNK, RET_CHUNK), BF16),
                        pltpu.VMEM((2, n_units, LANES, R_DV), F32)],
        compiler_params=pltpu.CompilerParams(
            dimension_semantics=("arbitrary", "arbitrary"), vmem_limit_bytes=VMEM_LIMIT),
        name="retention",
    )(qr, kr, vr, lg_rows)


def _outproj_kernel(oa_ref, ga_ref, or_ref, gr_ref, mg_ref, x_ref, gate_ref, nw_ref,
                    wpa_ref, wpb_ref, wout_ref, lng_ref, lnb_ref, o_ref, m_scr):
    rows = x_ref.shape[1]
    sub = min(OUT_SUB_ROWS, rows)

    def hgrn_head(h, rs):
        o = oa_ref[0, h, rs, :].astype(F32)
        ms = jnp.mean(o * o, axis=-1, keepdims=True)
        y = o * lax.rsqrt(ms + LN_EPS) * nw_ref[...] * ga_ref[0, h, rs, :].astype(F32)
        return y.astype(BF16)

    def retention_head(h, rs):
        o = or_ref[0, h, rs, :].astype(F32)
        mu = jnp.mean(o, axis=-1, keepdims=True)
        var = jnp.mean(o * o, axis=-1, keepdims=True) - mu * mu
        return ((o - mu) * lax.rsqrt(var + LN_EPS) * gr_ref[0, h, rs, :].astype(F32)).astype(BF16)

    def branch_task(r):
        rs = slice(r * sub, (r + 1) * sub)

        def issue():
            ya = jnp.concatenate([hgrn_head(h, rs) for h in range(A_HEADS)], axis=1)
            yr = jnp.concatenate([retention_head(h, rs) for h in range(R_HEADS)], axis=1)
            return _dot(ya, wpa_ref[...]), _dot(yr, wpb_ref[...])

        def consume(issued):
            pa, pr = issued
            mg = mg_ref[0, rs, :]
            m = mg[:, :D_MODEL].astype(F32) * pa + mg[:, D_MODEL:].astype(F32) * pr
            m_scr[rs, :] = m.astype(BF16)
        return issue, consume

    def out_task(r):
        rs = slice(r * sub, (r + 1) * sub)

        def issue():
            return _dot(m_scr[rs, :], wout_ref[...])

        def consume(s):
            h = DEEPNORM_ALPHA * x_ref[0, rs, :] + gate_ref[0] * s
            mu = jnp.mean(h, axis=-1, keepdims=True)
            cen = h - mu
            var = jnp.mean(cen * cen, axis=-1, keepdims=True)
            o_ref[0, rs, :] = cen * lax.rsqrt(var + LN_EPS) * lng_ref[...] + lnb_ref[...]
        return issue, consume

    n_sub = rows // sub
    tasks = [branch_task(0)]
    for r in range(n_sub):
        if r + 1 < n_sub:
            tasks.append(branch_task(r + 1))
        tasks.append(out_task(r))
    lag = 1 if n_sub > 1 else 0
    pending = []
    for issue, consume in tasks:
        pending.append((consume, issue()))
        if len(pending) > lag:
            cons, res = pending.pop(0)
            cons(res)
    for cons, res in pending:
        cons(res)


def _outproj_call(o_a, ga, o_r, gr, mg, x, gate1p, nw, w_pa, w_pb, w_out, ln_g, ln_b):
    B, T, D = x.shape
    rows = min(OUT_ROWS, T)
    hspec = pl.BlockSpec((1, A_HEADS, rows, LANES), lambda b, t: (b, 0, t, 0))
    return pl.pallas_call(
        _outproj_kernel,
        grid=(B, T // rows),
        in_specs=[hspec, hspec, hspec, hspec,
                  pl.BlockSpec((1, rows, 2 * D), lambda b, t: (b, t, 0)),
                  pl.BlockSpec((1, rows, D), lambda b, t: (b, t, 0)),
                  pl.BlockSpec((1, 1, D), lambda b, t: (b, 0, 0)),
                  _const_spec((1, LANES)),
                  _const_spec((D, D)), _const_spec((D, D)), _const_spec((D, D)),
                  _const_spec((1, D)), _const_spec((1, D))],
        out_specs=pl.BlockSpec((1, rows, D), lambda b, t: (b, t, 0)),
        out_shape=jax.ShapeDtypeStruct((B, T, D), F32),
        scratch_shapes=[pltpu.VMEM((rows, D), BF16)],
        compiler_params=pltpu.CompilerParams(
            dimension_semantics=("arbitrary", "arbitrary"), vmem_limit_bytes=VMEM_LIMIT),
        name="outproj",
    )(o_a, ga, o_r, gr, mg, x, gate1p, nw, w_pa, w_pb, w_out, ln_g, ln_b)


def _rope_tables(T):
    half = R_DK // 2
    pos = jnp.arange(T, dtype=F32)
    inv = ROPE_BASE ** (-jnp.arange(0, R_DK, 2, dtype=F32) / R_DK)
    ang = pos[:, None] * inv[None, :]
    cos, sin = jnp.cos(ang), jnp.sin(ang)
    return jnp.tile(cos, (1, LANES // half)), jnp.concatenate([-sin, -sin, sin, sin], axis=1)


def _trunk(x, ada, params):
    B, T, D = x.shape
    cos_p, sin_p = _rope_tables(T)
    for l in range(DEPTH):
        p = params[l]
        shift = ada[l, :, :D].reshape(B, 1, D)
        sc1p = (1.0 + ada[l, :, D:2 * D]).reshape(B, 1, D)
        gate1p = (1.0 + ada[l, :, 2 * D:]).reshape(B, 1, D)
        qa, lff, kf, lfb, kb, va, ga, qr, kr, vr, gr, mg, dstat, mstat = _inproj_call(
            x, sc1p, shift, p["w_in"], p["w_rope"], p["w_mg"], p["b_mg"], p["lbm"], p["oml"],
            cos_p, sin_p)
        o_a = _hgrn_call(qa, lff, kf, lfb, kb, va, dstat, mstat)
        o_r = _ret_call(qr, kr, vr, p["lg_rows"])
        x = _outproj_call(o_a, ga, o_r, gr, mg, x, gate1p, p["nw"], p["w_pa"], p["w_pb"], p["w_out"],
                          p["ln_g"], p["ln_b"])
    return x


def kernel(x_prompt, x_sample, c_prompt, c_sample, w_ada, b_ada, w_in, hgrn_lb, a_norm_w, ret_decay,
           w_pa, w_pb, w_mg, b_mg, w_out, ln_g, ln_b):
    D = D_MODEL
    pr = jax.nn.softmax(hgrn_lb.astype(F32), axis=0)
    lbs = jnp.cumsum(pr, axis=0) - pr[0:1]
    params = []
    for l in range(DEPTH):
        lb = lbs[l]
        lbm = jnp.maximum(lb, LOG_FLOOR)
        lg = jax.nn.log_sigmoid(ret_decay[l].astype(F32))
        lg_rows = jnp.broadcast_to(
            lg.reshape(2, R_PAIRS, 2).transpose(1, 0, 2).reshape(R_PAIRS, 4, 1), (R_PAIRS, 4, LANES))
        half = R_DK // 2
        w_rope = w_in[l][:, OFF_RQ:OFF_RV].reshape(D, 2, R_PAIRS, 2, 2, half)
        w_rope = w_rope.transpose(0, 1, 2, 4, 3, 5).reshape(D, 2 * R_HEADS * R_DK)
        params.append(dict(
            w_in=w_in[l].astype(BF16), w_rope=w_rope.astype(BF16), w_mg=w_mg[l].astype(BF16),
            w_pa=w_pa[l].astype(BF16), w_pb=w_pb[l].astype(BF16), w_out=w_out[l].astype(BF16),
            b_mg=b_mg[l].reshape(1, 2 * D), lbm=lbm.reshape(1, D), oml=(1.0 - lb).reshape(1, D),
            nw=a_norm_w[l].reshape(1, A_DV), lg_rows=lg_rows,
            ln_g=ln_g[l].reshape(1, D), ln_b=ln_b[l].reshape(1, D)))
    nbp = x_prompt.shape[0]
    ada = _ada_call(jnp.concatenate([c_prompt, c_sample], axis=0), w_ada, b_ada)
    y_prompt = _trunk(x_prompt, ada[:, :nbp], params)
    y_sample = _trunk(x_sample, ada[:, nbp:], params)
    return (y_prompt, y_sample)
```

```python
import functools

import numpy as np
import jax
import jax.numpy as jnp
from jax import lax
from jax.experimental import pallas as pl
from jax.experimental.pallas import tpu as pltpu

F32 = jnp.float32
BF16 = jnp.bfloat16

D_MODEL = 1024
DEPTH = 2
A_HEADS = 8
A_DK = 128
A_DV = 128
R_HEADS = 8
R_DK = 64
R_DV = 128
R_PAIRS = R_HEADS // 2
LN_EPS = 1e-5
LOG_FLOOR = 1e-30
NEG_BIG = -1e30
DEEPNORM_ALPHA = (2.0 * DEPTH) ** 0.25
ROPE_BASE = 10000.0

OFF_AQ, OFF_FF, OFF_FB, OFF_AI, OFF_AG, OFF_RQ, OFF_RK, OFF_RV, OFF_RG = (
    0, 1024, 2048, 3072, 4096, 5120, 5632, 6144, 7168)
IN_WIDTH = 8192

LANES = 128
HGRN_CHUNK = 64
HGRN_LEVELS = 6
RET_CHUNK = 128
HGRN_SAFE_DECAY = 75.0
HGRN_SAFE_MAG = 1e5
HGRN_FAST_UNROLL = 8
IN_ROLL_LAG = 1
HGRN_ROLL_LAG = 12
RET_STEP_CHUNKS = 4
RET_ROLL_LAG = 4
IN_ROWS = 512
IN_SUB_ROWS = 256
OUT_ROWS = 1024
OUT_SUB_ROWS = 256
VMEM_LIMIT = 60 * 1024 * 1024


def _dot(a, b):
    return jnp.dot(a, b, preferred_element_type=F32)


def _dot_nt(a, b):
    return lax.dot_general(a, b, (((1,), (1,)), ((), ())), preferred_element_type=F32)


def _dot_tn(a, b):
    return lax.dot_general(a, b, (((0,), (0,)), ((), ())), preferred_element_type=F32)


def _sigmoid(z):
    return 1.0 / (1.0 + jnp.exp(-z))


def _const_spec(shape):
    nd = len(shape)
    return pl.BlockSpec(shape, lambda *_: (0,) * nd, pipeline_mode=pl.Buffered(1))


def _ada_kernel(c_ref, w_ref, b_ref, o_ref):
    c = c_ref[...]
    a = c * _sigmoid(c)
    o_ref[0] = jnp.dot(a, w_ref[0], preferred_element_type=F32,
                       precision=lax.Precision.HIGHEST) + b_ref[0]


def _ada_call(c, w_ada, b_ada):
    nb = c.shape[0]
    d = D_MODEL
    return pl.pallas_call(
        _ada_kernel,
        grid=(DEPTH, 3),
        in_specs=[
            pl.BlockSpec((nb, d), lambda l, j: (0, 0)),
            pl.BlockSpec((1, d, d), lambda l, j: (l, 0, j)),
            pl.BlockSpec((1, 1, d), lambda l, j: (l, 0, j)),
        ],
        out_specs=pl.BlockSpec((1, nb, d), lambda l, j: (l, 0, j)),
        out_shape=jax.ShapeDtypeStruct((DEPTH, nb, 3 * d), F32),
        compiler_params=pltpu.CompilerParams(
            dimension_semantics=("arbitrary", "arbitrary"), vmem_limit_bytes=VMEM_LIMIT),
        name="adaln",
    )(c, w_ada, b_ada.reshape(DEPTH, 1, 3 * d))


def _inproj_kernel(x_ref, sc_ref, sh_ref, win_ref, wrope_ref, wmg_ref, bmg_ref, lbm_ref, oml_ref,
                   cos_ref, sin_ref,
                   qa_ref, lff_ref, kf_ref, lfb_ref, kb_ref, va_ref, ga_ref,
                   qr_ref, kr_ref, vr_ref, gr_ref, mg_ref, dstat_ref, mstat_ref):
    rows = x_ref.shape[1]
    sub = min(IN_SUB_ROWS, rows)
    n_sub = rows // sub
    row_slices = [slice(s * sub, (s + 1) * sub) for s in range(n_sub)]
    u = [(x_ref[0, rs, :] * sc_ref[0] + sh_ref[0]).astype(BF16) for rs in row_slices]

    def col_max(z):
        return jnp.max(jnp.abs(z), axis=0, keepdims=True)

    def store_heads(ref, g, rs, z):
        for i in range(2):
            ref[0, 2 * g + i, rs, :] = z[:, i * LANES:(i + 1) * LANES].astype(BF16)

    stats = {}

    def fold_stat(g, name, value, reduce):
        cur = stats.setdefault(g, {})
        cur[name] = value if name not in cur else reduce(cur[name], value)

    def q_epilogue(g, rs, z):
        z = z * _sigmoid(z) * (A_DK ** -0.5)
        fold_stat(g, "mag", col_max(z), jnp.maximum)
        store_heads(qa_ref, g, rs, z)

    def forget_epilogue(g, lf_ref, k_ref, rs, z):
        lbm = lbm_ref[:, 2 * LANES * g:2 * LANES * (g + 1)]
        oml = oml_ref[:, 2 * LANES * g:2 * LANES * (g + 1)]
        f = lbm + oml * _sigmoid(z)
        lf = jnp.log(f)
        halfsum = jnp.sum(lf.reshape(2 * sub // HGRN_CHUNK, HGRN_CHUNK // 2, 2 * LANES), axis=1)
        fold_stat(g, "dmin", jnp.min(halfsum, axis=0, keepdims=True), jnp.minimum)
        for i in range(2):
            lf_ref[0, 2 * g + i, rs, :] = lf[:, i * LANES:(i + 1) * LANES]
        store_heads(k_ref, g, rs, (oml + lbm) - f)

    def v_epilogue(g, last, rs, z):
        fold_stat(g, "mag", col_max(z), jnp.maximum)
        store_heads(va_ref, g, rs, z)
        if last:
            cur = stats.pop(g)
            for i in range(2):
                lanes = slice(i * LANES, (i + 1) * LANES)
                dstat_ref[0, 0, 2 * g + i:2 * g + i + 1, :] = cur["dmin"][:, lanes]
                mstat_ref[0, 0, 2 * g + i:2 * g + i + 1, :] = cur["mag"][:, lanes]

    def silu_epilogue(ref, g, rs, z):
        store_heads(ref, g, rs, z * _sigmoid(z))

    def rope_epilogue(ref, g, scale, rs, z):
        for i in range(2):
            zz = z[:, i * LANES:(i + 1) * LANES]
            rot = pltpu.roll(zz, LANES // 2, 1)
            ref[0, 2 * g + i, rs, :] = ((zz * cos_ref[rs, :] + rot * sin_ref[rs, :]) * scale).astype(BF16)

    def merge_epilogue(g, rs, z):
        c = 2 * LANES * g
        mg_ref[0, rs, c:c + 2 * LANES] = _sigmoid(z + bmg_ref[:, c:c + 2 * LANES]).astype(BF16)

    part = functools.partial

    def column_tasks(last):
        tasks = []
        for g in range(A_HEADS // 2):
            c = 2 * LANES * g
            tasks += [(win_ref, OFF_AQ + c, part(q_epilogue, g)),
                      (win_ref, OFF_FF + c, part(forget_epilogue, g, lff_ref, kf_ref)),
                      (win_ref, OFF_RV + c, part(store_heads, vr_ref, g)),
                      (win_ref, OFF_FB + c, part(forget_epilogue, g, lfb_ref, kb_ref)),
                      (win_ref, OFF_AI + c, part(v_epilogue, g, last)),
                      (win_ref, OFF_AG + c, part(silu_epilogue, ga_ref, g)),
                      (win_ref, OFF_RG + c, part(silu_epilogue, gr_ref, g))]
        for g in range(R_PAIRS // 2):
            c = 2 * LANES * g
            tasks += [(wrope_ref, c, part(rope_epilogue, qr_ref, g, R_DK ** -0.5)),
                      (wrope_ref, R_HEADS * R_DK + c, part(rope_epilogue, kr_ref, g, 1.0))]
        for g in range(2 * D_MODEL // (2 * LANES)):
            tasks.append((wmg_ref, 2 * LANES * g, part(merge_epilogue, g)))
        return tasks

    per_sub = [column_tasks(s == n_sub - 1) for s in range(n_sub)]
    pending = []
    for column in zip(*per_sub):
        for s, (w_ref, c0, epilogue) in enumerate(column):
            z = _dot(u[s], w_ref[:, c0:c0 + 2 * LANES])
            pending.append((part(epilogue, row_slices[s]), z))
            if len(pending) > IN_ROLL_LAG:
                ep, zz = pending.pop(0)
                ep(zz)
    for ep, zz in pending:
        ep(zz)
    assert not stats


def _inproj_call(x, sc1p, shift, w_in, w_rope, w_mg, b_mg, lbm, oml, cos_p, sin_p):
    B, T, D = x.shape
    rows = min(IN_ROWS, T)
    grid = (B, T // rows)
    hd = lambda n, dt: jax.ShapeDtypeStruct((B, n, T, LANES), dt)
    hspec = lambda n: pl.BlockSpec((1, n, rows, LANES), lambda b, t: (b, 0, t, 0))
    bvec = pl.BlockSpec((1, 1, D), lambda b, t: (b, 0, 0))
    stat_spec = pl.BlockSpec((1, 1, A_HEADS, LANES), lambda b, t: (b, t, 0, 0))
    stat_shape = jax.ShapeDtypeStruct((B, T // rows, A_HEADS, LANES), F32)
    return pl.pallas_call(
        _inproj_kernel,
        grid=grid,
        in_specs=[
            pl.BlockSpec((1, rows, D), lambda b, t: (b, t, 0)),
            bvec, bvec,
            _const_spec((D, IN_WIDTH)),
            _const_spec((D, 2 * R_HEADS * R_DK)),
            _const_spec((D, 2 * D)),
            _const_spec((1, 2 * D)),
            _const_spec((1, D)),
            _const_spec((1, D)),
            pl.BlockSpec((rows, LANES), lambda b, t: (t, 0)),
            pl.BlockSpec((rows, LANES), lambda b, t: (t, 0)),
        ],
        out_specs=[hspec(A_HEADS)] * 7 + [
            hspec(R_PAIRS), hspec(R_PAIRS), hspec(R_HEADS), hspec(R_HEADS),
            pl.BlockSpec((1, rows, 2 * D), lambda b, t: (b, t, 0)), stat_spec, stat_spec],
        out_shape=[hd(A_HEADS, BF16), hd(A_HEADS, F32), hd(A_HEADS, BF16), hd(A_HEADS, F32),
                   hd(A_HEADS, BF16), hd(A_HEADS, BF16), hd(A_HEADS, BF16),
            hd(R_PAIRS, BF16), hd(R_PAIRS, BF16), hd(R_HEADS, BF16), hd(R_HEADS, BF16),
            jax.ShapeDtypeStruct((B, T, 2 * D), BF16), stat_shape, stat_shape],
        compiler_params=pltpu.CompilerParams(
            dimension_semantics=("arbitrary", "arbitrary"), vmem_limit_bytes=VMEM_LIMIT),
        name="inproj",
    )(x, sc1p, shift, w_in, w_rope, w_mg, b_mg, lbm, oml, cos_p, sin_p)


def _hgrn_tables(reverse):
    C = HGRN_CHUNK
    idx = np.arange(C)
    cum = (idx[None, :] >= idx[:, None]) if reverse else (idx[None, :] <= idx[:, None])
    cum = cum.astype(np.float64)
    end = 0 if reverse else C - 1
    blocks = []
    for j in range(HGRN_LEVELS):
        m = C >> (j + 1)
        base = idx - idx % (2 * m)
        ref = base + (m if reverse else m - 1)
        blocks.append(cum - cum[ref])
    blocks.append(cum)
    blocks.append(cum[end][None, :] - cum)
    safe_tab = np.concatenate(blocks, axis=0)
    mid = C // 2 if reverse else C // 2 - 1
    fast_tab = np.tile(cum - cum[mid][None, :], (1, 2))
    x = idx[:, None] ^ idx[None, :]
    bl = np.zeros_like(x)
    nz = x > 0
    bl[nz] = np.floor(np.log2(x[nz])).astype(np.int64) + 1
    lvl = HGRN_LEVELS - bl
    coupled = (idx[:, None] <= idx[None, :]) if reverse else (idx[:, None] >= idx[None, :])
    lvl = np.where(coupled, lvl, -1)
    return safe_tab.astype(np.float32), fast_tab.astype(np.float32), lvl.astype(np.int32), end


def _hgrn_kernel(q_ref, lff_ref, kf_ref, lfb_ref, kb_ref, v_ref,
                 dstat_ref, mstat_ref,
                 tabf_ref, tabb_ref, ftabf_ref, ftabb_ref, lvf_ref, lvb_ref,
                 y_ref, o_scr, sf_scr, sb_scr, ops_scr, dec_scr, a_scr, u_scr, *, ends):
    C = HGRN_CHUNK
    T = q_ref.shape[2]
    n_chunks = T // C
    sf_scr[...] = jnp.zeros_like(sf_scr)
    sb_scr[...] = jnp.zeros_like(sb_scr)

    def load(ci, gate_refs):
        rows = pl.ds(pl.multiple_of(ci * C, C), C)
        lf_ref, k_ref = gate_refs
        lf = lf_ref[0, 0, rows, :]
        hi = lf.astype(BF16)
        lo = (lf - hi.astype(F32)).astype(BF16)
        return (rows, q_ref[0, 0, rows, :], k_ref[0, 0, rows, :], v_ref[0, 0, rows, :],
                (lf, hi, lo))

    def advance(s_scr, a, v, qi, kd, s_decay):
        st = s_scr[...]
        o = _dot(a.astype(BF16), v) + _dot_nt(qi, st.astype(BF16))
        s_scr[...] = st * s_decay + _dot_tn(v, kd)
        return o


    def chunk_rows(ci):
        return pl.ds(pl.multiple_of(ci * C, C), C)

    def prepare_task(c, ci, p, slot):
        start = C - 1 - p[4]

        def issue():
            _, q, k, _, (lf, hi, lo) = load(ci, p[0])
            return q, k, lf[start:start + 1], _dot(p[1][...], jnp.concatenate([hi, lo], axis=0))

        def consume(issued):
            q, k, lf_start, d = issued
            b_mid = lf_start - d[start:start + 1]
            d_end = d[p[4]:p[4] + 1]
            qh = q * jnp.exp(d).astype(BF16)
            kh = k * jnp.exp(-d).astype(BF16)
            ops_scr[slot, c, 0] = qh
            ops_scr[slot, c, 1] = kh
            ops_scr[slot, c, 2] = qh * jnp.exp(b_mid).astype(BF16)
            ops_scr[slot, c, 3] = kh * jnp.exp(d_end).astype(BF16)
            dec_scr[slot, c] = jnp.broadcast_to(jnp.exp(b_mid + d_end), (8, LANES))
        return issue, consume

    def scores_task(c, ci, p, ops_slot, slot):
        def issue():
            return (_dot_nt(ops_scr[ops_slot, c, 0], ops_scr[ops_slot, c, 1]),
                    _dot_tn(v_ref[0, 0, chunk_rows(ci), :], ops_scr[ops_slot, c, 3]))

        def consume(issued):
            a_scr[slot, c] = jnp.where(p[2][...] >= 0, issued[0], 0.0).astype(BF16)
            u_scr[slot, c] = issued[1]
        return issue, consume

    def finish_task(c, ci, p, ops_slot, slot, finish):
        rows = chunk_rows(ci)

        def issue():
            st = p[3][...]
            o = (_dot(a_scr[slot, c], v_ref[0, 0, rows, :])
                 + _dot_nt(ops_scr[ops_slot, c, 2], st.astype(BF16)))
            p[3][...] = st * dec_scr[ops_slot, c, 0:1, :] + u_scr[slot, c]
            return o

        def consume(o):
            finish(rows, o)
        return issue, consume

    def rolling(tasks, lag):
        pending = []
        for issue, consume in tasks:
            pending.append((consume, issue()))
            if len(pending) > lag:
                cons, res = pending.pop(0)
                cons(res)
        for cons, res in pending:
            cons(res)

    def safe_chunk(ci, lf_ref, tab_ref, lv_ref, s_scr, end):
        rows, qb, kb, v, (_, hi, lo) = load(ci, lf_ref)
        q = qb.astype(F32)
        k = kb.astype(F32)
        r = _dot(tab_ref[...], jnp.concatenate([hi, lo], axis=1))
        dm = r[:, :LANES] + r[:, LANES:]
        lv = lv_ref[...]
        a = jnp.where(lv == HGRN_LEVELS, _dot_nt(qb, kb), 0.0)
        for j in range(HGRN_LEVELS):
            e = jnp.exp(-jnp.abs(dm[j * C:(j + 1) * C]))
            aj = _dot_nt((q * e).astype(BF16), (k * e).astype(BF16))
            a = jnp.where(lv == j, aj, a)
        bq = dm[HGRN_LEVELS * C:(HGRN_LEVELS + 1) * C]
        bk = dm[(HGRN_LEVELS + 1) * C:(HGRN_LEVELS + 2) * C]
        qi = (q * jnp.exp(bq)).astype(BF16)
        kd = (k * jnp.exp(bk)).astype(BF16)
        return rows, advance(s_scr, a, v, qi, kd, jnp.exp(bq[end:end + 1, :]))

    def finalize(rows, o):
        y_ref[0, 0, rows, :] = (o_scr[rows, :] + o).astype(BF16)

    def first_visit(rows, o):
        o_scr[rows, :] = o

    def chain_groups(tf_ref, tb_ref, per_step):
        fwd = ((lff_ref, kf_ref), tf_ref, lvf_ref, sf_scr, ends[0])
        bwd = ((lfb_ref, kb_ref), tb_ref, lvb_ref, sb_scr, ends[1])

        def chains(g):
            out = []
            for u in range(per_step):
                i = g * per_step + u
                out += [(i, fwd), (n_chunks - 1 - i, bwd)]
            return out
        return chains

    def loop(lo, hi, body, finish):
        def wrapped(g, carry):
            body(g, finish)
            return carry
        lax.fori_loop(lo, hi, wrapped, 0)

    def run_safe():
        chains = chain_groups(tabf_ref, tabb_ref, 1)

        def step(g, finish):
            for ci, p in chains(g):
                finish(*safe_chunk(ci, *p))

        loop(0, n_chunks // 2, step, first_visit)
        loop(n_chunks // 2, n_chunks, step, finalize)

    def run_fast():
        per_step = ops_scr.shape[1] // 2
        chains = chain_groups(ftabf_ref, ftabb_ref, per_step)
        n_groups = n_chunks // per_step
        last = n_groups - 1

        def step(g, finish):
            g1 = jnp.minimum(g + 1, last)
            g2 = jnp.minimum(g + 2, last)
            tasks = []
            for c, ((c1, p1), (c2, p2), (c0, p0)) in enumerate(zip(chains(g1), chains(g2), chains(g))):
                tasks.append(scores_task(c, c1, p1, g1 % 3, (g + 1) % 2))
                tasks.append(prepare_task(c, c2, p2, (g + 2) % 3))
                tasks.append(finish_task(c, c0, p0, g % 3, g % 2, finish))
            rolling(tasks, HGRN_ROLL_LAG)

        fill = [prepare_task(c, ci, p, g) for g in range(2) for c, (ci, p) in enumerate(chains(g))]
        fill += [scores_task(c, ci, p, 0, 0) for c, (ci, p) in enumerate(chains(0))]
        rolling(fill, min(HGRN_ROLL_LAG, 2 * per_step))
        loop(0, n_groups // 2, step, first_visit)
        loop(n_groups // 2, n_groups, step, finalize)

    is_head = lax.broadcasted_iota(jnp.int32, (A_HEADS, LANES), 0) == pl.program_id(1)
    dec = jnp.min(jnp.where(is_head, jnp.min(dstat_ref[0], axis=0), 0.0))
    mag = jnp.max(jnp.where(is_head, jnp.max(mstat_ref[0], axis=0), 0.0))
    is_safe = jnp.logical_and(dec > -HGRN_SAFE_DECAY, mag < HGRN_SAFE_MAG)

    pl.when(is_safe)(run_fast)
    pl.when(jnp.logical_not(is_safe))(run_safe)


def _hgrn_call(qa, lff, kf, lfb, kb, va, dstat, mstat):
    B, H, T, _ = qa.shape
    stat = pl.BlockSpec((1,) + dstat.shape[1:], lambda b, h: (b, 0, 0, 0))
    tabf, ftabf, lvf, endf = _hgrn_tables(False)
    tabb, ftabb, lvb, endb = _hgrn_tables(True)
    seq = pl.BlockSpec((1, 1, T, LANES), lambda b, h: (b, h, 0, 0))
    n_chunks = T // HGRN_CHUNK
    assert n_chunks % 4 == 0, T
    per_step = max(u for u in (HGRN_FAST_UNROLL, 2, 1) if n_chunks % (4 * u) == 0)
    return pl.pallas_call(
        functools.partial(_hgrn_kernel, ends=(endf, endb)),
        grid=(B, H),
        in_specs=[seq] * 6 + [stat, stat,
                  _const_spec(tabf.shape), _const_spec(tabb.shape),
                  _const_spec(ftabf.shape), _const_spec(ftabb.shape),
                  _const_spec(lvf.shape), _const_spec(lvb.shape)],
        out_specs=seq,
        out_shape=jax.ShapeDtypeStruct((B, H, T, LANES), BF16),
        scratch_shapes=[pltpu.VMEM((T, LANES), F32),
                        pltpu.VMEM((A_DV, A_DK), F32),
                        pltpu.VMEM((A_DV, A_DK), F32),
                        pltpu.VMEM((3, 2 * per_step, 4, HGRN_CHUNK, LANES), BF16),
                        pltpu.VMEM((3, 2 * per_step, 8, LANES), F32),
                        pltpu.VMEM((2, 2 * per_step, HGRN_CHUNK, HGRN_CHUNK), BF16),
                        pltpu.VMEM((2, 2 * per_step, A_DV, A_DK), F32)],
        compiler_params=pltpu.CompilerParams(
            dimension_semantics=("arbitrary", "arbitrary"), vmem_limit_bytes=VMEM_LIMIT),
        name="hgrn",
    )(qa, lff, kf, lfb, kb, va, dstat, mstat,
      jnp.asarray(tabf, BF16), jnp.asarray(tabb, BF16),
      jnp.asarray(ftabf, BF16), jnp.asarray(ftabb, BF16), jnp.asarray(lvf), jnp.asarray(lvb))


def _ret_kernel(q_ref, k_ref, v_ref, lg_ref, y_ref,
                o_scr, s_scr, dec_scr, qd_scr, kd_scr, hm_scr, a_scr, u_scr):
    C = RET_CHUNK
    T = q_ref.shape[2]
    n_chunks = T // C
    per_step = a_scr.shape[1] // 4
    n_groups = n_chunks // per_step
    s_scr[...] = jnp.zeros_like(s_scr)

    lane = lax.broadcasted_iota(jnp.int32, (1, LANES), 1)
    head_mask = [(lane % (LANES // 2)) < (LANES // 4), (lane % (LANES // 2)) >= (LANES // 4)]
    tpos = lax.broadcasted_iota(jnp.int32, (C, C), 0)
    spos = lax.broadcasted_iota(jnp.int32, (C, C), 1)
    pos = lax.broadcasted_iota(jnp.int32, (C, 1), 0).astype(F32)
    s_dec = []
    for i in range(2):
        hm_scr[i] = jnp.broadcast_to(jnp.where(head_mask[i], 1.0, 0.0), (C, LANES)).astype(BF16)
    for d in range(2):
        dist = ((tpos - spos) if d == 0 else (spos - tpos)).astype(F32)
        after = pos if d == 0 else (C - 1.0 - pos)
        for i in range(2):
            lg = lg_ref[0, 2 * d + i:2 * d + i + 1, :]
            dec_scr[2 * d + i] = jnp.exp(jnp.where(dist >= 0, dist * lg, NEG_BIG))
            qd_scr[2 * d + i] = jnp.where(head_mask[i], jnp.exp((after + 1.0) * lg), 0.0).astype(BF16)
            kd_scr[2 * d + i] = jnp.where(head_mask[i], jnp.exp((C - 1.0 - after) * lg), 0.0).astype(BF16)
            s_dec.append(jnp.exp(C * lg))

    def chunk_rows(ci):
        return pl.ds(pl.multiple_of(ci * C, C), C)

    def units(g):
        out = []
        for u in range(per_step):
            j = g * per_step + u
            out += [(j, 0), (n_chunks - 1 - j, 1)]
        return out

    def scores_task(m, ci, d, slot):
        rows = chunk_rows(ci)

        def issue():
            q = q_ref[0, 0, rows, :]
            k = k_ref[0, 0, rows, :]
            qm = jnp.concatenate([q * hm_scr[0], q * hm_scr[1]], axis=0)
            km = k * (kd_scr[2 * d] + kd_scr[2 * d + 1])
            vv = jnp.concatenate([v_ref[0, 0, rows, :], v_ref[0, 1, rows, :]], axis=1)
            return _dot_nt(qm, k), _dot_tn(km, vv)

        def consume(issued):
            sc, upd = issued
            for i in range(2):
                a_scr[slot, 2 * m + i] = (sc[i * C:(i + 1) * C] * dec_scr[2 * d + i]).astype(BF16)
                u_scr[slot, 2 * m + i] = upd[:, i * R_DV:(i + 1) * R_DV]
        return issue, consume

    def finish_task(n, ci, d, i, slot, finish):
        rows = chunk_rows(ci)

        def issue():
            st = s_scr[2 * d + i]
            qd = q_ref[0, 0, rows, :] * qd_scr[2 * d + i]
            lhs = jnp.concatenate([a_scr[slot, n], qd], axis=1)
            rhs = jnp.concatenate([v_ref[0, i, rows, :], st.astype(BF16)], axis=0)
            s_scr[2 * d + i] = st * s_dec[2 * d + i] + u_scr[slot, n]
            return _dot(lhs, rhs)

        def consume(o):
            finish(i, rows, o)
        return issue, consume

    def rolling(tasks, lag):
        pending = []
        for issue, consume in tasks:
            pending.append((consume, issue()))
            if len(pending) > lag:
                cons, res = pending.pop(0)
                cons(res)
        for cons, res in pending:
            cons(res)

    def first_visit(i, rows, o):
        o_scr[i, rows, :] = o

    def finalize(i, rows, o):
        y_ref[0, i, rows, :] = (o_scr[i, rows, :] + o).astype(BF16)

    def step(g, finish):
        nxt = units(jnp.minimum(g + 1, n_groups - 1))
        cur = units(g)
        tasks = []
        for m, ((c1, d1), (c0, d0)) in enumerate(zip(nxt, cur)):
            tasks.append(scores_task(m, c1, d1, (g + 1) % 2))
            for i in range(2):
                tasks.append(finish_task(2 * m + i, c0, d0, i, g % 2, finish))
        rolling(tasks, RET_ROLL_LAG)

    def loop(lo, hi, finish):
        def wrapped(g, carry):
            step(g, finish)
            return carry
        lax.fori_loop(lo, hi, wrapped, 0)

    rolling([scores_task(m, ci, d, 0) for m, (ci, d) in enumerate(units(0))], RET_ROLL_LAG)
    loop(0, n_groups // 2, first_visit)
    loop(n_groups // 2, n_groups, finalize)


def _ret_call(qr, kr, vr, lg_rows):
    B, P, T, _ = qr.shape
    n_chunks = T // RET_CHUNK
    assert n_chunks % 2 == 0, T
    per_step = max(u for u in (RET_STEP_CHUNKS, 1) if n_chunks % (2 * u) == 0)
    n_units = 4 * per_step
    seq1 = pl.BlockSpec((1, 1, T, LANES), lambda b, p: (b, p, 0, 0))
    seq2 = pl.BlockSpec((1, 2, T, LANES), lambda b, p: (b, p, 0, 0))
    return pl.pallas_call(
        _ret_kernel,
        grid=(B, P),
        in_specs=[seq1, seq1, seq2,
                  pl.BlockSpec((1, 4, LANES), lambda b, p: (p, 0, 0))],
        out_specs=seq2,
        out_shape=jax.ShapeDtypeStruct((B, R_HEADS, T, LANES), BF16),
        scratch_shapes=[pltpu.VMEM((2, T, LANES), F32),
                        pltpu.VMEM((4, LANES, R_DV), F32),
                        pltpu.VMEM((4, RET_CHUNK, RET_CHUNK), F32),
                        pltpu.VMEM((4, RET_CHUNK, LANES), BF16),
                        pltpu.VMEM((4, RET_CHUNK, LANES), BF16),
                        pltpu.VMEM((2, RET_CHUNK, LANES), BF16),
                        pltpu.VMEM((2, n_units, RET_CHUNK, RET_CHUNK), BF16),
                        pltpu.VMEM((2, n_units, LANES, R_DV), F32)],
        compiler_params=pltpu.CompilerParams(
            dimension_semantics=("arbitrary", "arbitrary"), vmem_limit_bytes=VMEM_LIMIT),
        name="retention",
    )(qr, kr, vr, lg_rows)


def _outproj_kernel(oa_ref, ga_ref, or_ref, gr_ref, mg_ref, x_ref, gate_ref, nw_ref,
                    wpa_ref, wpb_ref, wout_ref, lng_ref, lnb_ref, o_ref, m_scr):
    rows = x_ref.shape[1]
    sub = min(OUT_SUB_ROWS, rows)

    def hgrn_head(h, rs):
        o = oa_ref[0, h, rs, :].astype(F32)
        ms = jnp.mean(o * o, axis=-1, keepdims=True)
        y = o * lax.rsqrt(ms + LN_EPS) * nw_ref[...] * ga_ref[0, h, rs, :].astype(F32)
        return y.astype(BF16)

    def retention_head(h, rs):
        o = or_ref[0, h, rs, :].astype(F32)
        mu = jnp.mean(o, axis=-1, keepdims=True)
        var = jnp.mean(o * o, axis=-1, keepdims=True) - mu * mu
        return ((o - mu) * lax.rsqrt(var + LN_EPS) * gr_ref[0, h, rs, :].astype(F32)).astype(BF16)

    def branch_task(r):
        rs = slice(r * sub, (r + 1) * sub)

        def issue():
            ya = jnp.concatenate([hgrn_head(h, rs) for h in range(A_HEADS)], axis=1)
            yr = jnp.concatenate([retention_head(h, rs) for h in range(R_HEADS)], axis=1)
            return _dot(ya, wpa_ref[...]), _dot(yr, wpb_ref[...])

        def consume(issued):
            pa, pr = issued
            mg = mg_ref[0, rs, :]
            m = mg[:, :D_MODEL].astype(F32) * pa + mg[:, D_MODEL:].astype(F32) * pr
            m_scr[rs, :] = m.astype(BF16)
        return issue, consume

    def out_task(r):
        rs = slice(r * sub, (r + 1) * sub)

        def issue():
            return _dot(m_scr[rs, :], wout_ref[...])

        def consume(s):
            h = DEEPNORM_ALPHA * x_ref[0, rs, :] + gate_ref[0] * s
            mu = jnp.mean(h, axis=-1, keepdims=True)
            cen = h - mu
            var = jnp.mean(cen * cen, axis=-1, keepdims=True)
            o_ref[0, rs, :] = cen * lax.rsqrt(var + LN_EPS) * lng_ref[...] + lnb_ref[...]
        return issue, consume

    n_sub = rows // sub
    tasks = [branch_task(0)]
    for r in range(n_sub):
        if r + 1 < n_sub:
            tasks.append(branch_task(r + 1))
        tasks.append(out_task(r))
    lag = 1 if n_sub > 1 else 0
    pending = []
    for issue, consume in tasks:
        pending.append((consume, issue()))
        if len(pending) > lag:
            cons, res = pending.pop(0)
            cons(res)
    for cons, res in pending:
        cons(res)


def _outproj_call(o_a, ga, o_r, gr, mg, x, gate1p, nw, w_pa, w_pb, w_out, ln_g, ln_b):
    B, T, D = x.shape
    rows = min(OUT_ROWS, T)
    hspec = pl.BlockSpec((1, A_HEADS, rows, LANES), lambda b, t: (b, 0, t, 0))
    return pl.pallas_call(
        _outproj_kernel,
        grid=(B, T // rows),
        in_specs=[hspec, hspec, hspec, hspec,
                  pl.BlockSpec((1, rows, 2 * D), lambda b, t: (b, t, 0)),
                  pl.BlockSpec((1, rows, D), lambda b, t: (b, t, 0)),
                  pl.BlockSpec((1, 1, D), lambda b, t: (b, 0, 0)),
                  _const_spec((1, LANES)),
                  _const_spec((D, D)), _const_spec((D, D)), _const_spec((D, D)),
                  _const_spec((1, D)), _const_spec((1, D))],
        out_specs=pl.BlockSpec((1, rows, D), lambda b, t: (b, t, 0)),
        out_shape=jax.ShapeDtypeStruct((B, T, D), F32),
        scratch_shapes=[pltpu.VMEM((rows, D), BF16)],
        compiler_params=pltpu.CompilerParams(
            dimension_semantics=("arbitrary", "arbitrary"), vmem_limit_bytes=VMEM_LIMIT),
        name="outproj",
    )(o_a, ga, o_r, gr, mg, x, gate1p, nw, w_pa, w_pb, w_out, ln_g, ln_b)


def _rope_tables(T):
    half = R_DK // 2
    pos = jnp.arange(T, dtype=F32)
    inv = ROPE_BASE ** (-jnp.arange(0, R_DK, 2, dtype=F32) / R_DK)
    ang = pos[:, None] * inv[None, :]
    cos, sin = jnp.cos(ang), jnp.sin(ang)
    return jnp.tile(cos, (1, LANES // half)), jnp.concatenate([-sin, -sin, sin, sin], axis=1)


def _trunk(x, ada, params):
    B, T, D = x.shape
    cos_p, sin_p = _rope_tables(T)
    for l in range(DEPTH):
        p = params[l]
        shift = ada[l, :, :D].reshape(B, 1, D)
        sc1p = (1.0 + ada[l, :, D:2 * D]).reshape(B, 1, D)
        gate1p = (1.0 + ada[l, :, 2 * D:]).reshape(B, 1, D)
        qa, lff, kf, lfb, kb, va, ga, qr, kr, vr, gr, mg, dstat, mstat = _inproj_call(
            x, sc1p, shift, p["w_in"], p["w_rope"], p["w_mg"], p["b_mg"], p["lbm"], p["oml"],
            cos_p, sin_p)
        o_a = _hgrn_call(qa, lff, kf, lfb, kb, va, dstat, mstat)
        o_r = _ret_call(qr, kr, vr, p["lg_rows"])
        x = _outproj_call(o_a, ga, o_r, gr, mg, x, gate1p, p["nw"], p["w_pa"], p["w_pb"], p["w_out"],
                          p["ln_g"], p["ln_b"])
    return x


def kernel(x_prompt, x_sample, c_prompt, c_sample, w_ada, b_ada, w_in, hgrn_lb, a_norm_w, ret_decay,
           w_pa, w_pb, w_mg, b_mg, w_out, ln_g, ln_b):
    D = D_MODEL
    pr = jax.nn.softmax(hgrn_lb.astype(F32), axis=0)
    lbs = jnp.cumsum(pr, axis=0) - pr[0:1]
    params = []
    for l in range(DEPTH):
        lb = lbs[l]
        lbm = jnp.maximum(lb, LOG_FLOOR)
        lg = jax.nn.log_sigmoid(ret_decay[l].astype(F32))
        lg_rows = jnp.broadcast_to(
            lg.reshape(2, R_PAIRS, 2).transpose(1, 0, 2).reshape(R_PAIRS, 4, 1), (R_PAIRS, 4, LANES))
        half = R_DK // 2
        w_rope = w_in[l][:, OFF_RQ:OFF_RV].reshape(D, 2, R_PAIRS, 2, 2, half)
        w_rope = w_rope.transpose(0, 1, 2, 4, 3, 5).reshape(D, 2 * R_HEADS * R_DK)
        params.append(dict(
            w_in=w_in[l].astype(BF16), w_rope=w_rope.astype(BF16), w_mg=w_mg[l].astype(BF16),
            w_pa=w_pa[l].astype(BF16), w_pb=w_pb[l].astype(BF16), w_out=w_out[l].astype(BF16),
            b_mg=b_mg[l].reshape(1, 2 * D), lbm=lbm.reshape(1, D), oml=(1.0 - lb).reshape(1, D),
            nw=a_norm_w[l].reshape(1, A_DV), lg_rows=lg_rows,
            ln_g=ln_g[l].reshape(1, D), ln_b=ln_b[l].reshape(1, D)))
    nbp = x_prompt.shape[0]
    ada = _ada_call(jnp.concatenate([c_prompt, c_sample], axis=0), w_ada, b_ada)
    y_prompt = _trunk(x_prompt, ada[:, :nbp], params)
    y_sample = _trunk(x_sample, ada[:, nbp:], params)
    return (y_prompt, y_sample)
```

```python
import functools

import numpy as np
import jax
import jax.numpy as jnp
from jax import lax
from jax.experimental import pallas as pl
from jax.experimental.pallas import tpu as pltpu

F32 = jnp.float32
BF16 = jnp.bfloat16

D_MODEL = 1024
DEPTH = 2
A_HEADS = 8
A_DK = 128
A_DV = 128
R_HEADS = 8
R_DK = 64
R_DV = 128
R_PAIRS = R_HEADS // 2
LN_EPS = 1e-5
LOG_FLOOR = 1e-30
NEG_BIG = -1e30
DEEPNORM_ALPHA = (2.0 * DEPTH) ** 0.25
ROPE_BASE = 10000.0

OFF_AQ, OFF_FF, OFF_FB, OFF_AI, OFF_AG, OFF_RQ, OFF_RK, OFF_RV, OFF_RG = (
    0, 1024, 2048, 3072, 4096, 5120, 5632, 6144, 7168)
IN_WIDTH = 8192

LANES = 128
HGRN_CHUNK = 64
HGRN_LEVELS = 6
RET_CHUNK = 128
HGRN_SAFE_DECAY = 75.0
HGRN_SAFE_MAG = 1e5
HGRN_FAST_UNROLL = 8
IN_ROLL_LAG = 1
HGRN_ROLL_LAG = 12
RET_STEP_CHUNKS = 4
RET_ROLL_LAG = 4
IN_ROWS = 512
IN_SUB_ROWS = 128
OUT_ROWS = 1024
OUT_SUB_ROWS = 256
VMEM_LIMIT = 60 * 1024 * 1024


def _dot(a, b):
    return jnp.dot(a, b, preferred_element_type=F32)


def _dot_nt(a, b):
    return lax.dot_general(a, b, (((1,), (1,)), ((), ())), preferred_element_type=F32)


def _dot_tn(a, b):
    return lax.dot_general(a, b, (((0,), (0,)), ((), ())), preferred_element_type=F32)


def _sigmoid(z):
    return 1.0 / (1.0 + jnp.exp(-z))


def _const_spec(shape):
    nd = len(shape)
    return pl.BlockSpec(shape, lambda *_: (0,) * nd, pipeline_mode=pl.Buffered(1))


def _ada_kernel(c_ref, w_ref, b_ref, o_ref):
    c = c_ref[...]
    a = c * _sigmoid(c)
    o_ref[0] = jnp.dot(a, w_ref[0], preferred_element_type=F32,
                       precision=lax.Precision.HIGHEST) + b_ref[0]


def _ada_call(c, w_ada, b_ada):
    nb = c.shape[0]
    d = D_MODEL
    return pl.pallas_call(
        _ada_kernel,
        grid=(DEPTH, 3),
        in_specs=[
            pl.BlockSpec((nb, d), lambda l, j: (0, 0)),
            pl.BlockSpec((1, d, d), lambda l, j: (l, 0, j)),
            pl.BlockSpec((1, 1, d), lambda l, j: (l, 0, j)),
        ],
        out_specs=pl.BlockSpec((1, nb, d), lambda l, j: (l, 0, j)),
        out_shape=jax.ShapeDtypeStruct((DEPTH, nb, 3 * d), F32),
        compiler_params=pltpu.CompilerParams(
            dimension_semantics=("arbitrary", "arbitrary"), vmem_limit_bytes=VMEM_LIMIT),
        name="adaln",
    )(c, w_ada, b_ada.reshape(DEPTH, 1, 3 * d))


def _inproj_kernel(x_ref, sc_ref, sh_ref, win_ref, wrope_ref, wmg_ref, bmg_ref, lbm_ref, oml_ref,
                   cos_ref, sin_ref,
                   qa_ref, lff_ref, kf_ref, lfb_ref, kb_ref, va_ref, ga_ref,
                   qr_ref, kr_ref, vr_ref, gr_ref, mg_ref, dstat_ref, mstat_ref):
    rows = x_ref.shape[1]
    sub = min(IN_SUB_ROWS, rows)
    n_sub = rows // sub
    row_slices = [slice(s * sub, (s + 1) * sub) for s in range(n_sub)]
    u = [(x_ref[0, rs, :] * sc_ref[0] + sh_ref[0]).astype(BF16) for rs in row_slices]

    def col_max(z):
        return jnp.max(jnp.abs(z), axis=0, keepdims=True)

    def store_heads(ref, g, rs, z):
        for i in range(2):
            ref[0, 2 * g + i, rs, :] = z[:, i * LANES:(i + 1) * LANES].astype(BF16)

    stats = {}

    def fold_stat(g, name, value, reduce):
        cur = stats.setdefault(g, {})
        cur[name] = value if name not in cur else reduce(cur[name], value)

    def q_epilogue(g, rs, z):
        z = z * _sigmoid(z) * (A_DK ** -0.5)
        fold_stat(g, "mag", col_max(z), jnp.maximum)
        store_heads(qa_ref, g, rs, z)

    def forget_epilogue(g, lf_ref, k_ref, rs, z):
        lbm = lbm_ref[:, 2 * LANES * g:2 * LANES * (g + 1)]
        oml = oml_ref[:, 2 * LANES * g:2 * LANES * (g + 1)]
        f = lbm + oml * _sigmoid(z)
        lf = jnp.log(f)
        halfsum = jnp.sum(lf.reshape(2 * sub // HGRN_CHUNK, HGRN_CHUNK // 2, 2 * LANES), axis=1)
        fold_stat(g, "dmin", jnp.min(halfsum, axis=0, keepdims=True), jnp.minimum)
        for i in range(2):
            lf_ref[0, 2 * g + i, rs, :] = lf[:, i * LANES:(i + 1) * LANES]
        store_heads(k_ref, g, rs, (oml + lbm) - f)

    def v_epilogue(g, last, rs, z):
        fold_stat(g, "mag", col_max(z), jnp.maximum)
        store_heads(va_ref, g, rs, z)
        if last:
            cur = stats.pop(g)
            for i in range(2):
                lanes = slice(i * LANES, (i + 1) * LANES)
                dstat_ref[0, 0, 2 * g + i:2 * g + i + 1, :] = cur["dmin"][:, lanes]
                mstat_ref[0, 0, 2 * g + i:2 * g + i + 1, :] = cur["mag"][:, lanes]

    def silu_epilogue(ref, g, rs, z):
        store_heads(ref, g, rs, z * _sigmoid(z))

    def rope_epilogue(ref, g, scale, rs, z):
        for i in range(2):
            zz = z[:, i * LANES:(i + 1) * LANES]
            rot = pltpu.roll(zz, LANES // 2, 1)
            ref[0, 2 * g + i, rs, :] = ((zz * cos_ref[rs, :] + rot * sin_ref[rs, :]) * scale).astype(BF16)

    def merge_epilogue(g, rs, z):
        c = 2 * LANES * g
        mg_ref[0, rs, c:c + 2 * LANES] = _sigmoid(z + bmg_ref[:, c:c + 2 * LANES]).astype(BF16)

    part = functools.partial

    def column_tasks(last):
        tasks = []
        for g in range(A_HEADS // 2):
            c = 2 * LANES * g
            tasks += [(win_ref, OFF_AQ + c, part(q_epilogue, g)),
                      (win_ref, OFF_FF + c, part(forget_epilogue, g, lff_ref, kf_ref)),
                      (win_ref, OFF_RV + c, part(store_heads, vr_ref, g)),
                      (win_ref, OFF_FB + c, part(forget_epilogue, g, lfb_ref, kb_ref)),
                      (win_ref, OFF_AI + c, part(v_epilogue, g, last)),
                      (win_ref, OFF_AG + c, part(silu_epilogue, ga_ref, g)),
                      (win_ref, OFF_RG + c, part(silu_epilogue, gr_ref, g))]
        for g in range(R_PAIRS // 2):
            c = 2 * LANES * g
            tasks += [(wrope_ref, c, part(rope_epilogue, qr_ref, g, R_DK ** -0.5)),
                      (wrope_ref, R_HEADS * R_DK + c, part(rope_epilogue, kr_ref, g, 1.0))]
        for g in range(2 * D_MODEL // (2 * LANES)):
            tasks.append((wmg_ref, 2 * LANES * g, part(merge_epilogue, g)))
        return tasks

    per_sub = [column_tasks(s == n_sub - 1) for s in range(n_sub)]
    pending = []
    for column in zip(*per_sub):
        for s, (w_ref, c0, epilogue) in enumerate(column):
            z = _dot(u[s], w_ref[:, c0:c0 + 2 * LANES])
            pending.append((part(epilogue, row_slices[s]), z))
            if len(pending) > IN_ROLL_LAG:
                ep, zz = pending.pop(0)
                ep(zz)
    for ep, zz in pending:
        ep(zz)
    assert not stats


def _inproj_call(x, sc1p, shift, w_in, w_rope, w_mg, b_mg, lbm, oml, cos_p, sin_p):
    B, T, D = x.shape
    rows = min(IN_ROWS, T)
    grid = (B, T // rows)
    hd = lambda n, dt: jax.ShapeDtypeStruct((B, n, T, LANES), dt)
    hspec = lambda n: pl.BlockSpec((1, n, rows, LANES), lambda b, t: (b, 0, t, 0))
    bvec = pl.BlockSpec((1, 1, D), lambda b, t: (b, 0, 0))
    stat_spec = pl.BlockSpec((1, 1, A_HEADS, LANES), lambda b, t: (b, t, 0, 0))
    stat_shape = jax.ShapeDtypeStruct((B, T // rows, A_HEADS, LANES), F32)
    return pl.pallas_call(
        _inproj_kernel,
        grid=grid,
        in_specs=[
            pl.BlockSpec((1, rows, D), lambda b, t: (b, t, 0)),
            bvec, bvec,
            _const_spec((D, IN_WIDTH)),
            _const_spec((D, 2 * R_HEADS * R_DK)),
            _const_spec((D, 2 * D)),
            _const_spec((1, 2 * D)),
            _const_spec((1, D)),
            _const_spec((1, D)),
            pl.BlockSpec((rows, LANES), lambda b, t: (t, 0)),
            pl.BlockSpec((rows, LANES), lambda b, t: (t, 0)),
        ],
        out_specs=[hspec(A_HEADS)] * 7 + [
            hspec(R_PAIRS), hspec(R_PAIRS), hspec(R_HEADS), hspec(R_HEADS),
            pl.BlockSpec((1, rows, 2 * D), lambda b, t: (b, t, 0)), stat_spec, stat_spec],
        out_shape=[hd(A_HEADS, BF16), hd(A_HEADS, F32), hd(A_HEADS, BF16), hd(A_HEADS, F32),
                   hd(A_HEADS, BF16), hd(A_HEADS, BF16), hd(A_HEADS, BF16),
            hd(R_PAIRS, BF16), hd(R_PAIRS, BF16), hd(R_HEADS, BF16), hd(R_HEADS, BF16),
            jax.ShapeDtypeStruct((B, T, 2 * D), BF16), stat_shape, stat_shape],
        compiler_params=pltpu.CompilerParams(
            dimension_semantics=("arbitrary", "arbitrary"), vmem_limit_bytes=VMEM_LIMIT),
        name="inproj",
    )(x, sc1p, shift, w_in, w_rope, w_mg, b_mg, lbm, oml, cos_p, sin_p)


def _hgrn_tables(reverse):
    C = HGRN_CHUNK
    idx = np.arange(C)
    cum = (idx[None, :] >= idx[:, None]) if reverse else (idx[None, :] <= idx[:, None])
    cum = cum.astype(np.float64)
    end = 0 if reverse else C - 1
    blocks = []
    for j in range(HGRN_LEVELS):
        m = C >> (j + 1)
        base = idx - idx % (2 * m)
        ref = base + (m if reverse else m - 1)
        blocks.append(cum - cum[ref])
    blocks.append(cum)
    blocks.append(cum[end][None, :] - cum)
    safe_tab = np.concatenate(blocks, axis=0)
    mid = C // 2 if reverse else C // 2 - 1
    fast_tab = np.tile(cum - cum[mid][None, :], (1, 2))
    x = idx[:, None] ^ idx[None, :]
    bl = np.zeros_like(x)
    nz = x > 0
    bl[nz] = np.floor(np.log2(x[nz])).astype(np.int64) + 1
    lvl = HGRN_LEVELS - bl
    coupled = (idx[:, None] <= idx[None, :]) if reverse else (idx[:, None] >= idx[None, :])
    lvl = np.where(coupled, lvl, -1)
    return safe_tab.astype(np.float32), fast_tab.astype(np.float32), lvl.astype(np.int32), end


def _hgrn_kernel(q_ref, lff_ref, kf_ref, lfb_ref, kb_ref, v_ref,
                 dstat_ref, mstat_ref,
                 tabf_ref, tabb_ref, ftabf_ref, ftabb_ref, lvf_ref, lvb_ref,
                 y_ref, o_scr, sf_scr, sb_scr, ops_scr, dec_scr, a_scr, u_scr, *, ends):
    C = HGRN_CHUNK
    T = q_ref.shape[2]
    n_chunks = T // C
    sf_scr[...] = jnp.zeros_like(sf_scr)
    sb_scr[...] = jnp.zeros_like(sb_scr)

    def load(ci, gate_refs):
        rows = pl.ds(pl.multiple_of(ci * C, C), C)
        lf_ref, k_ref = gate_refs
        lf = lf_ref[0, 0, rows, :]
        hi = lf.astype(BF16)
        lo = (lf - hi.astype(F32)).astype(BF16)
        return (rows, q_ref[0, 0, rows, :], k_ref[0, 0, rows, :], v_ref[0, 0, rows, :],
                (lf, hi, lo))

    def advance(s_scr, a, v, qi, kd, s_decay):
        st = s_scr[...]
        o = _dot(a.astype(BF16), v) + _dot_nt(qi, st.astype(BF16))
        s_scr[...] = st * s_decay + _dot_tn(v, kd)
        return o


    def chunk_rows(ci):
        return pl.ds(pl.multiple_of(ci * C, C), C)

    def prepare_task(c, ci, p, slot):
        start = C - 1 - p[4]

        def issue():
            _, q, k, _, (lf, hi, lo) = load(ci, p[0])
            return q, k, lf[start:start + 1], _dot(p[1][...], jnp.concatenate([hi, lo], axis=0))

        def consume(issued):
            q, k, lf_start, d = issued
            b_mid = lf_start - d[start:start + 1]
            d_end = d[p[4]:p[4] + 1]
            qh = q * jnp.exp(d).astype(BF16)
            kh = k * jnp.exp(-d).astype(BF16)
            ops_scr[slot, c, 0] = qh
            ops_scr[slot, c, 1] = kh
            ops_scr[slot, c, 2] = qh * jnp.exp(b_mid).astype(BF16)
            ops_scr[slot, c, 3] = kh * jnp.exp(d_end).astype(BF16)
            dec_scr[slot, c] = jnp.broadcast_to(jnp.exp(b_mid + d_end), (8, LANES))
        return issue, consume

    def scores_task(c, ci, p, ops_slot, slot):
        def issue():
            return (_dot_nt(ops_scr[ops_slot, c, 0], ops_scr[ops_slot, c, 1]),
                    _dot_tn(v_ref[0, 0, chunk_rows(ci), :], ops_scr[ops_slot, c, 3]))

        def consume(issued):
            a_scr[slot, c] = jnp.where(p[2][...] >= 0, issued[0], 0.0).astype(BF16)
            u_scr[slot, c] = issued[1]
        return issue, consume

    def finish_task(c, ci, p, ops_slot, slot, finish):
        rows = chunk_rows(ci)

        def issue():
            st = p[3][...]
            o = (_dot(a_scr[slot, c], v_ref[0, 0, rows, :])
                 + _dot_nt(ops_scr[ops_slot, c, 2], st.astype(BF16)))
            p[3][...] = st * dec_scr[ops_slot, c, 0:1, :] + u_scr[slot, c]
            return o

        def consume(o):
            finish(rows, o)
        return issue, consume

    def rolling(tasks, lag):
        pending = []
        for issue, consume in tasks:
            pending.append((consume, issue()))
            if len(pending) > lag:
                cons, res = pending.pop(0)
                cons(res)
        for cons, res in pending:
            cons(res)

    def safe_chunk(ci, lf_ref, tab_ref, lv_ref, s_scr, end):
        rows, qb, kb, v, (_, hi, lo) = load(ci, lf_ref)
        q = qb.astype(F32)
        k = kb.astype(F32)
        r = _dot(tab_ref[...], jnp.concatenate([hi, lo], axis=1))
        dm = r[:, :LANES] + r[:, LANES:]
        lv = lv_ref[...]
        a = jnp.where(lv == HGRN_LEVELS, _dot_nt(qb, kb), 0.0)
        for j in range(HGRN_LEVELS):
            e = jnp.exp(-jnp.abs(dm[j * C:(j + 1) * C]))
            aj = _dot_nt((q * e).astype(BF16), (k * e).astype(BF16))
            a = jnp.where(lv == j, aj, a)
        bq = dm[HGRN_LEVELS * C:(HGRN_LEVELS + 1) * C]
        bk = dm[(HGRN_LEVELS + 1) * C:(HGRN_LEVELS + 2) * C]
        qi = (q * jnp.exp(bq)).astype(BF16)
        kd = (k * jnp.exp(bk)).astype(BF16)
        return rows, advance(s_scr, a, v, qi, kd, jnp.exp(bq[end:end + 1, :]))

    def finalize(rows, o):
        y_ref[0, 0, rows, :] = (o_scr[rows, :] + o).astype(BF16)

    def first_visit(rows, o):
        o_scr[rows, :] = o

    def chain_groups(tf_ref, tb_ref, per_step):
        fwd = ((lff_ref, kf_ref), tf_ref, lvf_ref, sf_scr, ends[0])
        bwd = ((lfb_ref, kb_ref), tb_ref, lvb_ref, sb_scr, ends[1])

        def chains(g):
            out = []
            for u in range(per_step):
                i = g * per_step + u
                out += [(i, fwd), (n_chunks - 1 - i, bwd)]
            return out
        return chains

    def loop(lo, hi, body, finish):
        def wrapped(g, carry):
            body(g, finish)
            return carry
        lax.fori_loop(lo, hi, wrapped, 0)

    def run_safe():
        chains = chain_groups(tabf_ref, tabb_ref, 1)

        def step(g, finish):
            for ci, p in chains(g):
                finish(*safe_chunk(ci, *p))

        loop(0, n_chunks // 2, step, first_visit)
        loop(n_chunks // 2, n_chunks, step, finalize)

    def run_fast():
        per_step = ops_scr.shape[1] // 2
        chains = chain_groups(ftabf_ref, ftabb_ref, per_step)
        n_groups = n_chunks // per_step
        last = n_groups - 1

        def step(g, finish):
            g1 = jnp.minimum(g + 1, last)
            g2 = jnp.minimum(g + 2, last)
            tasks = []
            for c, ((c1, p1), (c2, p2), (c0, p0)) in enumerate(zip(chains(g1), chains(g2), chains(g))):
                tasks.append(scores_task(c, c1, p1, g1 % 3, (g + 1) % 2))
                tasks.append(prepare_task(c, c2, p2, (g + 2) % 3))
                tasks.append(finish_task(c, c0, p0, g % 3, g % 2, finish))
            rolling(tasks, HGRN_ROLL_LAG)

        fill = [prepare_task(c, ci, p, g) for g in range(2) for c, (ci, p) in enumerate(chains(g))]
        fill += [scores_task(c, ci, p, 0, 0) for c, (ci, p) in enumerate(chains(0))]
        rolling(fill, min(HGRN_ROLL_LAG, 2 * per_step))
        loop(0, n_groups // 2, step, first_visit)
        loop(n_groups // 2, n_groups, step, finalize)

    is_head = lax.broadcasted_iota(jnp.int32, (A_HEADS, LANES), 0) == pl.program_id(1)
    dec = jnp.min(jnp.where(is_head, jnp.min(dstat_ref[0], axis=0), 0.0))
    mag = jnp.max(jnp.where(is_head, jnp.max(mstat_ref[0], axis=0), 0.0))
    is_safe = jnp.logical_and(dec > -HGRN_SAFE_DECAY, mag < HGRN_SAFE_MAG)

    pl.when(is_safe)(run_fast)
    pl.when(jnp.logical_not(is_safe))(run_safe)


def _hgrn_call(qa, lff, kf, lfb, kb, va, dstat, mstat):
    B, H, T, _ = qa.shape
    stat = pl.BlockSpec((1,) + dstat.shape[1:], lambda b, h: (b, 0, 0, 0))
    tabf, ftabf, lvf, endf = _hgrn_tables(False)
    tabb, ftabb, lvb, endb = _hgrn_tables(True)
    seq = pl.BlockSpec((1, 1, T, LANES), lambda b, h: (b, h, 0, 0))
    n_chunks = T // HGRN_CHUNK
    assert n_chunks % 4 == 0, T
    per_step = max(u for u in (HGRN_FAST_UNROLL, 2, 1) if n_chunks % (4 * u) == 0)
    return pl.pallas_call(
        functools.partial(_hgrn_kernel, ends=(endf, endb)),
        grid=(B, H),
        in_specs=[seq] * 6 + [stat, stat,
                  _const_spec(tabf.shape), _const_spec(tabb.shape),
                  _const_spec(ftabf.shape), _const_spec(ftabb.shape),
                  _const_spec(lvf.shape), _const_spec(lvb.shape)],
        out_specs=seq,
        out_shape=jax.ShapeDtypeStruct((B, H, T, LANES), BF16),
        scratch_shapes=[pltpu.VMEM((T, LANES), F32),
                        pltpu.VMEM((A_DV, A_DK), F32),
                        pltpu.VMEM((A_DV, A_DK), F32),
                        pltpu.VMEM((3, 2 * per_step, 4, HGRN_CHUNK, LANES), BF16),
                        pltpu.VMEM((3, 2 * per_step, 8, LANES), F32),
                        pltpu.VMEM((2, 2 * per_step, HGRN_CHUNK, HGRN_CHUNK), BF16),
                        pltpu.VMEM((2, 2 * per_step, A_DV, A_DK), F32)],
        compiler_params=pltpu.CompilerParams(
            dimension_semantics=("arbitrary", "arbitrary"), vmem_limit_bytes=VMEM_LIMIT),
        name="hgrn",
    )(qa, lff, kf, lfb, kb, va, dstat, mstat,
      jnp.asarray(tabf, BF16), jnp.asarray(tabb, BF16),
      jnp.asarray(ftabf, BF16), jnp.asarray(ftabb, BF16), jnp.asarray(lvf), jnp.asarray(lvb))


def _ret_kernel(q_ref, k_ref, v_ref, lg_ref, y_ref,
                o_scr, s_scr, dec_scr, qd_scr, kd_scr, hm_scr, a_scr, u_scr):
    C = RET_CHUNK
    T = q_ref.shape[2]
    n_chunks = T // C
    per_step = a_scr.shape[1] // 4
    n_groups = n_chunks // per_step
    s_scr[...] = jnp.zeros_like(s_scr)

    lane = lax.broadcasted_iota(jnp.int32, (1, LANES), 1)
    head_mask = [(lane % (LANES // 2)) < (LANES // 4), (lane % (LANES // 2)) >= (LANES // 4)]
    tpos = lax.broadcasted_iota(jnp.int32, (C, C), 0)
    spos = lax.broadcasted_iota(jnp.int32, (C, C), 1)
    pos = lax.broadcasted_iota(jnp.int32, (C, 1), 0).astype(F32)
    s_dec = []
    for i in range(2):
        hm_scr[i] = jnp.broadcast_to(jnp.where(head_mask[i], 1.0, 0.0), (C, LANES)).astype(BF16)
    for d in range(2):
        dist = ((tpos - spos) if d == 0 else (spos - tpos)).astype(F32)
        after = pos if d == 0 else (C - 1.0 - pos)
        for i in range(2):
            lg = lg_ref[0, 2 * d + i:2 * d + i + 1, :]
            dec_scr[2 * d + i] = jnp.exp(jnp.where(dist >= 0, dist * lg, NEG_BIG))
            qd_scr[2 * d + i] = jnp.where(head_mask[i], jnp.exp((after + 1.0) * lg), 0.0).astype(BF16)
            kd_scr[2 * d + i] = jnp.where(head_mask[i], jnp.exp((C - 1.0 - after) * lg), 0.0).astype(BF16)
            s_dec.append(jnp.exp(C * lg))

    def chunk_rows(ci):
        return pl.ds(pl.multiple_of(ci * C, C), C)

    def units(g):
        out = []
        for u in range(per_step):
            j = g * per_step + u
            out += [(j, 0), (n_chunks - 1 - j, 1)]
        return out

    def scores_task(m, ci, d, slot):
        rows = chunk_rows(ci)

        def issue():
            q = q_ref[0, 0, rows, :]
            k = k_ref[0, 0, rows, :]
            qm = jnp.concatenate([q * hm_scr[0], q * hm_scr[1]], axis=0)
            km = k * (kd_scr[2 * d] + kd_scr[2 * d + 1])
            vv = jnp.concatenate([v_ref[0, 0, rows, :], v_ref[0, 1, rows, :]], axis=1)
            return _dot_nt(qm, k), _dot_tn(km, vv)

        def consume(issued):
            sc, upd = issued
            for i in range(2):
                a_scr[slot, 2 * m + i] = (sc[i * C:(i + 1) * C] * dec_scr[2 * d + i]).astype(BF16)
                u_scr[slot, 2 * m + i] = upd[:, i * R_DV:(i + 1) * R_DV]
        return issue, consume

    def finish_task(n, ci, d, i, slot, finish):
        rows = chunk_rows(ci)

        def issue():
            st = s_scr[2 * d + i]
            qd = q_ref[0, 0, rows, :] * qd_scr[2 * d + i]
            lhs = jnp.concatenate([a_scr[slot, n], qd], axis=1)
            rhs = jnp.concatenate([v_ref[0, i, rows, :], st.astype(BF16)], axis=0)
            s_scr[2 * d + i] = st * s_dec[2 * d + i] + u_scr[slot, n]
            return _dot(lhs, rhs)

        def consume(o):
            finish(i, rows, o)
        return issue, consume

    def rolling(tasks, lag):
        pending = []
        for issue, consume in tasks:
            pending.append((consume, issue()))
            if len(pending) > lag:
                cons, res = pending.pop(0)
                cons(res)
        for cons, res in pending:
            cons(res)

    def first_visit(i, rows, o):
        o_scr[i, rows, :] = o

    def finalize(i, rows, o):
        y_ref[0, i, rows, :] = (o_scr[i, rows, :] + o).astype(BF16)

    def step(g, finish):
        nxt = units(jnp.minimum(g + 1, n_groups - 1))
        cur = units(g)
        tasks = []
        for m, ((c1, d1), (c0, d0)) in enumerate(zip(nxt, cur)):
            tasks.append(scores_task(m, c1, d1, (g + 1) % 2))
            for i in range(2):
                tasks.append(finish_task(2 * m + i, c0, d0, i, g % 2, finish))
        rolling(tasks, RET_ROLL_LAG)

    def loop(lo, hi, finish):
        def wrapped(g, carry):
            step(g, finish)
            return carry
        lax.fori_loop(lo, hi, wrapped, 0)

    rolling([scores_task(m, ci, d, 0) for m, (ci, d) in enumerate(units(0))], RET_ROLL_LAG)
    loop(0, n_groups // 2, first_visit)
    loop(n_groups // 2, n_groups, finalize)


def _ret_call(qr, kr, vr, lg_rows):
    B, P, T, _ = qr.shape
    n_chunks = T // RET_CHUNK
    assert n_chunks % 2 == 0, T
    per_step = max(u for u in (RET_STEP_CHUNKS, 1) if n_chunks % (2 * u) == 0)
    n_units = 4 * per_step
    seq1 = pl.BlockSpec((1, 1, T, LANES), lambda b, p: (b, p, 0, 0))
    seq2 = pl.BlockSpec((1, 2, T, LANES), lambda b, p: (b, p, 0, 0))
    return pl.pallas_call(
        _ret_kernel,
        grid=(B, P),
        in_specs=[seq1, seq1, seq2,
                  pl.BlockSpec((1, 4, LANES), lambda b, p: (p, 0, 0))],
        out_specs=seq2,
        out_shape=jax.ShapeDtypeStruct((B, R_HEADS, T, LANES), BF16),
        scratch_shapes=[pltpu.VMEM((2, T, LANES), F32),
                        pltpu.VMEM((4, LANES, R_DV), F32),
                        pltpu.VMEM((4, RET_CHUNK, RET_CHUNK), F32),
                        pltpu.VMEM((4, RET_CHUNK, LANES), BF16),
                        pltpu.VMEM((4, RET_CHUNK, LANES), BF16),
                        pltpu.VMEM((2, RET_CHUNK, LANES), BF16),
                        pltpu.VMEM((2, n_units, RET_CHUNK, RET_CHUNK), BF16),
                        pltpu.VMEM((2, n_units, LANES, R_DV), F32)],
        compiler_params=pltpu.CompilerParams(
            dimension_semantics=("arbitrary", "arbitrary"), vmem_limit_bytes=VMEM_LIMIT),
        name="retention",
    )(qr, kr, vr, lg_rows)


def _outproj_kernel(oa_ref, ga_ref, or_ref, gr_ref, mg_ref, x_ref, gate_ref, nw_ref,
                    wpa_ref, wpb_ref, wout_ref, lng_ref, lnb_ref, o_ref, m_scr):
    rows = x_ref.shape[1]
    sub = min(OUT_SUB_ROWS, rows)

    def hgrn_head(h, rs):
        o = oa_ref[0, h, rs, :].astype(F32)
        ms = jnp.mean(o * o, axis=-1, keepdims=True)
        y = o * lax.rsqrt(ms + LN_EPS) * nw_ref[...] * ga_ref[0, h, rs, :].astype(F32)
        return y.astype(BF16)

    def retention_head(h, rs):
        o = or_ref[0, h, rs, :].astype(F32)
        mu = jnp.mean(o, axis=-1, keepdims=True)
        var = jnp.mean(o * o, axis=-1, keepdims=True) - mu * mu
        return ((o - mu) * lax.rsqrt(var + LN_EPS) * gr_ref[0, h, rs, :].astype(F32)).astype(BF16)

    def branch_task(r):
        rs = slice(r * sub, (r + 1) * sub)

        def issue():
            ya = jnp.concatenate([hgrn_head(h, rs) for h in range(A_HEADS)], axis=1)
            yr = jnp.concatenate([retention_head(h, rs) for h in range(R_HEADS)], axis=1)
            return _dot(ya, wpa_ref[...]), _dot(yr, wpb_ref[...])

        def consume(issued):
            pa, pr = issued
            mg = mg_ref[0, rs, :]
            m = mg[:, :D_MODEL].astype(F32) * pa + mg[:, D_MODEL:].astype(F32) * pr
            m_scr[rs, :] = m.astype(BF16)
        return issue, consume

    def out_task(r):
        rs = slice(r * sub, (r + 1) * sub)

        def issue():
            return _dot(m_scr[rs, :], wout_ref[...])

        def consume(s):
            h = DEEPNORM_ALPHA * x_ref[0, rs, :] + gate_ref[0] * s
            mu = jnp.mean(h, axis=-1, keepdims=True)
            cen = h - mu
            var = jnp.mean(cen * cen, axis=-1, keepdims=True)
            o_ref[0, rs, :] = cen * lax.rsqrt(var + LN_EPS) * lng_ref[...] + lnb_ref[...]
        return issue, consume

    n_sub = rows // sub
    tasks = [branch_task(0)]
    for r in range(n_sub):
        if r + 1 < n_sub:
            tasks.append(branch_task(r + 1))
        tasks.append(out_task(r))
    lag = 1 if n_sub > 1 else 0
    pending = []
    for issue, consume in tasks:
        pending.append((consume, issue()))
        if len(pending) > lag:
            cons, res = pending.pop(0)
            cons(res)
    for cons, res in pending:
        cons(res)


def _outproj_call(o_a, ga, o_r, gr, mg, x, gate1p, nw, w_pa, w_pb, w_out, ln_g, ln_b):
    B, T, D = x.shape
    rows = min(OUT_ROWS, T)
    hspec = pl.BlockSpec((1, A_HEADS, rows, LANES), lambda b, t: (b, 0, t, 0))
    return pl.pallas_call(
        _outproj_kernel,
        grid=(B, T // rows),
        in_specs=[hspec, hspec, hspec, hspec,
                  pl.BlockSpec((1, rows, 2 * D), lambda b, t: (b, t, 0)),
                  pl.BlockSpec((1, rows, D), lambda b, t: (b, t, 0)),
                  pl.BlockSpec((1, 1, D), lambda b, t: (b, 0, 0)),
                  _const_spec((1, LANES)),
                  _const_spec((D, D)), _const_spec((D, D)), _const_spec((D, D)),
                  _const_spec((1, D)), _const_spec((1, D))],
        out_specs=pl.BlockSpec((1, rows, D), lambda b, t: (b, t, 0)),
        out_shape=jax.ShapeDtypeStruct((B, T, D), F32),
        scratch_shapes=[pltpu.VMEM((rows, D), BF16)],
        compiler_params=pltpu.CompilerParams(
            dimension_semantics=("arbitrary", "arbitrary"), vmem_limit_bytes=VMEM_LIMIT),
        name="outproj",
    )(o_a, ga, o_r, gr, mg, x, gate1p, nw, w_pa, w_pb, w_out, ln_g, ln_b)


def _rope_tables(T):
    half = R_DK // 2
    pos = jnp.arange(T, dtype=F32)
    inv = ROPE_BASE ** (-jnp.arange(0, R_DK, 2, dtype=F32) / R_DK)
    ang = pos[:, None] * inv[None, :]
    cos, sin = jnp.cos(ang), jnp.sin(ang)
    return jnp.tile(cos, (1, LANES // half)), jnp.concatenate([-sin, -sin, sin, sin], axis=1)


def _trunk(x, ada, params):
    B, T, D = x.shape
    cos_p, sin_p = _rope_tables(T)
    for l in range(DEPTH):
        p = params[l]
        shift = ada[l, :, :D].reshape(B, 1, D)
        sc1p = (1.0 + ada[l, :, D:2 * D]).reshape(B, 1, D)
        gate1p = (1.0 + ada[l, :, 2 * D:]).reshape(B, 1, D)
        qa, lff, kf, lfb, kb, va, ga, qr, kr, vr, gr, mg, dstat, mstat = _inproj_call(
            x, sc1p, shift, p["w_in"], p["w_rope"], p["w_mg"], p["b_mg"], p["lbm"], p["oml"],
            cos_p, sin_p)
        o_a = _hgrn_call(qa, lff, kf, lfb, kb, va, dstat, mstat)
        o_r = _ret_call(qr, kr, vr, p["lg_rows"])
        x = _outproj_call(o_a, ga, o_r, gr, mg, x, gate1p, p["nw"], p["w_pa"], p["w_pb"], p["w_out"],
                          p["ln_g"], p["ln_b"])
    return x


def kernel(x_prompt, x_sample, c_prompt, c_sample, w_ada, b_ada, w_in, hgrn_lb, a_norm_w, ret_decay,
           w_pa, w_pb, w_mg, b_mg, w_out, ln_g, ln_b):
    D = D_MODEL
    pr = jax.nn.softmax(hgrn_lb.astype(F32), axis=0)
    lbs = jnp.cumsum(pr, axis=0) - pr[0:1]
    params = []
    for l in range(DEPTH):
        lb = lbs[l]
        lbm = jnp.maximum(lb, LOG_FLOOR)
        lg = jax.nn.log_sigmoid(ret_decay[l].astype(F32))
        lg_rows = jnp.broadcast_to(
            lg.reshape(2, R_PAIRS, 2).transpose(1, 0, 2).reshape(R_PAIRS, 4, 1), (R_PAIRS, 4, LANES))
        half = R_DK // 2
        w_rope = w_in[l][:, OFF_RQ:OFF_RV].reshape(D, 2, R_PAIRS, 2, 2, half)
        w_rope = w_rope.transpose(0, 1, 2, 4, 3, 5).reshape(D, 2 * R_HEADS * R_DK)
        params.append(dict(
            w_in=w_in[l].astype(BF16), w_rope=w_rope.astype(BF16), w_mg=w_mg[l].astype(BF16),
            w_pa=w_pa[l].astype(BF16), w_pb=w_pb[l].astype(BF16), w_out=w_out[l].astype(BF16),
            b_mg=b_mg[l].reshape(1, 2 * D), lbm=lbm.reshape(1, D), oml=(1.0 - lb).reshape(1, D),
            nw=a_norm_w[l].reshape(1, A_DV), lg_rows=lg_rows,
            ln_g=ln_g[l].reshape(1, D), ln_b=ln_b[l].reshape(1, D)))
    nbp = x_prompt.shape[0]
    ada = _ada_call(jnp.concatenate([c_prompt, c_sample], axis=0), w_ada, b_ada)
    y_prompt = _trunk(x_prompt, ada[:, :nbp], params)
    y_sample = _trunk(x_sample, ada[:, nbp:], params)
    return (y_prompt, y_sample)
```
